```python
import math
import jax
import jax.numpy as jnp
from jax import lax
import numpy as np

D_MODEL = 1024
BATCH = 16
SEQ = 2048
DEPTH = 2

CTX_LEN = 256
GRID_W = 64
HG_HEAD_DIM = 128
HG_HEADS = D_MODEL // HG_HEAD_DIM
HG_WIDTH = HG_HEADS * HG_HEAD_DIM
HG_CHUNK = 16
S5_WIDTH = D_MODEL // 2
S5_GROUP = 16
S5_GROUPS = S5_WIDTH // S5_GROUP
S5_STATE = 64
S5_MAX_RE = -1e-4
FFN_DIM = 128 * ((8 * D_MODEL) // (3 * 128))
CONV_W = 3
RMS_EPS = 1e-6
HG_COLS = 5 * HG_WIDTH
S5_OFF = HG_COLS
GATE_OFF = S5_OFF + S5_WIDTH
IN_COLS = GATE_OFF + 2 * D_MODEL

kernel_name = "hybrid_hgrn2_s5_convffn_prefix"


def rms_norm(x, g):
    xf = x.astype(jnp.float32)
    y = xf * lax.rsqrt(jnp.mean(xf * xf, axis=-1, keepdims=True) + RMS_EPS)
    return (y * g.astype(jnp.float32)).astype(x.dtype)


def modulate(x, g, shift, scale):
    return rms_norm(x, g) * (1 + scale) + shift


def ada_mod(cond, w, b):
    return jnp.split(jax.nn.silu(cond) @ w + b, 6, axis=-1)


def split_heads(t):
    bsz, length, _ = t.shape
    return t.reshape(bsz, length, HG_HEADS, HG_HEAD_DIM).transpose(0, 2, 1, 3).astype(jnp.float32)


def hgrn2_lower_bounds(lb_raw):
    p = jax.nn.softmax(lb_raw.astype(jnp.float32), axis=0)
    cs = jnp.cumsum(p, axis=0)
    return cs - cs[0:1]


def hgrn2_forget(z, lb):
    lb = lb.reshape(HG_HEADS, 1, HG_HEAD_DIM).astype(jnp.float32)
    log_f = jnp.logaddexp(jnp.log(lb), jnp.log1p(-lb) + jax.nn.log_sigmoid(z))
    k = (1 - lb) * jax.nn.sigmoid(-z)
    return k, log_f


def gla_chunked(q, k, v, log_f, s0):
    bsz, nh, seq_len, _ = q.shape
    dv = v.shape[-1]
    n_chunks = seq_len // HG_CHUNK

    def chunk(t):
        return t.reshape(bsz, nh, n_chunks, HG_CHUNK, t.shape[-1])

    q, k, v, log_f = chunk(q), chunk(k), chunk(v), chunk(log_f)
    b = jnp.cumsum(log_f, axis=3)
    b_end = b[:, :, :, -1:, :]
    lower = jnp.tril(jnp.ones((HG_CHUNK, HG_CHUNK), dtype=bool))[:, :, None]
    rel = jnp.where(lower, b[..., :, None, :] - b[..., None, :, :], -jnp.inf)
    scores = jnp.sum(q[..., :, None, :] * k[..., None, :, :] * jnp.exp(rel), axis=-1)
    o_intra = jnp.einsum('bhnts,bhnsv->bhntv', scores, v)
    q_in = q * jnp.exp(b)
    kv_chunk = jnp.einsum('bhnsd,bhnsv->bhndv', k * jnp.exp(b_end - b), v)
    decay_chunk = jnp.exp(b_end[:, :, :, 0, :])

    def step(s, xs):
        q_c, kv_c, dec_c = xs
        o_c = jnp.einsum('bhtd,bhdv->bhtv', q_c, s)
        return dec_c[..., None] * s + kv_c, o_c

    s_final, o_inter = lax.scan(step, s0, (jnp.moveaxis(q_in, 2, 0), jnp.moveaxis(kv_chunk, 2, 0), jnp.moveaxis(decay_chunk, 2, 0)))
    o = o_intra + jnp.moveaxis(o_inter, 0, 2)
    return o.reshape(bsz, nh, seq_len, dv), s_final


def hgrn2_branch(p_ctx, p_lat, lb, norm_g, w_br, with_ctx_out):
    def prep(p):
        qz, fz_f, fz_b, iz = [split_heads(p[..., j * HG_WIDTH:(j + 1) * HG_WIDTH]) for j in range(4)]
        return jax.nn.silu(qz), iz, hgrn2_forget(fz_f, lb[0]), hgrn2_forget(fz_b, lb[1])

    def readout(o, p):
        gz = p[..., 4 * HG_WIDTH:5 * HG_WIDTH]
        o = rms_norm(o, norm_g)
        bsz, _, length, _ = o.shape
        o = o.transpose(0, 2, 1, 3).reshape(bsz, length, HG_WIDTH).astype(p.dtype)
        return (o * jax.nn.silu(gz)) @ w_br

    def flip(t):
        return jnp.flip(t, axis=2)

    qc, vc, fc, bc = prep(p_ctx)
    ql, vl, fl, bl = prep(p_lat)
    s0 = jnp.zeros((p_ctx.shape[0], HG_HEADS, HG_HEAD_DIM, HG_HEAD_DIM), jnp.float32)
    oc_f, sc_f = gla_chunked(qc, fc[0], vc, fc[1], s0)
    oc_b, sc_b = gla_chunked(flip(qc), flip(bc[0]), flip(vc), flip(bc[1]), s0)
    ol_f, _ = gla_chunked(ql, fl[0], vl, fl[1], sc_f)
    ol_b, _ = gla_chunked(flip(ql), flip(bl[0]), flip(vl), flip(bl[1]), sc_b)
    y_lat = readout(ol_f + flip(ol_b), p_lat)
    y_ctx = readout(oc_f + flip(oc_b), p_ctx) if with_ctx_out else None
    return y_ctx, y_lat


def _affine_combine(e1, e2):
    a1r, a1i, b1r, b1i = e1
    a2r, a2i, b2r, b2i = e2
    return (a2r * a1r - a2i * a1i, a2r * a1i + a2i * a1r,
            a2r * b1r - a2i * b1i + b2r, a2r * b1i + a2i * b1r + b2i)


def s5_discretise(a_re, a_im, log_dt, b_re, b_im):
    a_re = jnp.minimum(a_re.astype(jnp.float32), S5_MAX_RE)
    a_im = a_im.astype(jnp.float32)
    dt = jnp.exp(log_dt.astype(jnp.float32))[:, None]
    mag = jnp.exp(dt * a_re)
    abar_re, abar_im = mag * jnp.cos(dt * a_im), mag * jnp.sin(dt * a_im)
    den = a_re * a_re + a_im * a_im
    nr, ni = abar_re - 1, abar_im
    f_re = ((nr * a_re + ni * a_im) / den)[..., None]
    f_im = ((ni * a_re - nr * a_im) / den)[..., None]
    b_re, b_im = b_re.astype(jnp.float32), b_im.astype(jnp.float32)
    return abar_re, abar_im, f_re * b_re - f_im * b_im, f_re * b_im + f_im * b_re


def s5_drive(u, bbar_re, bbar_im):
    bsz, length, _ = u.shape
    ug = u.reshape(bsz, length, S5_GROUPS, S5_GROUP)
    return jnp.einsum('blgc,gnc->blgn', ug, bbar_re), jnp.einsum('blgc,gnc->blgn', ug, bbar_im)


def s5_scan(abar_re, abar_im, bu_re, bu_im, x0_re, x0_im, reverse):
    length = bu_re.shape[1]
    a_re = jnp.broadcast_to(abar_re, (1, length) + abar_re.shape)
    a_im = jnp.broadcast_to(abar_im, (1, length) + abar_im.shape)
    ar, ai, br, bi = lax.associative_scan(_affine_combine, (a_re, a_im, bu_re, bu_im), reverse=reverse, axis=1)
    s_re = br + ar * x0_re[:, None] - ai * x0_im[:, None]
    s_im = bi + ar * x0_im[:, None] + ai * x0_re[:, None]
    last = 0 if reverse else -1
    return s_re, s_im, s_re[:, last], s_im[:, last]


def s5_branch(u_ctx, u_lat, a_re, a_im, log_dt, b_re, b_im, c_re, c_im, d, w_glu, w_br, with_ctx_out):
    uc, ul = u_ctx.astype(jnp.float32), u_lat.astype(jnp.float32)
    zero = jnp.zeros((u_ctx.shape[0], S5_GROUPS, S5_STATE), jnp.float32)
    ctx_states, lat_states = [], []
    for direction in range(2):
        abr, abi, bbr, bbi = s5_discretise(a_re[direction], a_im[direction], log_dt[direction], b_re, b_im)
        rev = direction == 1
        cr, ci, fr, fi = s5_scan(abr, abi, *s5_drive(uc, bbr, bbi), zero, zero, rev)
        lr, li, _, _ = s5_scan(abr, abi, *s5_drive(ul, bbr, bbi), fr, fi, rev)
        ctx_states.append((cr, ci))
        lat_states.append((lr, li))

    def readout(states, u, dtype):
        s_re = states[0][0] + states[1][0]
        s_im = states[0][1] + states[1][1]
        y = (jnp.einsum('blgn,gcn->blgc', s_re, c_re.astype(jnp.float32))
             - jnp.einsum('blgn,gcn->blgc', s_im, c_im.astype(jnp.float32)))
        y = y.reshape(u.shape) + d.astype(jnp.float32) * u
        y = jax.nn.gelu(y).astype(dtype)
        return (y * jax.nn.sigmoid(y @ w_glu)) @ w_br

    y_lat = readout(lat_states, ul, u_lat.dtype)
    y_ctx = readout(ctx_states, uc, u_ctx.dtype) if with_ctx_out else None
    return y_ctx, y_lat


def merge_branches(p, y_hg, y_s5, w_out):
    g_hg = p[..., GATE_OFF:GATE_OFF + D_MODEL]
    g_s5 = p[..., GATE_OFF + D_MODEL:IN_COLS]
    return (jax.nn.sigmoid(g_hg) * y_hg + jax.nn.sigmoid(g_s5) * y_s5) @ w_out


def conv_ffn(h, rows, w_up, conv_w, conv_b, w_down):
    bsz, length, _ = h.shape
    up = h @ w_up
    ch = up.shape[-1]
    grid = up.reshape(bsz, rows, length // rows, ch)
    y = lax.conv_general_dilated(grid, conv_w[:, :, None, :], (1, 1), 'SAME',
                                 dimension_numbers=('NHWC', 'HWIO', 'NHWC'), feature_group_count=ch)
    y = (y + conv_b).reshape(bsz, length, ch)
    a, b = jnp.split(y, 2, axis=-1)
    return (jax.nn.silu(a) * b) @ w_down


def setup_inputs(seed: int = 0) -> dict:
    key = jax.random.key(seed)
    ks = jax.random.split(key, 32)
    f32 = jnp.float32

    def nrm(k, shape, scale):
        return jax.random.normal(k, shape, f32) * scale

    n_idx = jnp.arange(S5_STATE, dtype=f32)
    return {
        'x': nrm(ks[0], (BATCH, SEQ, D_MODEL), 1.0),
        'c': nrm(ks[1], (BATCH, D_MODEL), 1.0),
        'ctx': nrm(ks[2], (BATCH, CTX_LEN, D_MODEL), 1.0),
        'c_ctx': nrm(ks[3], (D_MODEL,), 1.0),
        'ada_w': nrm(ks[4], (DEPTH, D_MODEL, 6 * D_MODEL), 0.5 * D_MODEL ** -0.5),
        'ada_b': nrm(ks[5], (DEPTH, 6 * D_MODEL), 0.02),
        'norm1_g': 1.0 + nrm(ks[6], (DEPTH, D_MODEL), 0.02),
        'w_in': nrm(ks[7], (DEPTH, D_MODEL, IN_COLS), D_MODEL ** -0.5),
        'hg_lb_raw': nrm(ks[8], (DEPTH, 2, HG_WIDTH), 0.5),
        'hg_norm_g': 1.0 + nrm(ks[9], (DEPTH, HG_HEAD_DIM), 0.02),
        'w_hg_br': nrm(ks[10], (DEPTH, HG_WIDTH, D_MODEL), HG_WIDTH ** -0.5),
        's5_a_re': -0.5 + nrm(ks[11], (DEPTH, 2, S5_GROUPS, S5_STATE), 0.01),
        's5_a_im': math.pi * n_idx + nrm(ks[12], (DEPTH, 2, S5_GROUPS, S5_STATE), 0.01),
        's5_log_dt': jax.random.uniform(ks[13], (DEPTH, 2, S5_GROUPS), f32, math.log(1e-3), math.log(1e-1)),
        's5_b_re': nrm(ks[14], (DEPTH, S5_GROUPS, S5_STATE, S5_GROUP), (2 * S5_GROUP) ** -0.5),
        's5_b_im': nrm(ks[15], (DEPTH, S5_GROUPS, S5_STATE, S5_GROUP), (2 * S5_GROUP) ** -0.5),
        's5_c_re': nrm(ks[16], (DEPTH, S5_GROUPS, S5_GROUP, S5_STATE), S5_STATE ** -0.5),
        's5_c_im': nrm(ks[17], (DEPTH, S5_GROUPS, S5_GROUP, S5_STATE), S5_STATE ** -0.5),
        's5_d': nrm(ks[18], (DEPTH, S5_WIDTH), 1.0),
        'w_s5_glu': nrm(ks[19], (DEPTH, S5_WIDTH, S5_WIDTH), S5_WIDTH ** -0.5),
        'w_s5_br': nrm(ks[20], (DEPTH, S5_WIDTH, D_MODEL), S5_WIDTH ** -0.5),
        'w_out': nrm(ks[21], (DEPTH, D_MODEL, D_MODEL), D_MODEL ** -0.5),
        'norm2_g': 1.0 + nrm(ks[22], (DEPTH, D_MODEL), 0.02),
        'w_up': nrm(ks[23], (DEPTH, D_MODEL, 2 * FFN_DIM), D_MODEL ** -0.5),
        'ffn_conv_w': nrm(ks[24], (DEPTH, CONV_W, CONV_W, 2 * FFN_DIM), (CONV_W * CONV_W) ** -0.5),
        'ffn_conv_b': nrm(ks[25], (DEPTH, 2 * FFN_DIM), 0.02),
        'w_down': nrm(ks[26], (DEPTH, FFN_DIM, D_MODEL), FFN_DIM ** -0.5),
        'final_g': 1.0 + nrm(ks[27], (D_MODEL,), 0.02),
    }


def reference(x, c, ctx, c_ctx, ada_w, ada_b, norm1_g, w_in, hg_lb_raw, hg_norm_g, w_hg_br,
              s5_a_re, s5_a_im, s5_log_dt, s5_b_re, s5_b_im, s5_c_re, s5_c_im, s5_d,
              w_s5_glu, w_s5_br, w_out, norm2_g, w_up, ffn_conv_w, ffn_conv_b, w_down, final_g):
    rows = x.shape[1] // GRID_W
    lbs = hgrn2_lower_bounds(hg_lb_raw)
    h_lat, h_ctx = x, ctx
    for l in range(DEPTH):
        with_ctx = l < DEPTH - 1
        ml = [m[:, None, :] for m in ada_mod(c, ada_w[l], ada_b[l])]
        mc = ada_mod(c_ctx, ada_w[l], ada_b[l])
        p_lat = modulate(h_lat, norm1_g[l], ml[0], ml[1]) @ w_in[l]
        p_ctx = modulate(h_ctx, norm1_g[l], mc[0], mc[1]) @ w_in[l]
        yh_c, yh_l = hgrn2_branch(p_ctx[..., :HG_COLS], p_lat[..., :HG_COLS], lbs[l], hg_norm_g[l], w_hg_br[l], with_ctx)
        ys_c, ys_l = s5_branch(p_ctx[..., S5_OFF:GATE_OFF], p_lat[..., S5_OFF:GATE_OFF],
                               s5_a_re[l], s5_a_im[l], s5_log_dt[l], s5_b_re[l], s5_b_im[l],
                               s5_c_re[l], s5_c_im[l], s5_d[l], w_s5_glu[l], w_s5_br[l], with_ctx)
        h_lat = h_lat + ml[2] * merge_branches(p_lat, yh_l, ys_l, w_out[l])
        if with_ctx:
            h_ctx = h_ctx + mc[2] * merge_branches(p_ctx, yh_c, ys_c, w_out[l])
        n_lat = modulate(h_lat, norm2_g[l], ml[3], ml[4])
        h_lat = h_lat + ml[5] * conv_ffn(n_lat, rows, w_up[l], ffn_conv_w[l], ffn_conv_b[l], w_down[l])
        if with_ctx:
            n_ctx = modulate(h_ctx, norm2_g[l], mc[3], mc[4])
            h_ctx = h_ctx + mc[5] * conv_ffn(n_ctx, 1, w_up[l], ffn_conv_w[l], ffn_conv_b[l], w_down[l])
    return rms_norm(h_lat, final_g)
```

```python
import functools
import math

import numpy as np
import jax
import jax.numpy as jnp
from jax import lax
from jax.experimental import pallas as pl
from jax.experimental.pallas import tpu as pltpu

GRID_W = 64
RMS_EPS = 1e-6
S5_MAX_RE = -1e-4
S5_T = 16
GLA_C = 128
TOK_TILE = 256
VMEM_LIMIT = 56 * 1024 * 1024

F32 = jnp.float32
BF16 = jnp.bfloat16
_HI = lax.Precision.HIGHEST


def _nt_dot(a, b):
    return lax.dot_general(a, b, (((1,), (1,)), ((), ())), preferred_element_type=F32)


def _tn_dot(a, b):
    return lax.dot_general(a, b, (((0,), (0,)), ((), ())), preferred_element_type=F32)


def _dot(a, b):
    return jnp.dot(a, b, preferred_element_type=F32)


def _sigmoid(x):
    return 1.0 / (1.0 + jnp.exp(-x))


def _const_spec(shape):
    nd = len(shape)
    return pl.BlockSpec(shape, lambda *_: (0,) * nd, pipeline_mode=pl.Buffered(1))


def _ada_kernel(cond_ref, w_ref, b_ref, o_ref):
    cnd = cond_ref[...]
    s = cnd * _sigmoid(cnd)
    o_ref[0] = jnp.dot(s, w_ref[0], preferred_element_type=F32, precision=_HI) + b_ref[0]


def _ada_mod(cond, ada_w, ada_b):
    depth, d, n6 = ada_w.shape
    rows = cond.shape[0]
    tn = 512
    return pl.pallas_call(
        _ada_kernel,
        grid=(depth, n6 // tn),
        in_specs=[
            pl.BlockSpec((rows, d), lambda l, j: (0, 0)),
            pl.BlockSpec((1, d, tn), lambda l, j: (l, 0, j)),
            pl.BlockSpec((1, 1, tn), lambda l, j: (l, 0, j)),
        ],
        out_specs=pl.BlockSpec((1, rows, tn), lambda l, j: (l, 0, j)),
        out_shape=jax.ShapeDtypeStruct((depth, rows, n6), F32),
        name="ada_mod",
    )(cond, ada_w, ada_b.reshape(depth, 1, n6))


def _modulated_norm(x, g, shift, scale):
    ms = jnp.mean(x * x, axis=-1, keepdims=True)
    y = x * lax.rsqrt(ms + RMS_EPS) * g
    return y * (1.0 + scale) + shift


def _proj_in_kernel(hl_ref, hc_ref, mod_ref, g_ref, w_ref, *rest, widths, nt_lat):
    outs, xn_scr = rest[:len(widths)], rest[len(widths)]

    def norm(h_ref):
        xn_scr[...] = _modulated_norm(h_ref[0], g_ref[...], mod_ref[0, 0:1, :], mod_ref[0, 1:2, :]).astype(BF16)

    pl.when(pl.program_id(1) < nt_lat)(lambda: norm(hl_ref))
    pl.when(pl.program_id(1) >= nt_lat)(lambda: norm(hc_ref))
    xn = xn_scr[...]
    col = 0
    for o_ref, width in zip(outs, widths):
        step = math.gcd(512, width)
        for c in range(0, width, step):
            o_ref[0, :, c:c + step] = _dot(xn, w_ref[:, col + c:col + c + step]).astype(o_ref.dtype)
        col += width


def _proj_in(h_lat, h_ctx, mods, norm_g, w_bf16, widths):
    bsz, n_lat, d = h_lat.shape
    n_ctx = h_ctx.shape[1]
    nt_lat, nt_ctx = n_lat // TOK_TILE, n_ctx // TOK_TILE
    return pl.pallas_call(
        functools.partial(_proj_in_kernel, widths=tuple(widths), nt_lat=nt_lat),
        grid=(bsz, nt_lat + nt_ctx),
        in_specs=[
            pl.BlockSpec((1, TOK_TILE, d), lambda b, t: (b, jnp.minimum(t, nt_lat - 1), 0)),
            pl.BlockSpec((1, TOK_TILE, d), lambda b, t: (b, jnp.maximum(t - nt_lat, 0), 0)),
            pl.BlockSpec((1, 6, d), lambda b, t: (jnp.where(t < nt_lat, b, bsz), 0, 0)),
            pl.BlockSpec((1, d), lambda b, t: (0, 0)),
            _const_spec(w_bf16.shape),
        ],
        out_specs=[pl.BlockSpec((1, TOK_TILE, w), lambda b, t: (b, t, 0)) for w in widths],
        out_shape=[jax.ShapeDtypeStruct((bsz, n_lat + n_ctx, w), BF16) for w in widths],
        scratch_shapes=[pltpu.VMEM((TOK_TILE, d), BF16)],
        compiler_params=pltpu.CompilerParams(
            dimension_semantics=("parallel", "arbitrary"), vmem_limit_bytes=VMEM_LIMIT),
        name="proj_in",
    )(h_lat, h_ctx, mods, norm_g.reshape(1, d), w_bf16)


def _gla_tables(c, rev):
    t = np.arange(c)[:, None]
    s = np.arange(c)[None, :]
    x = t ^ s
    lev = np.full((c, c), -2, np.int32)
    causal = (t < s) if rev else (t > s)
    with np.errstate(divide="ignore"):
        hb = np.floor(np.log2(np.maximum(x, 1))).astype(np.int32)
    lev = np.where(causal, hb, lev)
    lev = np.where(t == s, -1, lev).astype(np.int32)
    tri = ((t <= s) if rev else (t >= s)).astype(np.float32)
    return jnp.asarray(lev), jnp.asarray(tri, dtype=BF16)


def _gla_kernel(*refs, rev, readout, c, nheads, dk):
    if readout:
        (q_ref, fz_ref, i_ref, lb_ref, lev_ref, tri_ref, g_ref, of_ref, ng_ref,
         out_ref, st_ref, b_scr) = refs
    else:
        q_ref, fz_ref, i_ref, lb_ref, lev_ref, tri_ref, out_ref, st_ref, b_scr = refs
    nlev = int(math.log2(c))

    @pl.when(pl.program_id(1) == 0)
    def _():
        st_ref[...] = jnp.zeros_like(st_ref)

    lev = lev_ref[...]
    tri = tri_ref[...]
    row = lax.broadcasted_iota(jnp.int32, (c, 1), 0)

    def head(hd, carry):
        sl = pl.ds(pl.multiple_of(hd * dk, dk), dk)
        z = fz_ref[0, :, sl].astype(F32)
        qz = q_ref[0, :, sl].astype(F32)
        v = i_ref[0, :, sl]
        lb = lb_ref[0:1, sl]

        e = jnp.exp(-jnp.abs(z))
        inv = 1.0 / (1.0 + e)
        logsig = jnp.minimum(z, 0.0) - jnp.log(1.0 + e)
        pos = z >= 0
        sig_pos = jnp.where(pos, 1.0, e) * inv
        sig_neg = jnp.where(pos, e, 1.0) * inv
        a = jnp.where(lb > 0, jnp.log(lb + (1.0 - lb) * sig_pos), logsig)
        k = (1.0 - lb) * sig_neg
        q = qz * _sigmoid(qz)

        a_hi = a.astype(BF16)
        a_lo = (a - a_hi.astype(F32)).astype(BF16)
        bb = _dot(tri, jnp.concatenate([a_hi, a_lo], axis=1))
        b = bb[:, :dk] + bb[:, dk:]
        b_scr[...] = b

        b_prev = pltpu.roll(b, 1, axis=0)
        b_next = pltpu.roll(b, c - 1, axis=0)
        scores = jnp.zeros((c, c), F32)
        for level in range(nlev):
            half = 1 << level
            if level == 0:
                r = row & 1
                ref_b = jnp.where(r == (0 if rev else 1), b_next if rev else b_prev, b)
            elif level == 1:
                r = row & 3
                b_prev2 = pltpu.roll(b, 2, axis=0)
                b_next2 = pltpu.roll(b, c - 2, axis=0)
                if rev:
                    ref_b = jnp.where(r == 0, b_next2, jnp.where(r == 1, b_next, jnp.where(r == 2, b, b_prev)))
                else:
                    ref_b = jnp.where(r == 0, b_next, jnp.where(r == 1, b, jnp.where(r == 2, b_prev, b_prev2)))
            else:
                blk = 2 * half
                pieces = []
                for j in range(c // blk):
                    brow = b_scr[pl.ds(j * blk + (half if rev else half - 1), 1), :]
                    pieces.append(jnp.broadcast_to(brow, (blk, dk)))
                ref_b = jnp.concatenate(pieces, axis=0) if len(pieces) > 1 else pieces[0]
            q_role = ((row & half) == 0) if rev else ((row & half) != 0)
            x = (jnp.where(q_role, q, k) * jnp.exp(-jnp.abs(b - ref_b))).astype(BF16)
            scores = jnp.where(lev == level, _nt_dot(x, x), scores)
        scores = jnp.where(lev == -1, jnp.sum(q * k, axis=-1, keepdims=True), scores)
        o = _dot(scores.astype(BF16), v)

        st = st_ref[hd]
        b_end = b_scr[pl.ds(0 if rev else c - 1, 1), :]
        o = o + _nt_dot((q * jnp.exp(b)).astype(BF16), st.astype(BF16))
        k_out = (k * jnp.exp(b_end - b)).astype(BF16)
        st_ref[hd] = jnp.exp(b_end) * st + _tn_dot(v, k_out)

        if readout:
            o = o + of_ref[0, :, sl].astype(F32)
            o = o * lax.rsqrt(jnp.mean(o * o, axis=-1, keepdims=True) + RMS_EPS) * ng_ref[...]
            gz = g_ref[0, :, sl].astype(F32)
            o = o * (gz * _sigmoid(gz))
        out_ref[0, :, sl] = o.astype(out_ref.dtype)
        return carry

    lax.fori_loop(0, nheads, head, 0)


def _gla(p_hg, lb_row, norm_g, o_fwd, n_lat, n_ctx, width, dk, rev):
    bsz, total, _ = p_hg.shape
    c = GLA_C
    nl, nc = n_lat // c, n_ctx // c
    nheads = width // dk
    readout = rev

    if rev:
        def chunk(i):
            return jnp.where(i < nc, nl + nc - 1 - i, nl + nc - 1 - i)
    else:
        def chunk(i):
            return jnp.where(i < nc, nl + i, i - nc)

    def col_spec(colblk):
        return pl.BlockSpec((1, c, width), lambda b, i: (b, chunk(i), colblk))

    lev, tri = _gla_tables(c, rev)
    in_specs = [col_spec(0), col_spec(2 if rev else 1), col_spec(3),
                pl.BlockSpec((1, width), lambda b, i: (0, 0)),
                pl.BlockSpec((c, c), lambda b, i: (0, 0)),
                pl.BlockSpec((c, c), lambda b, i: (0, 0))]
    args = [p_hg, p_hg, p_hg, lb_row.reshape(1, width), lev, tri]
    if readout:
        in_specs += [col_spec(4),
                     pl.BlockSpec((1, c, width), lambda b, i: (b, chunk(i), 0)),
                     pl.BlockSpec((1, dk), lambda b, i: (0, 0))]
        args += [p_hg, o_fwd, norm_g.reshape(1, dk)]
    return pl.pallas_call(
        functools.partial(_gla_kernel, rev=rev, readout=readout, c=c, nheads=nheads, dk=dk),
        grid=(bsz, nl + nc),
        in_specs=in_specs,
        out_specs=pl.BlockSpec((1, c, width), lambda b, i: (b, chunk(i), 0)),
        out_shape=jax.ShapeDtypeStruct((bsz, total, width), BF16),
        scratch_shapes=[pltpu.VMEM((nheads, dk, dk), F32), pltpu.VMEM((c, dk), F32)],
        compiler_params=pltpu.CompilerParams(
            dimension_semantics=("parallel", "arbitrary"), vmem_limit_bytes=VMEM_LIMIT),
        name="gla_bwd" if rev else "gla_fwd",
    )(*args)


def _s5_tables(a_re, a_im, log_dt, b_re, b_im, c_re, c_im):
    t = S5_T
    ngrp, nst, cg = b_re.shape
    a_re = jnp.minimum(a_re.astype(F32), S5_MAX_RE)
    a_im = a_im.astype(F32)
    dt = jnp.exp(log_dt.astype(F32))[..., None]
    mag = jnp.exp(dt * a_re)
    abr, abi = mag * jnp.cos(dt * a_im), mag * jnp.sin(dt * a_im)
    den = a_re * a_re + a_im * a_im
    nr, ni = abr - 1.0, abi
    f_re = ((nr * a_re + ni * a_im) / den)[..., None]
    f_im = ((ni * a_re - nr * a_im) / den)[..., None]
    b_re, b_im = b_re.astype(F32)[None], b_im.astype(F32)[None]
    bb_re = f_re * b_re - f_im * b_im
    bb_im = f_re * b_im + f_im * b_re
    pr, pi = [jnp.ones_like(abr)], [jnp.zeros_like(abr)]
    for _ in range(t):
        pr.append(pr[-1] * abr - pi[-1] * abi)
        pi.append(pr[-2] * abi + pi[-1] * abr)
    pr, pi = jnp.stack(pr), jnp.stack(pi)
    c_re, c_im = c_re.astype(F32), c_im.astype(F32)
    l_re = c_re[None, None] * pr[:, :, :, None, :] - c_im[None, None] * pi[:, :, :, None, :]
    l_im = c_re[None, None] * pi[:, :, :, None, :] + c_im[None, None] * pr[:, :, :, None, :]
    kern = (jnp.einsum("drgcn,rgnk->drgck", l_re[:t], bb_re, precision=_HI)
            - jnp.einsum("drgcn,rgnk->drgck", l_im[:t], bb_im, precision=_HI))
    lag = np.arange(t)[:, None] - np.arange(t)[None, :]
    kf = jnp.where((lag >= 0)[:, :, None, None, None], kern[np.clip(lag, 0, t - 1), 0], 0.0)
    kb = jnp.where((lag <= 0)[:, :, None, None, None], kern[np.clip(-lag, 0, t - 1), 1], 0.0)
    m = kf + kb
    mt = m.transpose(2, 1, 4, 0, 3).reshape(ngrp, t * cg, t * cg)
    pf_r, pf_i = pr[t - 1 - np.arange(t), 0], pi[t - 1 - np.arange(t), 0]
    pb_r, pb_i = pr[np.arange(t), 1], pi[np.arange(t), 1]

    def drive(p_r, p_i, r):
        g_r = p_r[..., None] * bb_re[r][None] - p_i[..., None] * bb_im[r][None]
        g_i = p_r[..., None] * bb_im[r][None] + p_i[..., None] * bb_re[r][None]
        to = lambda x: x.transpose(1, 0, 3, 2).reshape(ngrp, t * cg, nst)
        return to(g_r), to(g_i)

    gf_r, gf_i = drive(pf_r, pf_i, 0)
    gb_r, gb_i = drive(pb_r, pb_i, 1)
    def read(idx, r):
        rr = l_re[idx, r].transpose(1, 3, 0, 2).reshape(ngrp, nst, t * cg)
        ri = -l_im[idx, r].transpose(1, 3, 0, 2).reshape(ngrp, nst, t * cg)
        return rr, ri

    of_r, of_i = read(np.arange(t) + 1, 0)
    ob_r, ob_i = read(t - np.arange(t), 1)

    def pair_in(x):
        x = x.reshape(ngrp // 2, 2, t * cg, nst)
        z = jnp.zeros_like(x[:, 0])
        return jnp.concatenate([jnp.concatenate([x[:, 0], z], axis=2),
                                jnp.concatenate([z, x[:, 1]], axis=2)], axis=1)

    def pair_out(x):
        x = x.reshape(ngrp // 2, 2, nst, t * cg)
        z = jnp.zeros_like(x[:, 0])
        return jnp.concatenate([jnp.concatenate([x[:, 0], z], axis=2),
                                jnp.concatenate([z, x[:, 1]], axis=2)], axis=1)

    gin = jnp.concatenate([pair_in(gf_r), pair_in(gf_i), pair_in(gb_r), pair_in(gb_i)], axis=2)
    gout = jnp.concatenate([pair_out(of_r), pair_out(of_i), pair_out(ob_r), pair_out(ob_i)], axis=1)
    a_chunk = jnp.stack([pr[t, 0], pi[t, 0], pr[t, 1], pi[t, 1]]).reshape(4, ngrp * nst)
    return mt.astype(BF16), gin.astype(BF16), gout.astype(BF16), a_chunk


def _s5_kernel(z_ref, mt_ref, gin_ref, gout_ref, ac_ref, y_ref, d_scr, x_scr, *, n_lat, n_ctx, npairs, pw):
    nrows = n_lat + n_ctx
    gw = pw // 2
    tw = pw // 4
    for p in range(npairs):
        d = _dot(z_ref[0, :, p * pw:(p + 1) * pw], gin_ref[p])
        for j in range(4):
            d_scr[j, :, p * tw:(p + 1) * tw] = d[:, j * tw:(j + 1) * tw]

    def scan(plane, forward):
        ar = ac_ref[plane:plane + 1, :]
        ai = ac_ref[plane + 1:plane + 2, :]

        def body(i, carry):
            xr, xi = carry
            if forward:
                n = jnp.where(i < n_ctx, n_lat + i, i - n_ctx)
            else:
                n = nrows - 1 - i
            x_scr[plane, pl.ds(n, 1), :] = xr
            x_scr[plane + 1, pl.ds(n, 1), :] = xi
            dr = d_scr[plane, pl.ds(n, 1), :]
            di = d_scr[plane + 1, pl.ds(n, 1), :]
            return ar * xr - ai * xi + dr, ar * xi + ai * xr + di

        zero = jnp.zeros((1, d_scr.shape[2]), F32)
        lax.fori_loop(0, nrows, body, (zero, zero))

    scan(0, True)
    scan(2, False)

    for p in range(npairs):
        xp = jnp.concatenate([x_scr[j, :, p * tw:(p + 1) * tw] for j in range(4)], axis=1).astype(BF16)
        y = _dot(xp, gout_ref[p])
        for g in range(2):
            lo = p * pw + g * gw
            yg = y[:, g * gw:(g + 1) * gw] + _dot(z_ref[0, :, lo:lo + gw], mt_ref[2 * p + g])
            y_ref[0, :, lo:lo + gw] = yg.astype(y_ref.dtype)


def _s5(zf, mt, gin, gout, a_chunk, n_lat_chunks, n_ctx_chunks):
    bsz, nrows, wide = zf.shape
    npairs, pw, _ = gin.shape
    planes = a_chunk.shape[1]
    return pl.pallas_call(
        functools.partial(_s5_kernel, n_lat=n_lat_chunks, n_ctx=n_ctx_chunks, npairs=npairs, pw=pw),
        grid=(bsz,),
        in_specs=[pl.BlockSpec((1, nrows, wide), lambda b: (b, 0, 0)),
                  _const_spec(mt.shape), _const_spec(gin.shape), _const_spec(gout.shape),
                  _const_spec(a_chunk.shape)],
        out_specs=pl.BlockSpec((1, nrows, wide), lambda b: (b, 0, 0)),
        out_shape=jax.ShapeDtypeStruct((bsz, nrows, wide), BF16),
        scratch_shapes=[pltpu.VMEM((4, nrows, planes), F32), pltpu.VMEM((4, nrows, planes), F32)],
        compiler_params=pltpu.CompilerParams(
            dimension_semantics=("parallel",), vmem_limit_bytes=VMEM_LIMIT),
        name="s5",
    )(zf, mt, gin, gout, a_chunk)


def _s5_flatten(u, ngrp, cg):
    bsz, total, _ = u.shape
    n = total // S5_T
    return u.reshape(bsz, n, S5_T, ngrp, cg).transpose(0, 1, 3, 2, 4).reshape(bsz, n, ngrp * S5_T * cg)


def _s5_unflatten(yz, ngrp, cg):
    bsz, n, _ = yz.shape
    return yz.reshape(bsz, n, ngrp, S5_T, cg).transpose(0, 1, 3, 2, 4).reshape(bsz, n * S5_T, ngrp * cg)


def _gelu_tanh(x):
    return 0.5 * x * (1.0 + jnp.tanh(math.sqrt(2.0 / math.pi) * (x + 0.044715 * (x * x * x))))


def _merge_kernel(h_ref, yh_ref, ypre_ref, u_ref, gate_ref, mod_ref, d_ref,
                  whg_ref, wglu_ref, wbr_ref, wout_ref, o_ref):
    d = h_ref.shape[2]
    y_hg = _dot(yh_ref[0], whg_ref[...])
    ys = _gelu_tanh(ypre_ref[0].astype(F32) + d_ref[...] * u_ref[0].astype(F32))
    glu = _sigmoid(_dot(ys.astype(BF16), wglu_ref[...]))
    y_s5 = _dot((ys * glu).astype(BF16), wbr_ref[...])
    gate = gate_ref[0].astype(F32)
    merged = _sigmoid(gate[:, :d]) * y_hg + _sigmoid(gate[:, d:]) * y_s5
    o_ref[0] = h_ref[0] + mod_ref[0, 2:3, :] * _dot(merged.astype(BF16), wout_ref[...])


def _merge(h, yh, ypre, p_u, p_gate, mods, mod_row_fn, s5_d, whg, wglu, wbr, wout, tok_off_blocks):
    bsz, length, d = h.shape
    sw = p_u.shape[2]
    nt = length // TOK_TILE
    seq = lambda w: pl.BlockSpec((1, TOK_TILE, w), lambda b, t: (b, t + tok_off_blocks, 0))
    return pl.pallas_call(
        _merge_kernel,
        grid=(bsz, nt),
        in_specs=[pl.BlockSpec((1, TOK_TILE, d), lambda b, t: (b, t, 0)),
                  seq(yh.shape[2]), seq(sw), seq(sw), seq(p_gate.shape[2]),
                  pl.BlockSpec((1, 6, d), lambda b, t: (mod_row_fn(b), 0, 0)),
                  pl.BlockSpec((1, sw), lambda b, t: (0, 0)),
                  _const_spec(whg.shape), _const_spec(wglu.shape), _const_spec(wbr.shape),
                  _const_spec(wout.shape)],
        out_specs=pl.BlockSpec((1, TOK_TILE, d), lambda b, t: (b, t, 0)),
        out_shape=jax.ShapeDtypeStruct(h.shape, F32),
        input_output_aliases={0: 0},
        compiler_params=pltpu.CompilerParams(
            dimension_semantics=("parallel", "parallel"), vmem_limit_bytes=VMEM_LIMIT),
        name="merge",
    )(h, yh, ypre, p_u, p_gate, mods, s5_d.reshape(1, sw), whg, wglu, wbr, wout)


def _ffn_kernel(h_ref, mod_ref, g_ref, wa_ref, wb_ref, cwa_ref, cwb_ref, cba_ref, cbb_ref, wd_ref, fg_ref,
                o_ref, xn_scr, up_scr, act_scr, *, n_tok, width, pad, cw, strip, mrows, final):
    j = pl.program_id(1)
    nj = pl.num_programs(1)
    vertical = n_tok > width
    nblk = n_tok // mrows

    @pl.when(j == 0)
    def _():
        def norm_body(i, carry):
            r0 = pl.multiple_of(i * mrows, mrows)
            xn_scr[pl.ds(r0, mrows), :] = _modulated_norm(
                h_ref[0, pl.ds(r0, mrows), :], g_ref[...], mod_ref[0, 3:4, :], mod_ref[0, 4:5, :]).astype(BF16)
            o_ref[0, pl.ds(r0, mrows), :] = jnp.zeros((mrows, o_ref.shape[2]), F32)
            return carry

        lax.fori_loop(0, nblk, norm_body, 0)
        zero = jnp.zeros((pad, 2 * cw), F32)
        up_scr[0:pad, :] = zero
        up_scr[pad + n_tok:pad + n_tok + pad, :] = zero

    def up_body(i, carry):
        r0 = pl.multiple_of(i * mrows, mrows)
        xn = xn_scr[pl.ds(r0, mrows), :]
        up_scr[pl.ds(pad + r0, mrows), 0:cw] = _dot(xn, wa_ref[...])
        up_scr[pl.ds(pad + r0, mrows), cw:2 * cw] = _dot(xn, wb_ref[...])
        return carry

    lax.fori_loop(0, nblk, up_body, 0)

    def conv(r0, col, lane0, cw_ref, cb_ref, w_lane0):
        sums = [None, None, None]
        for dr in ((-1, 0, 1) if vertical else (0,)):
            win = up_scr[pl.ds(pl.multiple_of(pad + r0 + dr * width, 8), strip), lane0:lane0 + 128]
            shifted = (pltpu.roll(win, 1, axis=0), win, pltpu.roll(win, strip - 1, axis=0))
            for dc in (-1, 0, 1):
                tap = (dr + 1) * 3 + (dc + 1)
                term = cw_ref[tap:tap + 1, w_lane0:w_lane0 + 128] * shifted[dc + 1]
                sums[dc + 1] = term if sums[dc + 1] is None else sums[dc + 1] + term
        left = jnp.where(col == 0, 0.0, sums[0])
        right = jnp.where(col == width - 1, 0.0, sums[2])
        return (sums[1] + cb_ref[:, w_lane0:w_lane0 + 128]) + (left + right)

    def strip_body(s, carry):
        r0 = pl.multiple_of(s * strip, strip)
        col = (lax.broadcasted_iota(jnp.int32, (strip, 1), 0) + r0) % width
        for lt in range(cw // 128):
            ca = conv(r0, col, lt * 128, cwa_ref, cba_ref, lt * 128)
            cb = conv(r0, col, cw + lt * 128, cwb_ref, cbb_ref, lt * 128)
            act_scr[pl.ds(r0, strip), lt * 128:(lt + 1) * 128] = (ca * _sigmoid(ca) * cb).astype(BF16)
        return carry

    lax.fori_loop(0, n_tok // strip, strip_body, 0)

    def down_body(i, carry):
        r0 = pl.multiple_of(i * mrows, mrows)
        o_ref[0, pl.ds(r0, mrows), :] += _dot(act_scr[pl.ds(r0, mrows), :], wd_ref[...])
        return carry

    lax.fori_loop(0, nblk, down_body, 0)

    @pl.when(j == nj - 1)
    def _():
        def fin_body(i, carry):
            r0 = pl.multiple_of(i * mrows, mrows)
            out = h_ref[0, pl.ds(r0, mrows), :] + mod_ref[0, 5:6, :] * o_ref[0, pl.ds(r0, mrows), :]
            if final:
                out = out * lax.rsqrt(jnp.mean(out * out, axis=-1, keepdims=True) + RMS_EPS) * fg_ref[...]
            o_ref[0, pl.ds(r0, mrows), :] = out
            return carry

        lax.fori_loop(0, nblk, fin_body, 0)


def _ffn(h, mods, mod_row_fn, norm_g, w_up, conv_w, conv_b, w_down, final_g, width, final):
    bsz, n_tok, d = h.shape
    ffn = w_down.shape[0]
    cw = 384 if ffn % 384 == 0 else 128
    nj = ffn // cw
    pad = width
    strip = width
    cwf = conv_w.reshape(9, 2 * ffn)
    cbf = conv_b.reshape(1, 2 * ffn)
    return pl.pallas_call(
        functools.partial(_ffn_kernel, n_tok=n_tok, width=width, pad=pad, cw=cw, strip=strip,
                          mrows=TOK_TILE, final=final),
        grid=(bsz, nj),
        in_specs=[pl.BlockSpec((1, n_tok, d), lambda b, j: (b, 0, 0), pipeline_mode=pl.Buffered(1)),
                  pl.BlockSpec((1, 6, d), lambda b, j: (mod_row_fn(b), 0, 0)),
                  pl.BlockSpec((1, d), lambda b, j: (0, 0)),
                  pl.BlockSpec((d, cw), lambda b, j: (0, j)),
                  pl.BlockSpec((d, cw), lambda b, j: (0, nj + j)),
                  pl.BlockSpec((9, cw), lambda b, j: (0, j)),
                  pl.BlockSpec((9, cw), lambda b, j: (0, nj + j)),
                  pl.BlockSpec((1, cw), lambda b, j: (0, j)),
                  pl.BlockSpec((1, cw), lambda b, j: (0, nj + j)),
                  pl.BlockSpec((cw, d), lambda b, j: (j, 0)),
                  pl.BlockSpec((1, d), lambda b, j: (0, 0))],
        out_specs=pl.BlockSpec((1, n_tok, d), lambda b, j: (b, 0, 0)),
        out_shape=jax.ShapeDtypeStruct(h.shape, F32),
        scratch_shapes=[pltpu.VMEM((n_tok, d), BF16),
                        pltpu.VMEM((n_tok + 2 * pad, 2 * cw), F32),
                        pltpu.VMEM((n_tok, cw), BF16)],
        input_output_aliases={0: 0},
        compiler_params=pltpu.CompilerParams(
            dimension_semantics=("parallel", "arbitrary"), vmem_limit_bytes=VMEM_LIMIT),
        name="ffn",
    )(h, mods, norm_g.reshape(1, d), w_up, w_up, cwf, cwf, cbf, cbf, w_down, final_g.reshape(1, d))


def _lower_bounds(lb_raw):
    p = jax.nn.softmax(lb_raw.astype(F32), axis=0)
    cs = jnp.cumsum(p, axis=0)
    return cs - cs[0:1]


def kernel(x, c, ctx, c_ctx, ada_w, ada_b, norm1_g, w_in, hg_lb_raw, hg_norm_g, w_hg_br, s5_a_re, s5_a_im, s5_log_dt, s5_b_re, s5_b_im, s5_c_re, s5_c_im, s5_d, w_s5_glu, w_s5_br, w_out, norm2_g, w_up, ffn_conv_w, ffn_conv_b, w_down, final_g):
    bsz, n_lat, d = x.shape
    n_ctx = ctx.shape[1]
    depth = ada_w.shape[0]
    dk = hg_norm_g.shape[1]
    hg_w = hg_lb_raw.shape[2]
    ngrp, nst, cg = s5_b_re.shape[1:]
    s5_w = ngrp * cg
    total = n_lat + n_ctx
    widths = (5 * hg_w, s5_w, 2 * d)
    assert n_lat % TOK_TILE == 0 and n_ctx % TOK_TILE == 0 and n_lat % GLA_C == 0 and n_ctx % GLA_C == 0
    assert sum(widths) == w_in.shape[2] and S5_T * cg == 256 and ngrp % 2 == 0

    rows = ((bsz + 1 + 7) // 8) * 8
    cond = jnp.zeros((rows, d), F32).at[:bsz].set(c).at[bsz].set(c_ctx)
    mods = _ada_mod(cond, ada_w, ada_b).reshape(depth, rows, 6, d)
    lat_row = lambda b: b
    ctx_row = lambda b: bsz
    lbs = _lower_bounds(hg_lb_raw)
    lat_blk = n_lat // TOK_TILE

    h_lat, h_ctx = x, ctx
    for l in range(depth):
        last = l == depth - 1
        w_in_l = w_in[l].astype(BF16)
        p_hg, p_u, p_gate = _proj_in(h_lat, h_ctx, mods[l], norm1_g[l], w_in_l, widths)

        o_fwd = _gla(p_hg, lbs[l, 0], hg_norm_g[l], None, n_lat, n_ctx, hg_w, dk, rev=False)
        yh = _gla(p_hg, lbs[l, 1], hg_norm_g[l], o_fwd, n_lat, n_ctx, hg_w, dk, rev=True)

        mt, gin, gout, a_chunk = _s5_tables(s5_a_re[l], s5_a_im[l], s5_log_dt[l], s5_b_re[l], s5_b_im[l],
                                            s5_c_re[l], s5_c_im[l])
        yz = _s5(_s5_flatten(p_u, ngrp, cg), mt, gin, gout, a_chunk, n_lat // S5_T, n_ctx // S5_T)
        ypre = _s5_unflatten(yz, ngrp, cg)

        wts = (s5_d[l], w_hg_br[l].astype(BF16), w_s5_glu[l].astype(BF16), w_s5_br[l].astype(BF16),
               w_out[l].astype(BF16))
        h_lat = _merge(h_lat, yh, ypre, p_u, p_gate, mods[l], lat_row, *wts, 0)
        ffn_w = (norm2_g[l], w_up[l].astype(BF16), ffn_conv_w[l], ffn_conv_b[l], w_down[l].astype(BF16), final_g)
        h_lat = _ffn(h_lat, mods[l], lat_row, *ffn_w, GRID_W, last)
        if not last:
            h_ctx = _merge(h_ctx, yh, ypre, p_u, p_gate, mods[l], ctx_row, *wts, lat_blk)
            h_ctx = _ffn(h_ctx, mods[l], ctx_row, *ffn_w, n_ctx, False)
    return h_lat
```

```python
import functools
import math

import numpy as np
import jax
import jax.numpy as jnp
from jax import lax
from jax.experimental import pallas as pl
from jax.experimental.pallas import tpu as pltpu

GRID_W = 64
RMS_EPS = 1e-6
S5_MAX_RE = -1e-4
S5_T = 16
GLA_C = 128
HEADS_PER_ITER = 2
TOK_TILE = 256
FFN_TILE = 256
VMEM_LIMIT = 56 * 1024 * 1024

F32 = jnp.float32
BF16 = jnp.bfloat16
_HI = lax.Precision.HIGHEST


def _nt_dot(a, b):
    return lax.dot_general(a, b, (((1,), (1,)), ((), ())), preferred_element_type=F32)


def _tn_dot(a, b):
    return lax.dot_general(a, b, (((0,), (0,)), ((), ())), preferred_element_type=F32)


def _dot(a, b):
    return jnp.dot(a, b, preferred_element_type=F32)


def _sigmoid(x):
    return 1.0 / (1.0 + jnp.exp(-x))


def _const_spec(shape):
    nd = len(shape)
    return pl.BlockSpec(shape, lambda *_: (0,) * nd, pipeline_mode=pl.Buffered(1))


def _layer_spec(stacked, layer):
    rest = stacked.shape[1:]
    return pl.BlockSpec((None,) + rest, lambda *_: (layer,) + (0,) * len(rest), pipeline_mode=pl.Buffered(1))


def _ada_kernel(cond_ref, w_ref, b_ref, o_ref):
    cnd = cond_ref[...]
    s = cnd * _sigmoid(cnd)
    o_ref[0] = jnp.dot(s, w_ref[0], preferred_element_type=F32, precision=_HI) + b_ref[0]


def _ada_mod(cond, ada_w, ada_b):
    depth, d, n6 = ada_w.shape
    rows = cond.shape[0]
    tn = 512
    return pl.pallas_call(
        _ada_kernel,
        grid=(depth, n6 // tn),
        in_specs=[
            pl.BlockSpec((rows, d), lambda l, j: (0, 0)),
            pl.BlockSpec((1, d, tn), lambda l, j: (l, 0, j)),
            pl.BlockSpec((1, 1, tn), lambda l, j: (l, 0, j)),
        ],
        out_specs=pl.BlockSpec((1, rows, tn), lambda l, j: (l, 0, j)),
        out_shape=jax.ShapeDtypeStruct((depth, rows, n6), F32),
        name="ada_mod",
    )(cond, ada_w, ada_b.reshape(depth, 1, n6))


def _modulated_norm(x, g, shift, scale):
    ms = jnp.mean(x * x, axis=-1, keepdims=True)
    y = x * lax.rsqrt(ms + RMS_EPS) * g
    return y * (1.0 + scale) + shift


def _proj_in_kernel(hl_ref, hc_ref, mod_ref, g_ref, w_ref, *rest, widths, nt_lat):
    outs, xn_scr = rest[:len(widths)], rest[len(widths)]

    def norm(h_ref):
        xn_scr[...] = _modulated_norm(h_ref[0], g_ref[...], mod_ref[0, 0:1, :], mod_ref[0, 1:2, :]).astype(BF16)

    pl.when(pl.program_id(1) < nt_lat)(lambda: norm(hl_ref))
    pl.when(pl.program_id(1) >= nt_lat)(lambda: norm(hc_ref))
    xn = xn_scr[...]
    col = 0
    for o_ref, width in zip(outs, widths):
        step = math.gcd(512, width)
        for c in range(0, width, step):
            o_ref[0, :, c:c + step] = _dot(xn, w_ref[:, col + c:col + c + step]).astype(o_ref.dtype)
        col += width


def _proj_in(h_lat, h_ctx, mods, norm_g, w_stack, layer, widths):
    bsz, n_lat, d = h_lat.shape
    n_ctx = h_ctx.shape[1]
    nt_lat, nt_ctx = n_lat // TOK_TILE, n_ctx // TOK_TILE
    return pl.pallas_call(
        functools.partial(_proj_in_kernel, widths=tuple(widths), nt_lat=nt_lat),
        grid=(bsz, nt_lat + nt_ctx),
        in_specs=[
            pl.BlockSpec((1, TOK_TILE, d), lambda b, t: (b, jnp.minimum(t, nt_lat - 1), 0)),
            pl.BlockSpec((1, TOK_TILE, d), lambda b, t: (b, jnp.maximum(t - nt_lat, 0), 0)),
            pl.BlockSpec((1, 6, d), lambda b, t: (jnp.where(t < nt_lat, b, bsz), 0, 0)),
            pl.BlockSpec((1, d), lambda b, t: (0, 0)),
            _layer_spec(w_stack, layer),
        ],
        out_specs=[pl.BlockSpec((1, TOK_TILE, w), lambda b, t: (b, t, 0)) for w in widths],
        out_shape=[jax.ShapeDtypeStruct((bsz, n_lat + n_ctx, w), BF16) for w in widths],
        scratch_shapes=[pltpu.VMEM((TOK_TILE, d), BF16)],
        compiler_params=pltpu.CompilerParams(
            dimension_semantics=("parallel", "arbitrary"), vmem_limit_bytes=VMEM_LIMIT),
        name="proj_in",
    )(h_lat, h_ctx, mods, norm_g.reshape(1, d), w_stack)


def _gla_tables(c, rev):
    t = np.arange(c)[:, None]
    s = np.arange(c)[None, :]
    x = t ^ s
    lev = np.full((c, c), -2, np.int32)
    causal = (t < s) if rev else (t > s)
    with np.errstate(divide="ignore"):
        hb = np.floor(np.log2(np.maximum(x, 1))).astype(np.int32)
    lev = np.where(causal, hb, lev)
    lev = np.where(t == s, -1, lev).astype(np.int32)
    tri = ((t <= s) if rev else (t >= s)).astype(np.float32)
    return jnp.asarray(lev), jnp.asarray(tri, dtype=BF16)


def _gla_kernel(*refs, rev, readout, c, nheads, dk):
    if readout:
        (q_ref, fz_ref, i_ref, lb_ref, lev_ref, tri_ref, g_ref, of_ref, ng_ref,
         out_ref, st_ref, b_scr_all) = refs
    else:
        q_ref, fz_ref, i_ref, lb_ref, lev_ref, tri_ref, out_ref, st_ref, b_scr_all = refs
    nlev = int(math.log2(c))

    @pl.when(pl.program_id(1) == 0)
    def _():
        st_ref[...] = jnp.zeros_like(st_ref)

    lev = lev_ref[...]
    tri = tri_ref[...]
    row = lax.broadcasted_iota(jnp.int32, (c, 1), 0)

    def head(hd, b_scr):
        sl = pl.ds(pl.multiple_of(hd * dk, dk), dk)
        z = fz_ref[0, :, sl].astype(F32)
        qz = q_ref[0, :, sl].astype(F32)
        v = i_ref[0, :, sl]
        lb = lb_ref[0:1, sl]

        e = jnp.exp(-jnp.abs(z))
        inv = 1.0 / (1.0 + e)
        logsig = jnp.minimum(z, 0.0) - jnp.log(1.0 + e)
        pos = z >= 0
        sig_pos = jnp.where(pos, 1.0, e) * inv
        sig_neg = jnp.where(pos, e, 1.0) * inv
        a = jnp.where(lb > 0, jnp.log(lb + (1.0 - lb) * sig_pos), logsig)
        k = (1.0 - lb) * sig_neg
        q = qz * _sigmoid(qz)

        a_hi = a.astype(BF16)
        a_lo = (a - a_hi.astype(F32)).astype(BF16)
        bb = _dot(tri, jnp.concatenate([a_hi, a_lo], axis=1))
        b = bb[:, :dk] + bb[:, dk:]
        b_scr[...] = b

        b_prev = pltpu.roll(b, 1, axis=0)
        b_next = pltpu.roll(b, c - 1, axis=0)
        scores = jnp.zeros((c, c), F32)
        for level in range(nlev):
            half = 1 << level
            if level == 0:
                r = row & 1
                ref_b = jnp.where(r == (0 if rev else 1), b_next if rev else b_prev, b)
            elif level == 1:
                r = row & 3
                b_prev2 = pltpu.roll(b, 2, axis=0)
                b_next2 = pltpu.roll(b, c - 2, axis=0)
                if rev:
                    ref_b = jnp.where(r == 0, b_next2, jnp.where(r == 1, b_next, jnp.where(r == 2, b, b_prev)))
                else:
                    ref_b = jnp.where(r == 0, b_next, jnp.where(r == 1, b, jnp.where(r == 2, b_prev, b_prev2)))
            else:
                blk = 2 * half
                pieces = []
                for j in range(c // blk):
                    brow = b_scr[pl.ds(j * blk + (half if rev else half - 1), 1), :]
                    pieces.append(jnp.broadcast_to(brow, (blk, dk)))
                ref_b = jnp.concatenate(pieces, axis=0) if len(pieces) > 1 else pieces[0]
            q_role = ((row & half) == 0) if rev else ((row & half) != 0)
            x = (jnp.where(q_role, q, k) * jnp.exp(-jnp.abs(b - ref_b))).astype(BF16)
            scores = jnp.where(lev == level, _nt_dot(x, x), scores)
        scores = jnp.where(lev == -1, jnp.sum(q * k, axis=-1, keepdims=True), scores)
        o = _dot(scores.astype(BF16), v)

        st = st_ref[hd]
        b_end = b_scr[pl.ds(0 if rev else c - 1, 1), :]
        o = o + _nt_dot((q * jnp.exp(b)).astype(BF16), st.astype(BF16))
        k_out = (k * jnp.exp(b_end - b)).astype(BF16)
        st_ref[hd] = jnp.exp(b_end) * st + _tn_dot(v, k_out)

        if readout:
            o = o + of_ref[0, :, sl].astype(F32)
            o = o * lax.rsqrt(jnp.mean(o * o, axis=-1, keepdims=True) + RMS_EPS) * ng_ref[...]
            gz = g_ref[0, :, sl].astype(F32)
            o = o * (gz * _sigmoid(gz))
        out_ref[0, :, sl] = o.astype(out_ref.dtype)

    def head_pair(i, carry):
        for slot in range(HEADS_PER_ITER):
            head(i * HEADS_PER_ITER + slot, b_scr_all.at[slot])
        return carry

    lax.fori_loop(0, nheads // HEADS_PER_ITER, head_pair, 0)


def _gla(p_hg, lb_row, norm_g, o_fwd, n_lat, n_ctx, width, dk, rev):
    bsz, total, _ = p_hg.shape
    c = GLA_C
    nl, nc = n_lat // c, n_ctx // c
    nheads = width // dk
    readout = rev

    if rev:
        def chunk(i):
            return jnp.where(i < nc, nl + nc - 1 - i, nl + nc - 1 - i)
    else:
        def chunk(i):
            return jnp.where(i < nc, nl + i, i - nc)

    def col_spec(colblk):
        return pl.BlockSpec((1, c, width), lambda b, i: (b, chunk(i), colblk))

    lev, tri = _gla_tables(c, rev)
    in_specs = [col_spec(0), col_spec(2 if rev else 1), col_spec(3),
                pl.BlockSpec((1, width), lambda b, i: (0, 0)),
                pl.BlockSpec((c, c), lambda b, i: (0, 0)),
                pl.BlockSpec((c, c), lambda b, i: (0, 0))]
    args = [p_hg, p_hg, p_hg, lb_row.reshape(1, width), lev, tri]
    if readout:
        in_specs += [col_spec(4),
                     pl.BlockSpec((1, c, width), lambda b, i: (b, chunk(i), 0)),
                     pl.BlockSpec((1, dk), lambda b, i: (0, 0))]
        args += [p_hg, o_fwd, norm_g.reshape(1, dk)]
    return pl.pallas_call(
        functools.partial(_gla_kernel, rev=rev, readout=readout, c=c, nheads=nheads, dk=dk),
        grid=(bsz, nl + nc),
        in_specs=in_specs,
        out_specs=pl.BlockSpec((1, c, width), lambda b, i: (b, chunk(i), 0)),
        out_shape=jax.ShapeDtypeStruct((bsz, total, width), BF16),
        scratch_shapes=[pltpu.VMEM((nheads, dk, dk), F32), pltpu.VMEM((HEADS_PER_ITER, c, dk), F32)],
        compiler_params=pltpu.CompilerParams(
            dimension_semantics=("parallel", "arbitrary"), vmem_limit_bytes=VMEM_LIMIT),
        name="gla_bwd" if rev else "gla_fwd",
    )(*args)


def _s5_tables(a_re, a_im, log_dt, b_re, b_im, c_re, c_im):
    t = S5_T
    ngrp, nst, cg = b_re.shape
    a_re = jnp.minimum(a_re.astype(F32), S5_MAX_RE)
    a_im = a_im.astype(F32)
    dt = jnp.exp(log_dt.astype(F32))[..., None]
    mag = jnp.exp(dt * a_re)
    abr, abi = mag * jnp.cos(dt * a_im), mag * jnp.sin(dt * a_im)
    den = a_re * a_re + a_im * a_im
    nr, ni = abr - 1.0, abi
    f_re = ((nr * a_re + ni * a_im) / den)[..., None]
    f_im = ((ni * a_re - nr * a_im) / den)[..., None]
    b_re, b_im = b_re.astype(F32)[None], b_im.astype(F32)[None]
    bb_re = f_re * b_re - f_im * b_im
    bb_im = f_re * b_im + f_im * b_re
    pr, pi = [jnp.ones_like(abr)], [jnp.zeros_like(abr)]
    for _ in range(t):
        pr.append(pr[-1] * abr - pi[-1] * abi)
        pi.append(pr[-2] * abi + pi[-1] * abr)
    pr, pi = jnp.stack(pr), jnp.stack(pi)
    c_re, c_im = c_re.astype(F32), c_im.astype(F32)
    l_re = c_re[None, None] * pr[:, :, :, None, :] - c_im[None, None] * pi[:, :, :, None, :]
    l_im = c_re[None, None] * pi[:, :, :, None, :] + c_im[None, None] * pr[:, :, :, None, :]
    kern = (jnp.einsum("drgcn,rgnk->drgck", l_re[:t], bb_re, precision=_HI)
            - jnp.einsum("drgcn,rgnk->drgck", l_im[:t], bb_im, precision=_HI))
    lag = np.arange(t)[:, None] - np.arange(t)[None, :]
    kf = jnp.where((lag >= 0)[:, :, None, None, None], kern[np.clip(lag, 0, t - 1), 0], 0.0)
    kb = jnp.where((lag <= 0)[:, :, None, None, None], kern[np.clip(-lag, 0, t - 1), 1], 0.0)
    m = kf + kb
    mt = m.transpose(2, 1, 4, 0, 3).reshape(ngrp, t * cg, t * cg)
    pf_r, pf_i = pr[t - 1 - np.arange(t), 0], pi[t - 1 - np.arange(t), 0]
    pb_r, pb_i = pr[np.arange(t), 1], pi[np.arange(t), 1]

    def drive(p_r, p_i, r):
        g_r = p_r[..., None] * bb_re[r][None] - p_i[..., None] * bb_im[r][None]
        g_i = p_r[..., None] * bb_im[r][None] + p_i[..., None] * bb_re[r][None]
        to = lambda x: x.transpose(1, 0, 3, 2).reshape(ngrp, t * cg, nst)
        return to(g_r), to(g_i)

    gf_r, gf_i = drive(pf_r, pf_i, 0)
    gb_r, gb_i = drive(pb_r, pb_i, 1)
    def read(idx, r):
        rr = l_re[idx, r].transpose(1, 3, 0, 2).reshape(ngrp, nst, t * cg)
        ri = -l_im[idx, r].transpose(1, 3, 0, 2).reshape(ngrp, nst, t * cg)
        return rr, ri

    of_r, of_i = read(np.arange(t) + 1, 0)
    ob_r, ob_i = read(t - np.arange(t), 1)

    def pair_in(x):
        x = x.reshape(ngrp // 2, 2, t * cg, nst)
        z = jnp.zeros_like(x[:, 0])
        return jnp.concatenate([jnp.concatenate([x[:, 0], z], axis=2),
                                jnp.concatenate([z, x[:, 1]], axis=2)], axis=1)

    def pair_out(x):
        x = x.reshape(ngrp // 2, 2, nst, t * cg)
        z = jnp.zeros_like(x[:, 0])
        return jnp.concatenate([jnp.concatenate([x[:, 0], z], axis=2),
                                jnp.concatenate([z, x[:, 1]], axis=2)], axis=1)

    gin = jnp.concatenate([pair_in(gf_r), pair_in(gf_i), pair_in(gb_r), pair_in(gb_i)], axis=2)
    gout = jnp.concatenate([pair_out(of_r), pair_out(of_i), pair_out(ob_r), pair_out(ob_i)], axis=1)
    a_chunk = jnp.stack([pr[t, 0], pi[t, 0], pr[t, 1], pi[t, 1]]).reshape(4, ngrp * nst)
    return mt.astype(BF16), gin.astype(BF16), gout.astype(BF16), a_chunk


def _s5_kernel(z_ref, mt_ref, gin_ref, gout_ref, ac_ref, y_ref, d_scr, x_scr, *, n_lat, n_ctx, npairs, pw):
    nrows = n_lat + n_ctx
    gw = pw // 2
    tw = pw // 4
    for p in range(npairs):
        d = _dot(z_ref[0, :, p * pw:(p + 1) * pw], gin_ref[p])
        for j in range(4):
            d_scr[j, :, p * tw:(p + 1) * tw] = d[:, j * tw:(j + 1) * tw]

    def scan(plane, forward):
        ar = ac_ref[plane:plane + 1, :]
        ai = ac_ref[plane + 1:plane + 2, :]

        def body(i, carry):
            xr, xi = carry
            if forward:
                n = jnp.where(i < n_ctx, n_lat + i, i - n_ctx)
            else:
                n = nrows - 1 - i
            x_scr[plane, pl.ds(n, 1), :] = xr
            x_scr[plane + 1, pl.ds(n, 1), :] = xi
            dr = d_scr[plane, pl.ds(n, 1), :]
            di = d_scr[plane + 1, pl.ds(n, 1), :]
            return ar * xr - ai * xi + dr, ar * xi + ai * xr + di

        zero = jnp.zeros((1, d_scr.shape[2]), F32)
        lax.fori_loop(0, nrows, body, (zero, zero))

    scan(0, True)
    scan(2, False)

    for p in range(npairs):
        xp = jnp.concatenate([x_scr[j, :, p * tw:(p + 1) * tw] for j in range(4)], axis=1).astype(BF16)
        y = _dot(xp, gout_ref[p])
        for g in range(2):
            lo = p * pw + g * gw
            yg = y[:, g * gw:(g + 1) * gw] + _dot(z_ref[0, :, lo:lo + gw], mt_ref[2 * p + g])
            y_ref[0, :, lo:lo + gw] = yg.astype(y_ref.dtype)


def _s5(zf, mt, gin, gout, a_chunk, n_lat_chunks, n_ctx_chunks):
    bsz, nrows, wide = zf.shape
    npairs, pw, _ = gin.shape
    planes = a_chunk.shape[1]
    return pl.pallas_call(
        functools.partial(_s5_kernel, n_lat=n_lat_chunks, n_ctx=n_ctx_chunks, npairs=npairs, pw=pw),
        grid=(bsz,),
        in_specs=[pl.BlockSpec((1, nrows, wide), lambda b: (b, 0, 0)),
                  _const_spec(mt.shape), _const_spec(gin.shape), _const_spec(gout.shape),
                  _const_spec(a_chunk.shape)],
        out_specs=pl.BlockSpec((1, nrows, wide), lambda b: (b, 0, 0)),
        out_shape=jax.ShapeDtypeStruct((bsz, nrows, wide), BF16),
        scratch_shapes=[pltpu.VMEM((4, nrows, planes), F32), pltpu.VMEM((4, nrows, planes), F32)],
        compiler_params=pltpu.CompilerParams(
            dimension_semantics=("parallel",), vmem_limit_bytes=VMEM_LIMIT),
        name="s5",
    )(zf, mt, gin, gout, a_chunk)


def _s5_flatten(u, ngrp, cg):
    bsz, total, _ = u.shape
    n = total // S5_T
    return u.reshape(bsz, n, S5_T, ngrp, cg).transpose(0, 1, 3, 2, 4).reshape(bsz, n, ngrp * S5_T * cg)


def _s5_unflatten(yz, ngrp, cg):
    bsz, n, _ = yz.shape
    return yz.reshape(bsz, n, ngrp, S5_T, cg).transpose(0, 1, 3, 2, 4).reshape(bsz, n * S5_T, ngrp * cg)


def _gelu_tanh(x):
    return 0.5 * x * (1.0 + jnp.tanh(math.sqrt(2.0 / math.pi) * (x + 0.044715 * (x * x * x))))


def _merge_kernel(h_ref, yh_ref, ypre_ref, u_ref, gate_ref, mod_ref, d_ref,
                  whg_ref, wglu_ref, wbr_ref, wout_ref, o_ref):
    d = h_ref.shape[2]
    y_hg = _dot(yh_ref[0], whg_ref[...])
    ys = _gelu_tanh(ypre_ref[0].astype(F32) + d_ref[...] * u_ref[0].astype(F32))
    glu = _sigmoid(_dot(ys.astype(BF16), wglu_ref[...]))
    y_s5 = _dot((ys * glu).astype(BF16), wbr_ref[...])
    gate = gate_ref[0].astype(F32)
    merged = _sigmoid(gate[:, :d]) * y_hg + _sigmoid(gate[:, d:]) * y_s5
    o_ref[0] = h_ref[0] + mod_ref[0, 2:3, :] * _dot(merged.astype(BF16), wout_ref[...])


def _merge(h, yh, ypre, p_u, p_gate, mods, mod_row_fn, s5_d, whg, wglu, wbr, wout, layer, tok_off_blocks):
    bsz, length, d = h.shape
    sw = p_u.shape[2]
    nt = length // TOK_TILE
    seq = lambda w: pl.BlockSpec((1, TOK_TILE, w), lambda b, t: (b, t + tok_off_blocks, 0))
    return pl.pallas_call(
        _merge_kernel,
        grid=(bsz, nt),
        in_specs=[pl.BlockSpec((1, TOK_TILE, d), lambda b, t: (b, t, 0)),
                  seq(yh.shape[2]), seq(sw), seq(sw), seq(p_gate.shape[2]),
                  pl.BlockSpec((1, 6, d), lambda b, t: (mod_row_fn(b), 0, 0)),
                  pl.BlockSpec((1, sw), lambda b, t: (0, 0)),
                  _layer_spec(whg, layer), _layer_spec(wglu, layer), _layer_spec(wbr, layer),
                  _layer_spec(wout, layer)],
        out_specs=pl.BlockSpec((1, TOK_TILE, d), lambda b, t: (b, t, 0)),
        out_shape=jax.ShapeDtypeStruct(h.shape, F32),
        input_output_aliases={0: 0},
        compiler_params=pltpu.CompilerParams(
            dimension_semantics=("parallel", "parallel"), vmem_limit_bytes=VMEM_LIMIT),
        name="merge",
    )(h, yh, ypre, p_u, p_gate, mods, s5_d.reshape(1, sw), whg, wglu, wbr, wout)


def _ffn_kernel(h_ref, mod_ref, g_ref, wab_ref, cw_ref, cb_ref, wd_ref, fg_ref,
                o_ref, xn_scr, up_scr, act_scr, *, n_tok, width, cw, mrows, final):
    j = pl.program_id(1)
    nj = pl.num_programs(1)
    vertical = n_tok > width
    nblk = n_tok // mrows
    rpb = mrows // width
    pad = width

    @pl.when(j == 0)
    def _():
        def norm_body(i, carry):
            r0 = pl.multiple_of(i * mrows, mrows)
            xn_scr[pl.ds(r0, mrows), :] = _modulated_norm(
                h_ref[0, pl.ds(r0, mrows), :], g_ref[...], mod_ref[0, 3:4, :], mod_ref[0, 4:5, :]).astype(BF16)
            o_ref[0, pl.ds(r0, mrows), :] = jnp.zeros((mrows, o_ref.shape[2]), F32)
            act_scr[pl.ds(r0, mrows), :] = jnp.zeros((mrows, cw), BF16)
            return carry

        lax.fori_loop(0, nblk, norm_body, 0)
        xn_scr[n_tok:n_tok + width, :] = jnp.zeros((width, xn_scr.shape[1]), BF16)

        def zero_body(i, carry):
            for copy in range(3):
                up_scr[copy, pl.ds(pl.multiple_of(i * width, 8), width), :] = jnp.zeros((width, 2 * cw), F32)
            return carry

        lax.fori_loop(0, up_scr.shape[1] // width, zero_body, 0)

    def up_rows(row0, m):
        up = _dot(xn_scr[pl.ds(row0, m), :], wab_ref[...])
        col = lax.broadcasted_iota(jnp.int32, (m, 1), 0) % width
        up_scr[0, pl.ds(pad + row0, m), :] = jnp.where(col == 0, 0.0, pltpu.roll(up, 1, axis=0))
        up_scr[1, pl.ds(pad + row0, m), :] = up
        up_scr[2, pl.ds(pad + row0, m), :] = jnp.where(col == width - 1, 0.0, pltpu.roll(up, m - 1, axis=0))

    def conv_row(row0, lane0):
        acc = cb_ref[:, lane0:lane0 + 128]
        for dr in ((-1, 0, 1) if vertical else (0,)):
            for dc in (-1, 0, 1):
                tap = (dr + 1) * 3 + (dc + 1)
                src = up_scr[dc + 1, pl.ds(pl.multiple_of(pad + row0 + dr * width, 8), width), lane0:lane0 + 128]
                acc = acc + cw_ref[tap:tap + 1, lane0:lane0 + 128] * src
        return acc

    def conv_block(blk):
        r0 = blk * mrows
        for lt in range(cw // 128):
            for r in range(rpb):
                row0 = r0 + r * width
                ca = conv_row(row0, lt * 128)
                cb = conv_row(row0, cw + lt * 128)
                act_scr[pl.ds(pl.multiple_of(row0, 16), width), lt * 128:(lt + 1) * 128] = (
                    ca * _sigmoid(ca) * cb).astype(BF16)

    up_rows(0, width)

    def body(k, carry):
        rd = pl.multiple_of(jnp.maximum(k - 2, 0) * mrows, mrows)
        a_blk = act_scr[pl.ds(rd, mrows), :]
        a_blk = jnp.where(k >= 2, a_blk, jnp.zeros_like(a_blk))
        o_ref[0, pl.ds(rd, mrows), :] += _dot(a_blk, wd_ref[...])
        conv_block(jnp.clip(k - 1, 0, nblk - 1))
        up_rows(pl.multiple_of(jnp.minimum(k, nblk - 1) * mrows + width, width), mrows)
        return carry

    lax.fori_loop(0, nblk + 2, body, 0)

    @pl.when(j == nj - 1)
    def _():
        def fin_body(i, carry):
            r0 = pl.multiple_of(i * mrows, mrows)
            out = h_ref[0, pl.ds(r0, mrows), :] + mod_ref[0, 5:6, :] * o_ref[0, pl.ds(r0, mrows), :]
            if final:
                out = out * lax.rsqrt(jnp.mean(out * out, axis=-1, keepdims=True) + RMS_EPS) * fg_ref[...]
            o_ref[0, pl.ds(r0, mrows), :] = out
            return carry

        lax.fori_loop(0, nblk, fin_body, 0)


def _ffn(h, mods, mod_row_fn, norm_g, ffn_w, layer, final_g, width, final):
    bsz, n_tok, d = h.shape
    wab, cwf, cbf, wd = ffn_w
    cw = FFN_TILE
    nj = wd.shape[1] // cw
    return pl.pallas_call(
        functools.partial(_ffn_kernel, n_tok=n_tok, width=width, cw=cw, mrows=TOK_TILE, final=final),
        grid=(bsz, nj),
        in_specs=[pl.BlockSpec((1, n_tok, d), lambda b, j: (b, 0, 0), pipeline_mode=pl.Buffered(1)),
                  pl.BlockSpec((1, 6, d), lambda b, j: (mod_row_fn(b), 0, 0)),
                  pl.BlockSpec((1, d), lambda b, j: (0, 0)),
                  pl.BlockSpec((None, d, 2 * cw), lambda b, j: (layer, 0, j)),
                  pl.BlockSpec((None, 9, 2 * cw), lambda b, j: (layer, 0, j)),
                  pl.BlockSpec((None, 1, 2 * cw), lambda b, j: (layer, 0, j)),
                  pl.BlockSpec((None, cw, d), lambda b, j: (layer, j, 0)),
                  pl.BlockSpec((1, d), lambda b, j: (0, 0))],
        out_specs=pl.BlockSpec((1, n_tok, d), lambda b, j: (b, 0, 0)),
        out_shape=jax.ShapeDtypeStruct(h.shape, F32),
        scratch_shapes=[pltpu.VMEM((n_tok + width, d), BF16),
                        pltpu.VMEM((3, n_tok + 2 * width, 2 * cw), F32),
                        pltpu.VMEM((n_tok, cw), BF16)],
        input_output_aliases={0: 0},
        compiler_params=pltpu.CompilerParams(
            dimension_semantics=("parallel", "arbitrary"), vmem_limit_bytes=VMEM_LIMIT),
        name="ffn",
    )(h, mods, norm_g.reshape(1, d), wab, cwf, cbf, wd, final_g.reshape(1, d))


def _ffn_weights(w_up, conv_w, conv_b, w_down):
    depth, ffn, _ = w_down.shape
    cw = FFN_TILE
    fp = ((ffn + cw - 1) // cw) * cw
    nj = fp // cw

    def tiles(x):
        lead = x.shape[:-1]
        halves = []
        for part in (x[..., :ffn], x[..., ffn:]):
            part = jnp.pad(part, [(0, 0)] * len(lead) + [(0, fp - ffn)])
            halves.append(part.reshape(lead + (nj, cw)))
        return jnp.concatenate(halves, axis=-1).reshape(lead + (nj * 2 * cw,))

    wd = jnp.pad(w_down.astype(BF16), [(0, 0), (0, fp - ffn), (0, 0)])
    return (tiles(w_up.astype(BF16)), tiles(conv_w.reshape(depth, 9, 2 * ffn)),
            tiles(conv_b.reshape(depth, 1, 2 * ffn)), wd)


def _lower_bounds(lb_raw):
    p = jax.nn.softmax(lb_raw.astype(F32), axis=0)
    cs = jnp.cumsum(p, axis=0)
    return cs - cs[0:1]


def kernel(x, c, ctx, c_ctx, ada_w, ada_b, norm1_g, w_in, hg_lb_raw, hg_norm_g, w_hg_br, s5_a_re, s5_a_im, s5_log_dt, s5_b_re, s5_b_im, s5_c_re, s5_c_im, s5_d, w_s5_glu, w_s5_br, w_out, norm2_g, w_up, ffn_conv_w, ffn_conv_b, w_down, final_g):
    bsz, n_lat, d = x.shape
    n_ctx = ctx.shape[1]
    depth = ada_w.shape[0]
    dk = hg_norm_g.shape[1]
    hg_w = hg_lb_raw.shape[2]
    ngrp, nst, cg = s5_b_re.shape[1:]
    s5_w = ngrp * cg
    total = n_lat + n_ctx
    widths = (5 * hg_w, s5_w, 2 * d)
    assert n_lat % TOK_TILE == 0 and n_ctx % TOK_TILE == 0 and n_lat % GLA_C == 0 and n_ctx % GLA_C == 0
    assert sum(widths) == w_in.shape[2] and S5_T * cg == 256 and ngrp % 2 == 0

    rows = ((bsz + 1 + 7) // 8) * 8
    cond = jnp.zeros((rows, d), F32).at[:bsz].set(c).at[bsz].set(c_ctx)
    mods = _ada_mod(cond, ada_w, ada_b).reshape(depth, rows, 6, d)
    lat_row = lambda b: b
    ctx_row = lambda b: bsz
    lbs = _lower_bounds(hg_lb_raw)
    lat_blk = n_lat // TOK_TILE

    w_in_b, whg_b, wglu_b, wbr_b, wout_b = (w.astype(BF16) for w in (w_in, w_hg_br, w_s5_glu, w_s5_br, w_out))
    ffn_w = _ffn_weights(w_up, ffn_conv_w, ffn_conv_b, w_down)

    h_lat, h_ctx = x, ctx
    for l in range(depth):
        last = l == depth - 1
        p_hg, p_u, p_gate = _proj_in(h_lat, h_ctx, mods[l], norm1_g[l], w_in_b, l, widths)

        o_fwd = _gla(p_hg, lbs[l, 0], hg_norm_g[l], None, n_lat, n_ctx, hg_w, dk, rev=False)
        yh = _gla(p_hg, lbs[l, 1], hg_norm_g[l], o_fwd, n_lat, n_ctx, hg_w, dk, rev=True)

        mt, gin, gout, a_chunk = _s5_tables(s5_a_re[l], s5_a_im[l], s5_log_dt[l], s5_b_re[l], s5_b_im[l],
                                            s5_c_re[l], s5_c_im[l])
        yz = _s5(_s5_flatten(p_u, ngrp, cg), mt, gin, gout, a_chunk, n_lat // S5_T, n_ctx // S5_T)
        ypre = _s5_unflatten(yz, ngrp, cg)

        wts = (s5_d[l], whg_b, wglu_b, wbr_b, wout_b, l)
        h_lat = _merge(h_lat, yh, ypre, p_u, p_gate, mods[l], lat_row, *wts, 0)
        h_lat = _ffn(h_lat, mods[l], lat_row, norm2_g[l], ffn_w, l, final_g, GRID_W, last)
        if not last:
            h_ctx = _merge(h_ctx, yh, ypre, p_u, p_gate, mods[l], ctx_row, *wts, lat_blk)
            h_ctx = _ffn(h_ctx, mods[l], ctx_row, norm2_g[l], ffn_w, l, final_g, n_ctx, False)
    return h_lat
```

```python
import functools
import math

import numpy as np
import jax
import jax.numpy as jnp
from jax import lax
from jax.experimental import pallas as pl
from jax.experimental.pallas import tpu as pltpu

GRID_W = 64
RMS_EPS = 1e-6
S5_MAX_RE = -1e-4
S5_T = 16
GLA_C = 128
HEADS_PER_ITER = 2
TOK_TILE = 256
FFN_TILE = 256
VMEM_LIMIT = 56 * 1024 * 1024

F32 = jnp.float32
BF16 = jnp.bfloat16
_HI = lax.Precision.HIGHEST


def _nt_dot(a, b):
    return lax.dot_general(a, b, (((1,), (1,)), ((), ())), preferred_element_type=F32)


def _tn_dot(a, b):
    return lax.dot_general(a, b, (((0,), (0,)), ((), ())), preferred_element_type=F32)


def _dot(a, b):
    return jnp.dot(a, b, preferred_element_type=F32)


def _sigmoid(x):
    return 1.0 / (1.0 + jnp.exp(-x))


def _const_spec(shape):
    nd = len(shape)
    return pl.BlockSpec(shape, lambda *_: (0,) * nd, pipeline_mode=pl.Buffered(1))


def _layer_spec(stacked, layer):
    rest = stacked.shape[1:]
    return pl.BlockSpec((None,) + rest, lambda *_: (layer,) + (0,) * len(rest), pipeline_mode=pl.Buffered(1))


def _ada_kernel(cond_ref, w_ref, b_ref, o_ref):
    cnd = cond_ref[...]
    s = cnd * _sigmoid(cnd)
    o_ref[0] = jnp.dot(s, w_ref[0], preferred_element_type=F32, precision=_HI) + b_ref[0]


def _ada_mod(cond, ada_w, ada_b):
    depth, d, n6 = ada_w.shape
    rows = cond.shape[0]
    tn = 512
    return pl.pallas_call(
        _ada_kernel,
        grid=(depth, n6 // tn),
        in_specs=[
            pl.BlockSpec((rows, d), lambda l, j: (0, 0)),
            pl.BlockSpec((1, d, tn), lambda l, j: (l, 0, j)),
            pl.BlockSpec((1, 1, tn), lambda l, j: (l, 0, j)),
        ],
        out_specs=pl.BlockSpec((1, rows, tn), lambda l, j: (l, 0, j)),
        out_shape=jax.ShapeDtypeStruct((depth, rows, n6), F32),
        name="ada_mod",
    )(cond, ada_w, ada_b.reshape(depth, 1, n6))


def _modulated_norm(x, g, shift, scale):
    ms = jnp.mean(x * x, axis=-1, keepdims=True)
    y = x * lax.rsqrt(ms + RMS_EPS) * g
    return y * (1.0 + scale) + shift


def _strip_masks(n_rows, cg):
    lane = lax.broadcasted_iota(jnp.int32, (n_rows, 128), 1)
    return [(lane >= j * cg) & (lane < (j + 1) * cg) for j in range(128 // cg)]


def _chunk_flatten(u_scr, z_ref, n_chunks, ngrp, cg):
    gpt = 128 // cg
    masks = _strip_masks(n_chunks, cg)
    for g in range(ngrp):
        k, i = divmod(g, gpt)
        for hf in range(S5_T // gpt):
            acc = None
            for j in range(gpt):
                src = u_scr[k, pl.ds(hf * gpt + j, n_chunks, stride=S5_T), :]
                shift = ((j - i) * cg) % 128
                if shift:
                    src = pltpu.roll(src, shift, axis=1)
                acc = src if acc is None else jnp.where(masks[j], src, acc)
            tile = g * (S5_T // gpt) + hf
            z_ref[0, :, tile * 128:(tile + 1) * 128] = acc.astype(z_ref.dtype)


def _chunk_unflatten(yz_ref, scr, n_chunks, ngrp, cg):
    gpt = 128 // cg
    masks = _strip_masks(n_chunks, cg)
    for k in range(ngrp // gpt):
        for hf in range(S5_T // gpt):
            srcs = []
            for i in range(gpt):
                tile = (k * gpt + i) * (S5_T // gpt) + hf
                srcs.append(yz_ref[0, :, tile * 128:(tile + 1) * 128].astype(F32))
            for j in range(gpt):
                acc = None
                for i in range(gpt):
                    shift = ((i - j) * cg) % 128
                    src = pltpu.roll(srcs[i], shift, axis=1) if shift else srcs[i]
                    acc = src if acc is None else jnp.where(masks[i], src, acc)
                t = hf * gpt + j
                scr[k, t * n_chunks:(t + 1) * n_chunks, :] = acc
    return jnp.concatenate(
        [jnp.concatenate([scr[k, pl.ds(ch, S5_T, stride=n_chunks), :] for k in range(scr.shape[0])], axis=1)
         for ch in range(n_chunks)], axis=0)


def _proj_in_kernel(hl_ref, hc_ref, mod_ref, g_ref, w_ref, hg_ref, z_ref, gate_ref, xn_scr, u_scr,
                    *, nt_lat, ngrp, cg):
    def norm(h_ref):
        xn_scr[...] = _modulated_norm(h_ref[0], g_ref[...], mod_ref[0, 0:1, :], mod_ref[0, 1:2, :]).astype(BF16)

    pl.when(pl.program_id(1) < nt_lat)(lambda: norm(hl_ref))
    pl.when(pl.program_id(1) >= nt_lat)(lambda: norm(hc_ref))
    xn = xn_scr[...]
    hg_w, s5_w = hg_ref.shape[2], ngrp * cg

    def project(o_ref, col0):
        width = o_ref.shape[2]
        step = math.gcd(512, width)
        for c in range(0, width, step):
            o_ref[0, :, c:c + step] = _dot(xn, w_ref[:, col0 + c:col0 + c + step]).astype(o_ref.dtype)

    project(hg_ref, 0)
    project(gate_ref, hg_w + s5_w)
    u = _dot(xn, w_ref[:, hg_w:hg_w + s5_w])
    for k in range(s5_w // 128):
        u_scr[k] = u[:, k * 128:(k + 1) * 128]
    _chunk_flatten(u_scr, z_ref, xn_scr.shape[0] // S5_T, ngrp, cg)


def _proj_in(h_lat, h_ctx, mods, norm_g, w_stack, layer, hg_cols, ngrp, cg):
    bsz, n_lat, d = h_lat.shape
    n_ctx = h_ctx.shape[1]
    total = n_lat + n_ctx
    nt_lat, nt_ctx = n_lat // TOK_TILE, n_ctx // TOK_TILE
    s5_w = ngrp * cg
    gate_cols = w_stack.shape[2] - hg_cols - s5_w
    ct = TOK_TILE // S5_T
    return pl.pallas_call(
        functools.partial(_proj_in_kernel, nt_lat=nt_lat, ngrp=ngrp, cg=cg),
        grid=(bsz, nt_lat + nt_ctx),
        in_specs=[
            pl.BlockSpec((1, TOK_TILE, d), lambda b, t: (b, jnp.minimum(t, nt_lat - 1), 0)),
            pl.BlockSpec((1, TOK_TILE, d), lambda b, t: (b, jnp.maximum(t - nt_lat, 0), 0)),
            pl.BlockSpec((1, 6, d), lambda b, t: (jnp.where(t < nt_lat, b, bsz), 0, 0)),
            pl.BlockSpec((1, d), lambda b, t: (0, 0)),
            _layer_spec(w_stack, layer),
        ],
        out_specs=[pl.BlockSpec((1, TOK_TILE, hg_cols), lambda b, t: (b, t, 0)),
                   pl.BlockSpec((1, ct, s5_w * S5_T), lambda b, t: (b, t, 0)),
                   pl.BlockSpec((1, TOK_TILE, gate_cols), lambda b, t: (b, t, 0))],
        out_shape=[jax.ShapeDtypeStruct((bsz, total, hg_cols), BF16),
                   jax.ShapeDtypeStruct((bsz, total // S5_T, s5_w * S5_T), BF16),
                   jax.ShapeDtypeStruct((bsz, total, gate_cols), BF16)],
        scratch_shapes=[pltpu.VMEM((TOK_TILE, d), BF16), pltpu.VMEM((s5_w // 128, TOK_TILE, 128), F32)],
        compiler_params=pltpu.CompilerParams(
            dimension_semantics=("parallel", "arbitrary"), vmem_limit_bytes=VMEM_LIMIT),
        name="proj_in",
    )(h_lat, h_ctx, mods, norm_g.reshape(1, d), w_stack)


def _gla_tables(c, rev):
    t = np.arange(c)[:, None]
    s = np.arange(c)[None, :]
    x = t ^ s
    lev = np.full((c, c), -2, np.int32)
    causal = (t < s) if rev else (t > s)
    with np.errstate(divide="ignore"):
        hb = np.floor(np.log2(np.maximum(x, 1))).astype(np.int32)
    lev = np.where(causal, hb, lev)
    lev = np.where(t == s, -1, lev).astype(np.int32)
    tri = ((t <= s) if rev else (t >= s)).astype(np.float32)
    return jnp.asarray(lev), jnp.asarray(tri, dtype=BF16)


def _gla_kernel(*refs, rev, readout, c, nheads, dk):
    if readout:
        (q_ref, fz_ref, i_ref, lb_ref, lev_ref, tri_ref, g_ref, of_ref, ng_ref,
         out_ref, st_ref, b_scr_all) = refs
    else:
        q_ref, fz_ref, i_ref, lb_ref, lev_ref, tri_ref, out_ref, st_ref, b_scr_all = refs
    nlev = int(math.log2(c))

    @pl.when(pl.program_id(1) == 0)
    def _():
        st_ref[...] = jnp.zeros_like(st_ref)

    lev = lev_ref[...]
    tri = tri_ref[...]
    row = lax.broadcasted_iota(jnp.int32, (c, 1), 0)

    def head(hd, b_scr):
        sl = pl.ds(pl.multiple_of(hd * dk, dk), dk)
        z = fz_ref[0, :, sl].astype(F32)
        qz = q_ref[0, :, sl].astype(F32)
        v = i_ref[0, :, sl]
        lb = lb_ref[0:1, sl]

        e = jnp.exp(-jnp.abs(z))
        inv = 1.0 / (1.0 + e)
        logsig = jnp.minimum(z, 0.0) - jnp.log(1.0 + e)
        pos = z >= 0
        sig_pos = jnp.where(pos, 1.0, e) * inv
        sig_neg = jnp.where(pos, e, 1.0) * inv
        a = jnp.where(lb > 0, jnp.log(lb + (1.0 - lb) * sig_pos), logsig)
        k = (1.0 - lb) * sig_neg
        q = qz * _sigmoid(qz)

        a_hi = a.astype(BF16)
        a_lo = (a - a_hi.astype(F32)).astype(BF16)
        bb = _dot(tri, jnp.concatenate([a_hi, a_lo], axis=1))
        b = bb[:, :dk] + bb[:, dk:]
        b_scr[...] = b

        b_prev = pltpu.roll(b, 1, axis=0)
        b_next = pltpu.roll(b, c - 1, axis=0)
        scores = jnp.zeros((c, c), F32)
        for level in range(nlev):
            half = 1 << level
            if level == 0:
                r = row & 1
                ref_b = jnp.where(r == (0 if rev else 1), b_next if rev else b_prev, b)
            elif level == 1:
                r = row & 3
                b_prev2 = pltpu.roll(b, 2, axis=0)
                b_next2 = pltpu.roll(b, c - 2, axis=0)
                if rev:
                    ref_b = jnp.where(r == 0, b_next2, jnp.where(r == 1, b_next, jnp.where(r == 2, b, b_prev)))
                else:
                    ref_b = jnp.where(r == 0, b_next, jnp.where(r == 1, b, jnp.where(r == 2, b_prev, b_prev2)))
            else:
                blk = 2 * half
                pieces = []
                for j in range(c // blk):
                    brow = b_scr[pl.ds(j * blk + (half if rev else half - 1), 1), :]
                    pieces.append(jnp.broadcast_to(brow, (blk, dk)))
                ref_b = jnp.concatenate(pieces, axis=0) if len(pieces) > 1 else pieces[0]
            q_role = ((row & half) == 0) if rev else ((row & half) != 0)
            x = (jnp.where(q_role, q, k) * jnp.exp(-jnp.abs(b - ref_b))).astype(BF16)
            scores = jnp.where(lev == level, _nt_dot(x, x), scores)
        scores = jnp.where(lev == -1, jnp.sum(q * k, axis=-1, keepdims=True), scores)
        o = _dot(scores.astype(BF16), v)

        st = st_ref[hd]
        b_end = b_scr[pl.ds(0 if rev else c - 1, 1), :]
        o = o + _nt_dot((q * jnp.exp(b)).astype(BF16), st.astype(BF16))
        k_out = (k * jnp.exp(b_end - b)).astype(BF16)
        st_ref[hd] = jnp.exp(b_end) * st + _tn_dot(v, k_out)

        if readout:
            o = o + of_ref[0, :, sl].astype(F32)
            o = o * lax.rsqrt(jnp.mean(o * o, axis=-1, keepdims=True) + RMS_EPS) * ng_ref[...]
            gz = g_ref[0, :, sl].astype(F32)
            o = o * (gz * _sigmoid(gz))
        out_ref[0, :, sl] = o.astype(out_ref.dtype)

    def head_pair(i, carry):
        for slot in range(HEADS_PER_ITER):
            head(i * HEADS_PER_ITER + slot, b_scr_all.at[slot])
        return carry

    lax.fori_loop(0, nheads // HEADS_PER_ITER, head_pair, 0)


def _gla(p_hg, lb_row, norm_g, o_fwd, n_lat, n_ctx, width, dk, rev):
    bsz, total, _ = p_hg.shape
    c = GLA_C
    nl, nc = n_lat // c, n_ctx // c
    nheads = width // dk
    readout = rev

    if rev:
        def chunk(i):
            return jnp.where(i < nc, nl + nc - 1 - i, nl + nc - 1 - i)
    else:
        def chunk(i):
            return jnp.where(i < nc, nl + i, i - nc)

    def col_spec(colblk):
        return pl.BlockSpec((1, c, width), lambda b, i: (b, chunk(i), colblk))

    lev, tri = _gla_tables(c, rev)
    in_specs = [col_spec(0), col_spec(2 if rev else 1), col_spec(3),
                pl.BlockSpec((1, width), lambda b, i: (0, 0)),
                pl.BlockSpec((c, c), lambda b, i: (0, 0)),
                pl.BlockSpec((c, c), lambda b, i: (0, 0))]
    args = [p_hg, p_hg, p_hg, lb_row.reshape(1, width), lev, tri]
    if readout:
        in_specs += [col_spec(4),
                     pl.BlockSpec((1, c, width), lambda b, i: (b, chunk(i), 0)),
                     pl.BlockSpec((1, dk), lambda b, i: (0, 0))]
        args += [p_hg, o_fwd, norm_g.reshape(1, dk)]
    return pl.pallas_call(
        functools.partial(_gla_kernel, rev=rev, readout=readout, c=c, nheads=nheads, dk=dk),
        grid=(bsz, nl + nc),
        in_specs=in_specs,
        out_specs=pl.BlockSpec((1, c, width), lambda b, i: (b, chunk(i), 0)),
        out_shape=jax.ShapeDtypeStruct((bsz, total, width), BF16),
        scratch_shapes=[pltpu.VMEM((nheads, dk, dk), F32), pltpu.VMEM((HEADS_PER_ITER, c, dk), F32)],
        compiler_params=pltpu.CompilerParams(
            dimension_semantics=("parallel", "arbitrary"), vmem_limit_bytes=VMEM_LIMIT),
        name="gla_bwd" if rev else "gla_fwd",
    )(*args)


def _s5_tables(a_re, a_im, log_dt, b_re, b_im, c_re, c_im):
    t = S5_T
    ngrp, nst, cg = b_re.shape
    a_re = jnp.minimum(a_re.astype(F32), S5_MAX_RE)
    a_im = a_im.astype(F32)
    dt = jnp.exp(log_dt.astype(F32))[..., None]
    mag = jnp.exp(dt * a_re)
    abr, abi = mag * jnp.cos(dt * a_im), mag * jnp.sin(dt * a_im)
    den = a_re * a_re + a_im * a_im
    nr, ni = abr - 1.0, abi
    f_re = ((nr * a_re + ni * a_im) / den)[..., None]
    f_im = ((ni * a_re - nr * a_im) / den)[..., None]
    b_re, b_im = b_re.astype(F32)[None], b_im.astype(F32)[None]
    bb_re = f_re * b_re - f_im * b_im
    bb_im = f_re * b_im + f_im * b_re
    pr, pi = [jnp.ones_like(abr)], [jnp.zeros_like(abr)]
    for _ in range(t):
        pr.append(pr[-1] * abr - pi[-1] * abi)
        pi.append(pr[-2] * abi + pi[-1] * abr)
    pr, pi = jnp.stack(pr), jnp.stack(pi)
    c_re, c_im = c_re.astype(F32), c_im.astype(F32)
    l_re = c_re[None, None] * pr[:, :, :, None, :] - c_im[None, None] * pi[:, :, :, None, :]
    l_im = c_re[None, None] * pi[:, :, :, None, :] + c_im[None, None] * pr[:, :, :, None, :]
    kern = (jnp.einsum("drgcn,rgnk->drgck", l_re[:t], bb_re, precision=_HI)
            - jnp.einsum("drgcn,rgnk->drgck", l_im[:t], bb_im, precision=_HI))
    lag = np.arange(t)[:, None] - np.arange(t)[None, :]
    sel = np.stack([lag[:, :, None] == np.arange(t), -lag[:, :, None] == np.arange(t)])
    mt = jnp.einsum("rtsd,drgck->gsktc", jnp.asarray(sel, BF16), kern.astype(BF16),
                    preferred_element_type=F32).reshape(ngrp, t * cg, t * cg)
    pf_r, pf_i = pr[t - 1 - np.arange(t), 0], pi[t - 1 - np.arange(t), 0]
    pb_r, pb_i = pr[np.arange(t), 1], pi[np.arange(t), 1]

    def drive(p_r, p_i, r):
        g_r = p_r[..., None] * bb_re[r][None] - p_i[..., None] * bb_im[r][None]
        g_i = p_r[..., None] * bb_im[r][None] + p_i[..., None] * bb_re[r][None]
        to = lambda x: x.transpose(1, 0, 3, 2).reshape(ngrp, t * cg, nst)
        return to(g_r), to(g_i)

    gf_r, gf_i = drive(pf_r, pf_i, 0)
    gb_r, gb_i = drive(pb_r, pb_i, 1)
    def read(idx, r):
        rr = l_re[idx, r].transpose(1, 3, 0, 2).reshape(ngrp, nst, t * cg)
        ri = -l_im[idx, r].transpose(1, 3, 0, 2).reshape(ngrp, nst, t * cg)
        return rr, ri

    of_r, of_i = read(np.arange(t) + 1, 0)
    ob_r, ob_i = read(t - np.arange(t), 1)

    eye = jnp.eye(2, dtype=F32)
    g4 = jnp.stack([gf_r, gf_i, gb_r, gb_i], axis=2).reshape(ngrp // 2, 2, t * cg, 4, 1, nst)
    gin = (g4 * eye[None, :, None, None, :, None]).reshape(ngrp // 2, 2 * t * cg, 8 * nst)
    o4 = jnp.stack([of_r, of_i, ob_r, ob_i], axis=1).reshape(ngrp // 2, 2, 4, nst, 1, t * cg)
    o4 = o4.transpose(0, 2, 1, 3, 4, 5)
    gout = (o4 * eye[None, None, :, None, :, None]).reshape(ngrp // 2, 8 * nst, 2 * t * cg)
    a_chunk = jnp.stack([pr[t, 0], pi[t, 0], pr[t, 1], pi[t, 1]]).reshape(4, ngrp * nst)
    return mt.astype(BF16), gin.astype(BF16), gout.astype(BF16), a_chunk


def _s5_kernel(z_ref, mt_ref, gin_ref, gout_ref, ac_ref, skip_ref, y_ref, d_scr, x_scr,
               *, n_lat, n_ctx, npairs, pw):
    nrows = n_lat + n_ctx
    gw = pw // 2
    tw = pw // 4
    for p in range(npairs):
        d = _dot(z_ref[0, :, p * pw:(p + 1) * pw], gin_ref[p])
        for j in range(4):
            d_scr[j, :, p * tw:(p + 1) * tw] = d[:, j * tw:(j + 1) * tw]

    def scan(plane, forward):
        ar = ac_ref[plane:plane + 1, :]
        ai = ac_ref[plane + 1:plane + 2, :]

        def body(i, carry):
            xr, xi = carry
            if forward:
                n = jnp.where(i < n_ctx, n_lat + i, i - n_ctx)
            else:
                n = nrows - 1 - i
            x_scr[plane, pl.ds(n, 1), :] = xr
            x_scr[plane + 1, pl.ds(n, 1), :] = xi
            dr = d_scr[plane, pl.ds(n, 1), :]
            di = d_scr[plane + 1, pl.ds(n, 1), :]
            return ar * xr - ai * xi + dr, ar * xi + ai * xr + di

        zero = jnp.zeros((1, d_scr.shape[2]), F32)
        lax.fori_loop(0, nrows, body, (zero, zero))

    scan(0, True)
    scan(2, False)

    for p in range(npairs):
        xp = jnp.concatenate([x_scr[j, :, p * tw:(p + 1) * tw] for j in range(4)], axis=1).astype(BF16)
        y = _dot(xp, gout_ref[p])
        for g in range(2):
            lo = p * pw + g * gw
            zg = z_ref[0, :, lo:lo + gw]
            yg = y[:, g * gw:(g + 1) * gw] + _dot(zg, mt_ref[2 * p + g]) + skip_ref[:, lo:lo + gw] * zg.astype(F32)
            y_ref[0, :, lo:lo + gw] = yg.astype(y_ref.dtype)


def _s5(zf, tables, layer, skip_flat, n_lat_chunks, n_ctx_chunks):
    bsz, nrows, wide = zf.shape
    mt, gin, gout, a_chunk = tables
    npairs, pw = gin.shape[1:3]
    planes = a_chunk.shape[2]
    return pl.pallas_call(
        functools.partial(_s5_kernel, n_lat=n_lat_chunks, n_ctx=n_ctx_chunks, npairs=npairs, pw=pw),
        grid=(bsz,),
        in_specs=[pl.BlockSpec((1, nrows, wide), lambda b: (b, 0, 0)),
                  _layer_spec(mt, layer), _layer_spec(gin, layer), _layer_spec(gout, layer),
                  _layer_spec(a_chunk, layer), _const_spec(skip_flat.shape)],
        out_specs=pl.BlockSpec((1, nrows, wide), lambda b: (b, 0, 0)),
        out_shape=jax.ShapeDtypeStruct((bsz, nrows, wide), BF16),
        scratch_shapes=[pltpu.VMEM((4, nrows, planes), F32), pltpu.VMEM((4, nrows, planes), F32)],
        compiler_params=pltpu.CompilerParams(
            dimension_semantics=("parallel",), vmem_limit_bytes=VMEM_LIMIT),
        name="s5",
    )(zf, mt, gin, gout, a_chunk, skip_flat)


def _gelu_tanh(x):
    return 0.5 * x * (1.0 + jnp.tanh(math.sqrt(2.0 / math.pi) * (x + 0.044715 * (x * x * x))))


def _merge_kernel(h_ref, yh_ref, yz_ref, gate_ref, mod_ref, whg_ref, wglu_ref, wbr_ref, wout_ref, o_ref, y_scr,
                  *, ngrp, cg):
    d = h_ref.shape[2]
    y_hg = _dot(yh_ref[0], whg_ref[...])
    ys = _gelu_tanh(_chunk_unflatten(yz_ref, y_scr, yz_ref.shape[1], ngrp, cg))
    glu = _sigmoid(_dot(ys.astype(BF16), wglu_ref[...]))
    y_s5 = _dot((ys * glu).astype(BF16), wbr_ref[...])
    gate = gate_ref[0].astype(F32)
    merged = _sigmoid(gate[:, :d]) * y_hg + _sigmoid(gate[:, d:]) * y_s5
    o_ref[0] = h_ref[0] + mod_ref[0, 2:3, :] * _dot(merged.astype(BF16), wout_ref[...])


def _merge(h, yh, yz, p_gate, mods, mod_row_fn, whg, wglu, wbr, wout, layer, ngrp, cg, tok_off_blocks):
    bsz, length, d = h.shape
    nt = length // TOK_TILE
    ct = TOK_TILE // S5_T
    seq = lambda rows, w: pl.BlockSpec((1, rows, w), lambda b, t: (b, t + tok_off_blocks, 0))
    return pl.pallas_call(
        functools.partial(_merge_kernel, ngrp=ngrp, cg=cg),
        grid=(bsz, nt),
        in_specs=[pl.BlockSpec((1, TOK_TILE, d), lambda b, t: (b, t, 0)),
                  seq(TOK_TILE, yh.shape[2]), seq(ct, yz.shape[2]), seq(TOK_TILE, p_gate.shape[2]),
                  pl.BlockSpec((1, 6, d), lambda b, t: (mod_row_fn(b), 0, 0)),
                  _layer_spec(whg, layer), _layer_spec(wglu, layer), _layer_spec(wbr, layer),
                  _layer_spec(wout, layer)],
        out_specs=pl.BlockSpec((1, TOK_TILE, d), lambda b, t: (b, t, 0)),
        out_shape=jax.ShapeDtypeStruct(h.shape, F32),
        scratch_shapes=[pltpu.VMEM((ngrp * cg // 128, TOK_TILE, 128), F32)],
        input_output_aliases={0: 0},
        compiler_params=pltpu.CompilerParams(
            dimension_semantics=("parallel", "parallel"), vmem_limit_bytes=VMEM_LIMIT),
        name="merge",
    )(h, yh, yz, p_gate, mods, whg, wglu, wbr, wout)


def _ffn_kernel(h_ref, mod_ref, g_ref, wab_ref, cw_ref, cb_ref, wd_ref, fg_ref,
                o_ref, xn_scr, up_scr, act_scr, *, n_tok, width, cw, mrows, final):
    j = pl.program_id(1)
    nj = pl.num_programs(1)
    vertical = n_tok > width
    nblk = n_tok // mrows
    rpb = mrows // width
    pad = width

    @pl.when(j == 0)
    def _():
        def norm_body(i, carry):
            r0 = pl.multiple_of(i * mrows, mrows)
            xn_scr[pl.ds(r0, mrows), :] = _modulated_norm(
                h_ref[0, pl.ds(r0, mrows), :], g_ref[...], mod_ref[0, 3:4, :], mod_ref[0, 4:5, :]).astype(BF16)
            o_ref[0, pl.ds(r0, mrows), :] = jnp.zeros((mrows, o_ref.shape[2]), F32)
            act_scr[pl.ds(r0, mrows), :] = jnp.zeros((mrows, cw), BF16)
            return carry

        lax.fori_loop(0, nblk, norm_body, 0)
        xn_scr[n_tok:n_tok + width, :] = jnp.zeros((width, xn_scr.shape[1]), BF16)

        def zero_body(i, carry):
            for copy in range(3):
                up_scr[copy, pl.ds(pl.multiple_of(i * width, 8), width), :] = jnp.zeros((width, 2 * cw), F32)
            return carry

        lax.fori_loop(0, up_scr.shape[1] // width, zero_body, 0)

    def up_rows(row0, m):
        up = _dot(xn_scr[pl.ds(row0, m), :], wab_ref[...])
        col = lax.broadcasted_iota(jnp.int32, (m, 1), 0) % width
        up_scr[0, pl.ds(pad + row0, m), :] = jnp.where(col == 0, 0.0, pltpu.roll(up, 1, axis=0))
        up_scr[1, pl.ds(pad + row0, m), :] = up
        up_scr[2, pl.ds(pad + row0, m), :] = jnp.where(col == width - 1, 0.0, pltpu.roll(up, m - 1, axis=0))

    def conv_row(row0, lane0):
        acc = cb_ref[:, lane0:lane0 + 128]
        for dr in ((-1, 0, 1) if vertical else (0,)):
            for dc in (-1, 0, 1):
                tap = (dr + 1) * 3 + (dc + 1)
                src = up_scr[dc + 1, pl.ds(pl.multiple_of(pad + row0 + dr * width, 8), width), lane0:lane0 + 128]
                acc = acc + cw_ref[tap:tap + 1, lane0:lane0 + 128] * src
        return acc

    def conv_block(blk):
        r0 = blk * mrows
        for lt in range(cw // 128):
            for r in range(rpb):
                row0 = r0 + r * width
                ca = conv_row(row0, lt * 128)
                cb = conv_row(row0, cw + lt * 128)
                act_scr[pl.ds(pl.multiple_of(row0, 16), width), lt * 128:(lt + 1) * 128] = (
                    ca * _sigmoid(ca) * cb).astype(BF16)

    if nblk == 1:
        up_rows(0, mrows)
        conv_block(0)
        o_ref[0] += _dot(act_scr[...], wd_ref[...])
    else:
        up_rows(0, width)

    def body(k, carry):
        rd = pl.multiple_of(jnp.maximum(k - 2, 0) * mrows, mrows)
        a_blk = act_scr[pl.ds(rd, mrows), :]
        a_blk = jnp.where(k >= 2, a_blk, jnp.zeros_like(a_blk))
        o_ref[0, pl.ds(rd, mrows), :] += _dot(a_blk, wd_ref[...])
        conv_block(jnp.clip(k - 1, 0, nblk - 1))
        up_rows(pl.multiple_of(jnp.minimum(k, nblk - 1) * mrows + width, width), mrows)
        return carry

    if nblk > 1:
        lax.fori_loop(0, nblk + 2, body, 0)

    @pl.when(j == nj - 1)
    def _():
        def fin_body(i, carry):
            r0 = pl.multiple_of(i * mrows, mrows)
            out = h_ref[0, pl.ds(r0, mrows), :] + mod_ref[0, 5:6, :] * o_ref[0, pl.ds(r0, mrows), :]
            if final:
                out = out * lax.rsqrt(jnp.mean(out * out, axis=-1, keepdims=True) + RMS_EPS) * fg_ref[...]
            o_ref[0, pl.ds(r0, mrows), :] = out
            return carry

        lax.fori_loop(0, nblk, fin_body, 0)


def _ffn(h, mods, mod_row_fn, norm_g, ffn_w, layer, final_g, width, final):
    bsz, n_tok, d = h.shape
    wab, cwf, cbf, wd = ffn_w
    cw = FFN_TILE
    nj = wd.shape[1] // cw
    return pl.pallas_call(
        functools.partial(_ffn_kernel, n_tok=n_tok, width=width, cw=cw, mrows=TOK_TILE, final=final),
        grid=(bsz, nj),
        in_specs=[pl.BlockSpec((1, n_tok, d), lambda b, j: (b, 0, 0), pipeline_mode=pl.Buffered(1)),
                  pl.BlockSpec((1, 6, d), lambda b, j: (mod_row_fn(b), 0, 0)),
                  pl.BlockSpec((1, d), lambda b, j: (0, 0)),
                  pl.BlockSpec((None, d, 2 * cw), lambda b, j: (layer, 0, j)),
                  pl.BlockSpec((None, 9, 2 * cw), lambda b, j: (layer, 0, j)),
                  pl.BlockSpec((None, 1, 2 * cw), lambda b, j: (layer, 0, j)),
                  pl.BlockSpec((None, cw, d), lambda b, j: (layer, j, 0)),
                  pl.BlockSpec((1, d), lambda b, j: (0, 0))],
        out_specs=pl.BlockSpec((1, n_tok, d), lambda b, j: (b, 0, 0)),
        out_shape=jax.ShapeDtypeStruct(h.shape, F32),
        scratch_shapes=[pltpu.VMEM((n_tok + width, d), BF16),
                        pltpu.VMEM((3, n_tok + 2 * width, 2 * cw), F32),
                        pltpu.VMEM((n_tok, cw), BF16)],
        input_output_aliases={0: 0},
        compiler_params=pltpu.CompilerParams(
            dimension_semantics=("parallel", "arbitrary"), vmem_limit_bytes=VMEM_LIMIT),
        name="ffn",
    )(h, mods, norm_g.reshape(1, d), wab, cwf, cbf, wd, final_g.reshape(1, d))


def _ffn_weights(w_up, conv_w, conv_b, w_down):
    depth, ffn, _ = w_down.shape
    cw = FFN_TILE
    fp = ((ffn + cw - 1) // cw) * cw
    nj = fp // cw

    def tiles(x):
        lead = x.shape[:-1]
        halves = []
        for part in (x[..., :ffn], x[..., ffn:]):
            part = jnp.pad(part, [(0, 0)] * len(lead) + [(0, fp - ffn)])
            halves.append(part.reshape(lead + (nj, cw)))
        return jnp.concatenate(halves, axis=-1).reshape(lead + (nj * 2 * cw,))

    wd = jnp.pad(w_down.astype(BF16), [(0, 0), (0, fp - ffn), (0, 0)])
    return (tiles(w_up.astype(BF16)), tiles(conv_w.reshape(depth, 9, 2 * ffn)),
            tiles(conv_b.reshape(depth, 1, 2 * ffn)), wd)


def _lower_bounds(lb_raw):
    p = jax.nn.softmax(lb_raw.astype(F32), axis=0)
    cs = jnp.cumsum(p, axis=0)
    return cs - cs[0:1]


def kernel(x, c, ctx, c_ctx, ada_w, ada_b, norm1_g, w_in, hg_lb_raw, hg_norm_g, w_hg_br, s5_a_re, s5_a_im, s5_log_dt, s5_b_re, s5_b_im, s5_c_re, s5_c_im, s5_d, w_s5_glu, w_s5_br, w_out, norm2_g, w_up, ffn_conv_w, ffn_conv_b, w_down, final_g):
    bsz, n_lat, d = x.shape
    n_ctx = ctx.shape[1]
    depth = ada_w.shape[0]
    dk = hg_norm_g.shape[1]
    hg_w = hg_lb_raw.shape[2]
    ngrp, nst, cg = s5_b_re.shape[1:]
    assert n_lat % TOK_TILE == 0 and n_ctx % TOK_TILE == 0 and n_lat % GLA_C == 0 and n_ctx % GLA_C == 0
    assert 5 * hg_w + ngrp * cg + 2 * d == w_in.shape[2] and S5_T * cg == 256 and 128 % cg == 0
    assert ngrp % (128 // cg) == 0

    rows = ((bsz + 1 + 7) // 8) * 8
    cond = jnp.zeros((rows, d), F32).at[:bsz].set(c).at[bsz].set(c_ctx)
    mods = _ada_mod(cond, ada_w, ada_b).reshape(depth, rows, 6, d)
    lat_row = lambda b: b
    ctx_row = lambda b: bsz
    lbs = _lower_bounds(hg_lb_raw)
    lat_blk = n_lat // TOK_TILE

    w_in_b, whg_b, wglu_b, wbr_b, wout_b = (w.astype(BF16) for w in (w_in, w_hg_br, w_s5_glu, w_s5_br, w_out))
    ffn_w = _ffn_weights(w_up, ffn_conv_w, ffn_conv_b, w_down)
    s5_tabs = jax.vmap(_s5_tables)(s5_a_re, s5_a_im, s5_log_dt, s5_b_re, s5_b_im, s5_c_re, s5_c_im)
    skip = jnp.broadcast_to(s5_d.astype(F32).reshape(depth, ngrp, 1, cg), (depth, ngrp, S5_T, cg))
    skip = skip.reshape(depth, 1, ngrp * S5_T * cg)

    h_lat, h_ctx = x, ctx
    for l in range(depth):
        last = l == depth - 1
        p_hg, z, p_gate = _proj_in(h_lat, h_ctx, mods[l], norm1_g[l], w_in_b, l, 5 * hg_w, ngrp, cg)

        o_fwd = _gla(p_hg, lbs[l, 0], hg_norm_g[l], None, n_lat, n_ctx, hg_w, dk, rev=False)
        yh = _gla(p_hg, lbs[l, 1], hg_norm_g[l], o_fwd, n_lat, n_ctx, hg_w, dk, rev=True)
        yz = _s5(z, s5_tabs, l, skip[l], n_lat // S5_T, n_ctx // S5_T)

        wts = (whg_b, wglu_b, wbr_b, wout_b, l, ngrp, cg)
        h_lat = _merge(h_lat, yh, yz, p_gate, mods[l], lat_row, *wts, 0)
        h_lat = _ffn(h_lat, mods[l], lat_row, norm2_g[l], ffn_w, l, final_g, GRID_W, last)
        if not last:
            h_ctx = _merge(h_ctx, yh, yz, p_gate, mods[l], ctx_row, *wts, lat_blk)
            h_ctx = _ffn(h_ctx, mods[l], ctx_row, norm2_g[l], ffn_w, l, final_g, n_ctx, False)
    return h_lat
```

```python
import functools
import math

import numpy as np
import jax
import jax.numpy as jnp
from jax import lax
from jax.experimental import pallas as pl
from jax.experimental.pallas import tpu as pltpu

GRID_W = 64
RMS_EPS = 1e-6
S5_MAX_RE = -1e-4
S5_T = 16
GLA_C = 128
HEADS_PER_ITER = 2
TOK_TILE = 256
FFN_TILE = 256
VMEM_LIMIT = 56 * 1024 * 1024

F32 = jnp.float32
BF16 = jnp.bfloat16
_HI = lax.Precision.HIGHEST


def _nt_dot(a, b):
    return lax.dot_general(a, b, (((1,), (1,)), ((), ())), preferred_element_type=F32)


def _tn_dot(a, b):
    return lax.dot_general(a, b, (((0,), (0,)), ((), ())), preferred_element_type=F32)


def _dot(a, b):
    return jnp.dot(a, b, preferred_element_type=F32)


def _sigmoid(x):
    return 1.0 / (1.0 + jnp.exp(-x))


def _const_spec(shape):
    nd = len(shape)
    return pl.BlockSpec(shape, lambda *_: (0,) * nd, pipeline_mode=pl.Buffered(1))


def _layer_spec(stacked, layer):
    rest = stacked.shape[1:]
    return pl.BlockSpec((None,) + rest, lambda *_: (layer,) + (0,) * len(rest), pipeline_mode=pl.Buffered(1))


def _ada_kernel(cond_ref, w_ref, b_ref, o_ref):
    cnd = cond_ref[...]
    s = cnd * _sigmoid(cnd)
    o_ref[0] = jnp.dot(s, w_ref[0], preferred_element_type=F32, precision=_HI) + b_ref[0]


def _ada_mod(cond, ada_w, ada_b):
    depth, d, n6 = ada_w.shape
    rows = cond.shape[0]
    tn = 512
    return pl.pallas_call(
        _ada_kernel,
        grid=(depth, n6 // tn),
        in_specs=[
            pl.BlockSpec((rows, d), lambda l, j: (0, 0)),
            pl.BlockSpec((1, d, tn), lambda l, j: (l, 0, j)),
            pl.BlockSpec((1, 1, tn), lambda l, j: (l, 0, j)),
        ],
        out_specs=pl.BlockSpec((1, rows, tn), lambda l, j: (l, 0, j)),
        out_shape=jax.ShapeDtypeStruct((depth, rows, n6), F32),
        name="ada_mod",
    )(cond, ada_w, ada_b.reshape(depth, 1, n6))


def _modulated_norm(x, g, shift, scale):
    ms = jnp.mean(x * x, axis=-1, keepdims=True)
    y = x * lax.rsqrt(ms + RMS_EPS) * g
    return y * (1.0 + scale) + shift


def _strip_masks(n_rows, cg):
    lane = lax.broadcasted_iota(jnp.int32, (n_rows, 128), 1)
    return [(lane >= j * cg) & (lane < (j + 1) * cg) for j in range(128 // cg)]


def _chunk_flatten(u_scr, z_ref, n_chunks, ngrp, cg):
    gpt = 128 // cg
    masks = _strip_masks(n_chunks, cg)
    for g in range(ngrp):
        k, i = divmod(g, gpt)
        for hf in range(S5_T // gpt):
            acc = None
            for j in range(gpt):
                src = u_scr[k, pl.ds(hf * gpt + j, n_chunks, stride=S5_T), :]
                shift = ((j - i) * cg) % 128
                if shift:
                    src = pltpu.roll(src, shift, axis=1)
                acc = src if acc is None else jnp.where(masks[j], src, acc)
            tile = g * (S5_T // gpt) + hf
            z_ref[0, :, tile * 128:(tile + 1) * 128] = acc.astype(z_ref.dtype)


def _chunk_unflatten(yz_ref, scr, n_chunks, ngrp, cg):
    gpt = 128 // cg
    masks = _strip_masks(n_chunks, cg)
    for k in range(ngrp // gpt):
        for hf in range(S5_T // gpt):
            srcs = []
            for i in range(gpt):
                tile = (k * gpt + i) * (S5_T // gpt) + hf
                srcs.append(yz_ref[0, :, tile * 128:(tile + 1) * 128].astype(F32))
            for j in range(gpt):
                acc = None
                for i in range(gpt):
                    shift = ((i - j) * cg) % 128
                    src = pltpu.roll(srcs[i], shift, axis=1) if shift else srcs[i]
                    acc = src if acc is None else jnp.where(masks[i], src, acc)
                t = hf * gpt + j
                scr[k, t * n_chunks:(t + 1) * n_chunks, :] = acc
    return jnp.concatenate(
        [jnp.concatenate([scr[k, pl.ds(ch, S5_T, stride=n_chunks), :] for k in range(scr.shape[0])], axis=1)
         for ch in range(n_chunks)], axis=0)


def _proj_in_kernel(hl_ref, hc_ref, mod_ref, g_ref, w_ref, hg_ref, z_ref, gate_ref, xn_scr, u_scr,
                    *, nt_lat, ngrp, cg):
    def norm(h_ref):
        xn_scr[...] = _modulated_norm(h_ref[0], g_ref[...], mod_ref[0, 0:1, :], mod_ref[0, 1:2, :]).astype(BF16)

    pl.when(pl.program_id(1) < nt_lat)(lambda: norm(hl_ref))
    pl.when(pl.program_id(1) >= nt_lat)(lambda: norm(hc_ref))
    xn = xn_scr[...]
    hg_w, s5_w = hg_ref.shape[2], ngrp * cg

    def project(o_ref, col0):
        width = o_ref.shape[2]
        step = math.gcd(512, width)
        for c in range(0, width, step):
            o_ref[0, :, c:c + step] = _dot(xn, w_ref[:, col0 + c:col0 + c + step]).astype(o_ref.dtype)

    project(hg_ref, 0)
    project(gate_ref, hg_w + s5_w)
    u = _dot(xn, w_ref[:, hg_w:hg_w + s5_w])
    for k in range(s5_w // 128):
        u_scr[k] = u[:, k * 128:(k + 1) * 128]
    _chunk_flatten(u_scr, z_ref, xn_scr.shape[0] // S5_T, ngrp, cg)


def _proj_in(h_lat, h_ctx, mods, norm_g, w_stack, layer, hg_cols, ngrp, cg):
    bsz, n_lat, d = h_lat.shape
    n_ctx = h_ctx.shape[1]
    total = n_lat + n_ctx
    nt_lat, nt_ctx = n_lat // TOK_TILE, n_ctx // TOK_TILE
    s5_w = ngrp * cg
    gate_cols = w_stack.shape[2] - hg_cols - s5_w
    ct = TOK_TILE // S5_T
    return pl.pallas_call(
        functools.partial(_proj_in_kernel, nt_lat=nt_lat, ngrp=ngrp, cg=cg),
        grid=(bsz, nt_lat + nt_ctx),
        in_specs=[
            pl.BlockSpec((1, TOK_TILE, d), lambda b, t: (b, jnp.minimum(t, nt_lat - 1), 0)),
            pl.BlockSpec((1, TOK_TILE, d), lambda b, t: (b, jnp.maximum(t - nt_lat, 0), 0)),
            pl.BlockSpec((1, 6, d), lambda b, t: (jnp.where(t < nt_lat, b, bsz), 0, 0)),
            pl.BlockSpec((1, d), lambda b, t: (0, 0)),
            _layer_spec(w_stack, layer),
        ],
        out_specs=[pl.BlockSpec((1, TOK_TILE, hg_cols), lambda b, t: (b, t, 0)),
                   pl.BlockSpec((1, ct, s5_w * S5_T), lambda b, t: (b, t, 0)),
                   pl.BlockSpec((1, TOK_TILE, gate_cols), lambda b, t: (b, t, 0))],
        out_shape=[jax.ShapeDtypeStruct((bsz, total, hg_cols), BF16),
                   jax.ShapeDtypeStruct((bsz, total // S5_T, s5_w * S5_T), BF16),
                   jax.ShapeDtypeStruct((bsz, total, gate_cols), BF16)],
        scratch_shapes=[pltpu.VMEM((TOK_TILE, d), BF16), pltpu.VMEM((s5_w // 128, TOK_TILE, 128), F32)],
        compiler_params=pltpu.CompilerParams(
            dimension_semantics=("parallel", "arbitrary"), vmem_limit_bytes=VMEM_LIMIT),
        name="proj_in",
    )(h_lat, h_ctx, mods, norm_g.reshape(1, d), w_stack)


def _gla_tables(c, rev):
    t = np.arange(c)[:, None]
    s = np.arange(c)[None, :]
    x = t ^ s
    lev = np.full((c, c), -2, np.int32)
    causal = (t < s) if rev else (t > s)
    with np.errstate(divide="ignore"):
        hb = np.floor(np.log2(np.maximum(x, 1))).astype(np.int32)
    lev = np.where(causal, hb, lev)
    lev = np.where(t == s, -1, lev).astype(np.int32)
    tri = ((t <= s) if rev else (t >= s)).astype(np.float32)
    return jnp.asarray(lev), jnp.asarray(tri, dtype=BF16)


def _gla_kernel(*refs, rev, readout, c, nheads, dk):
    if readout:
        (q_ref, fz_ref, i_ref, lb_ref, lev_ref, tri_ref, g_ref, of_ref, ng_ref,
         out_ref, st_ref, b_scr_all) = refs
    else:
        q_ref, fz_ref, i_ref, lb_ref, lev_ref, tri_ref, out_ref, st_ref, b_scr_all = refs
    nlev = int(math.log2(c))

    @pl.when(pl.program_id(1) == 0)
    def _():
        st_ref[...] = jnp.zeros_like(st_ref)

    lev = lev_ref[...]
    tri = tri_ref[...]
    row = lax.broadcasted_iota(jnp.int32, (c, 1), 0)

    def head(hd, b_scr):
        sl = pl.ds(pl.multiple_of(hd * dk, dk), dk)
        z = fz_ref[0, :, sl].astype(F32)
        qz = q_ref[0, :, sl].astype(F32)
        v = i_ref[0, :, sl]
        lb = lb_ref[0:1, sl]

        e = jnp.exp(-jnp.abs(z))
        den = 1.0 + e
        inv = 1.0 / den
        pos = z >= 0
        sig_pos = jnp.where(pos, 1.0, e) * inv
        sig_neg = jnp.where(pos, e, 1.0) * inv
        f = lb + (1.0 - lb) * sig_pos
        a = jnp.where(lb > 0, jnp.log(f), jnp.minimum(z, 0.0) - jnp.log(den))
        k = (1.0 - lb) * sig_neg
        q = qz * _sigmoid(qz)

        a_hi = a.astype(BF16)
        a_lo = (a - a_hi.astype(F32)).astype(BF16)
        bb = _dot(tri, jnp.concatenate([a_hi, a_lo], axis=1))
        b = bb[:, :dk] + bb[:, dk:]
        b_scr[...] = b

        odd = (row & 1) == 1
        f_prev = pltpu.roll(f, 1, axis=0)
        f_next = pltpu.roll(f, c - 1, axis=0)
        qf = q * f
        scores = jnp.zeros((c, c), F32)
        for level in range(nlev):
            half = 1 << level
            if level == 0:
                xq, xk = qf, k
            elif level == 1:
                if rev:
                    xq, xk = qf * jnp.where(odd, 1.0, f_next), k * jnp.where(odd, f_prev, 1.0)
                else:
                    xq, xk = qf * jnp.where(odd, f_prev, 1.0), k * jnp.where(odd, 1.0, f_next)
            else:
                blk = 2 * half
                pieces = []
                for j in range(c // blk):
                    brow = b_scr[pl.ds(j * blk + (half if rev else half - 1), 1), :]
                    pieces.append(jnp.broadcast_to(brow, (blk, dk)))
                ref_b = jnp.concatenate(pieces, axis=0) if len(pieces) > 1 else pieces[0]
                decay = jnp.exp(-jnp.abs(b - ref_b))
                xq, xk = q * decay, k * decay
            scores = jnp.where(lev == level, _nt_dot(xq.astype(BF16), xk.astype(BF16)), scores)
        scores = jnp.where(lev == -1, jnp.sum(q * k, axis=-1, keepdims=True), scores)
        o = _dot(scores.astype(BF16), v)

        st = st_ref[hd]
        b_end = b_scr[pl.ds(0 if rev else c - 1, 1), :]
        o = o + _nt_dot((q * jnp.exp(b)).astype(BF16), st.astype(BF16))
        k_out = (k * jnp.exp(b_end - b)).astype(BF16)
        st_ref[hd] = jnp.exp(b_end) * st + _tn_dot(v, k_out)

        if readout:
            o = o + of_ref[0, :, sl].astype(F32)
            o = o * lax.rsqrt(jnp.mean(o * o, axis=-1, keepdims=True) + RMS_EPS) * ng_ref[...]
            gz = g_ref[0, :, sl].astype(F32)
            o = o * (gz * _sigmoid(gz))
        out_ref[0, :, sl] = o.astype(out_ref.dtype)

    def head_pair(i, carry):
        for slot in range(HEADS_PER_ITER):
            head(i * HEADS_PER_ITER + slot, b_scr_all.at[slot])
        return carry

    lax.fori_loop(0, nheads // HEADS_PER_ITER, head_pair, 0)


def _gla(p_hg, lb_row, norm_g, o_fwd, n_lat, n_ctx, width, dk, rev):
    bsz, total, _ = p_hg.shape
    c = GLA_C
    nl, nc = n_lat // c, n_ctx // c
    nheads = width // dk
    readout = rev

    if rev:
        def chunk(i):
            return jnp.where(i < nc, nl + nc - 1 - i, nl + nc - 1 - i)
    else:
        def chunk(i):
            return jnp.where(i < nc, nl + i, i - nc)

    def col_spec(colblk):
        return pl.BlockSpec((1, c, width), lambda b, i: (b, chunk(i), colblk))

    lev, tri = _gla_tables(c, rev)
    in_specs = [col_spec(0), col_spec(2 if rev else 1), col_spec(3),
                pl.BlockSpec((1, width), lambda b, i: (0, 0)),
                pl.BlockSpec((c, c), lambda b, i: (0, 0)),
                pl.BlockSpec((c, c), lambda b, i: (0, 0))]
    args = [p_hg, p_hg, p_hg, lb_row.reshape(1, width), lev, tri]
    if readout:
        in_specs += [col_spec(4),
                     pl.BlockSpec((1, c, width), lambda b, i: (b, chunk(i), 0)),
                     pl.BlockSpec((1, dk), lambda b, i: (0, 0))]
        args += [p_hg, o_fwd, norm_g.reshape(1, dk)]
    return pl.pallas_call(
        functools.partial(_gla_kernel, rev=rev, readout=readout, c=c, nheads=nheads, dk=dk),
        grid=(bsz, nl + nc),
        in_specs=in_specs,
        out_specs=pl.BlockSpec((1, c, width), lambda b, i: (b, chunk(i), 0)),
        out_shape=jax.ShapeDtypeStruct((bsz, total, width), BF16),
        scratch_shapes=[pltpu.VMEM((nheads, dk, dk), F32), pltpu.VMEM((HEADS_PER_ITER, c, dk), F32)],
        compiler_params=pltpu.CompilerParams(
            dimension_semantics=("parallel", "arbitrary"), vmem_limit_bytes=VMEM_LIMIT),
        name="gla_bwd" if rev else "gla_fwd",
    )(*args)


def _s5_tables(a_re, a_im, log_dt, b_re, b_im, c_re, c_im):
    t = S5_T
    ngrp, nst, cg = b_re.shape
    a_re = jnp.minimum(a_re.astype(F32), S5_MAX_RE)
    a_im = a_im.astype(F32)
    dt = jnp.exp(log_dt.astype(F32))[..., None]
    mag = jnp.exp(dt * a_re)
    abr, abi = mag * jnp.cos(dt * a_im), mag * jnp.sin(dt * a_im)
    den = a_re * a_re + a_im * a_im
    nr, ni = abr - 1.0, abi
    f_re = ((nr * a_re + ni * a_im) / den)[..., None]
    f_im = ((ni * a_re - nr * a_im) / den)[..., None]
    b_re, b_im = b_re.astype(F32)[None], b_im.astype(F32)[None]
    bb_re = f_re * b_re - f_im * b_im
    bb_im = f_re * b_im + f_im * b_re
    pr, pi = [jnp.ones_like(abr)], [jnp.zeros_like(abr)]
    for _ in range(t):
        pr.append(pr[-1] * abr - pi[-1] * abi)
        pi.append(pr[-2] * abi + pi[-1] * abr)
    pr, pi = jnp.stack(pr), jnp.stack(pi)
    c_re, c_im = c_re.astype(F32), c_im.astype(F32)
    l_re = c_re[None, None] * pr[:, :, :, None, :] - c_im[None, None] * pi[:, :, :, None, :]
    l_im = c_re[None, None] * pi[:, :, :, None, :] + c_im[None, None] * pr[:, :, :, None, :]
    kern = (jnp.einsum("drgcn,rgnk->drgck", l_re[:t], bb_re, precision=_HI)
            - jnp.einsum("drgcn,rgnk->drgck", l_im[:t], bb_im, precision=_HI))
    lag = np.arange(t)[:, None] - np.arange(t)[None, :]
    sel = np.stack([lag[:, :, None] == np.arange(t), -lag[:, :, None] == np.arange(t)])
    mt = jnp.einsum("rtsd,drgck->gsktc", jnp.asarray(sel, BF16), kern.astype(BF16),
                    preferred_element_type=F32).reshape(ngrp, t * cg, t * cg)
    pf_r, pf_i = pr[t - 1 - np.arange(t), 0], pi[t - 1 - np.arange(t), 0]
    pb_r, pb_i = pr[np.arange(t), 1], pi[np.arange(t), 1]

    def drive(p_r, p_i, r):
        g_r = p_r[..., None] * bb_re[r][None] - p_i[..., None] * bb_im[r][None]
        g_i = p_r[..., None] * bb_im[r][None] + p_i[..., None] * bb_re[r][None]
        to = lambda x: x.transpose(1, 0, 3, 2).reshape(ngrp, t * cg, nst)
        return to(g_r), to(g_i)

    gf_r, gf_i = drive(pf_r, pf_i, 0)
    gb_r, gb_i = drive(pb_r, pb_i, 1)
    def read(idx, r):
        rr = l_re[idx, r].transpose(1, 3, 0, 2).reshape(ngrp, nst, t * cg)
        ri = -l_im[idx, r].transpose(1, 3, 0, 2).reshape(ngrp, nst, t * cg)
        return rr, ri

    of_r, of_i = read(np.arange(t) + 1, 0)
    ob_r, ob_i = read(t - np.arange(t), 1)

    eye = jnp.eye(2, dtype=F32)
    g4 = jnp.stack([gf_r, gf_i, gb_r, gb_i], axis=2).reshape(ngrp // 2, 2, t * cg, 4, 1, nst)
    gin = (g4 * eye[None, :, None, None, :, None]).reshape(ngrp // 2, 2 * t * cg, 8 * nst)
    o4 = jnp.stack([of_r, of_i, ob_r, ob_i], axis=1).reshape(ngrp // 2, 2, 4, nst, 1, t * cg)
    o4 = o4.transpose(0, 2, 1, 3, 4, 5)
    gout = (o4 * eye[None, None, :, None, :, None]).reshape(ngrp // 2, 8 * nst, 2 * t * cg)
    a_chunk = jnp.stack([pr[t, 0], pi[t, 0], pr[t, 1], pi[t, 1]]).reshape(4, ngrp * nst)
    return mt.astype(BF16), gin.astype(BF16), gout.astype(BF16), a_chunk


def _s5_kernel(z_ref, mt_ref, gin_ref, gout_ref, ac_ref, skip_ref, y_ref, d_scr, x_scr,
               *, n_lat, n_ctx, npairs, pw):
    nrows = n_lat + n_ctx
    gw = pw // 2
    tw = pw // 4
    for p in range(npairs):
        d = _dot(z_ref[0, :, p * pw:(p + 1) * pw], gin_ref[p])
        for j in range(4):
            d_scr[j, :, p * tw:(p + 1) * tw] = d[:, j * tw:(j + 1) * tw]

    def scan(plane, forward):
        ar = ac_ref[plane:plane + 1, :]
        ai = ac_ref[plane + 1:plane + 2, :]

        def body(i, carry):
            xr, xi = carry
            if forward:
                n = jnp.where(i < n_ctx, n_lat + i, i - n_ctx)
            else:
                n = nrows - 1 - i
            x_scr[plane, pl.ds(n, 1), :] = xr
            x_scr[plane + 1, pl.ds(n, 1), :] = xi
            dr = d_scr[plane, pl.ds(n, 1), :]
            di = d_scr[plane + 1, pl.ds(n, 1), :]
            return ar * xr - ai * xi + dr, ar * xi + ai * xr + di

        zero = jnp.zeros((1, d_scr.shape[2]), F32)
        lax.fori_loop(0, nrows, body, (zero, zero))

    scan(0, True)
    scan(2, False)

    for p in range(npairs):
        xp = jnp.concatenate([x_scr[j, :, p * tw:(p + 1) * tw] for j in range(4)], axis=1).astype(BF16)
        y = _dot(xp, gout_ref[p])
        for g in range(2):
            lo = p * pw + g * gw
            zg = z_ref[0, :, lo:lo + gw]
            yg = y[:, g * gw:(g + 1) * gw] + _dot(zg, mt_ref[2 * p + g]) + skip_ref[:, lo:lo + gw] * zg.astype(F32)
            y_ref[0, :, lo:lo + gw] = yg.astype(y_ref.dtype)


def _s5(zf, tables, layer, skip_flat, n_lat_chunks, n_ctx_chunks):
    bsz, nrows, wide = zf.shape
    mt, gin, gout, a_chunk = tables
    npairs, pw = gin.shape[1:3]
    planes = a_chunk.shape[2]
    return pl.pallas_call(
        functools.partial(_s5_kernel, n_lat=n_lat_chunks, n_ctx=n_ctx_chunks, npairs=npairs, pw=pw),
        grid=(bsz,),
        in_specs=[pl.BlockSpec((1, nrows, wide), lambda b: (b, 0, 0)),
                  _layer_spec(mt, layer), _layer_spec(gin, layer), _layer_spec(gout, layer),
                  _layer_spec(a_chunk, layer), _const_spec(skip_flat.shape)],
        out_specs=pl.BlockSpec((1, nrows, wide), lambda b: (b, 0, 0)),
        out_shape=jax.ShapeDtypeStruct((bsz, nrows, wide), BF16),
        scratch_shapes=[pltpu.VMEM((4, nrows, planes), F32), pltpu.VMEM((4, nrows, planes), F32)],
        compiler_params=pltpu.CompilerParams(
            dimension_semantics=("parallel",), vmem_limit_bytes=VMEM_LIMIT),
        name="s5",
    )(zf, mt, gin, gout, a_chunk, skip_flat)


def _gelu_tanh(x):
    return 0.5 * x * (1.0 + jnp.tanh(math.sqrt(2.0 / math.pi) * (x + 0.044715 * (x * x * x))))


def _merge_kernel(h_ref, yh_ref, yz_ref, gate_ref, mod_ref, whg_ref, wglu_ref, wbr_ref, wout_ref, o_ref, y_scr,
                  *, ngrp, cg):
    d = h_ref.shape[2]
    y_hg = _dot(yh_ref[0], whg_ref[...])
    ys = _gelu_tanh(_chunk_unflatten(yz_ref, y_scr, yz_ref.shape[1], ngrp, cg))
    glu = _sigmoid(_dot(ys.astype(BF16), wglu_ref[...]))
    y_s5 = _dot((ys * glu).astype(BF16), wbr_ref[...])
    gate = gate_ref[0].astype(F32)
    merged = _sigmoid(gate[:, :d]) * y_hg + _sigmoid(gate[:, d:]) * y_s5
    o_ref[0] = h_ref[0] + mod_ref[0, 2:3, :] * _dot(merged.astype(BF16), wout_ref[...])


def _merge(h, yh, yz, p_gate, mods, mod_row_fn, whg, wglu, wbr, wout, layer, ngrp, cg, tok_off_blocks):
    bsz, length, d = h.shape
    nt = length // TOK_TILE
    ct = TOK_TILE // S5_T
    seq = lambda rows, w: pl.BlockSpec((1, rows, w), lambda b, t: (b, t + tok_off_blocks, 0))
    return pl.pallas_call(
        functools.partial(_merge_kernel, ngrp=ngrp, cg=cg),
        grid=(bsz, nt),
        in_specs=[pl.BlockSpec((1, TOK_TILE, d), lambda b, t: (b, t, 0)),
                  seq(TOK_TILE, yh.shape[2]), seq(ct, yz.shape[2]), seq(TOK_TILE, p_gate.shape[2]),
                  pl.BlockSpec((1, 6, d), lambda b, t: (mod_row_fn(b), 0, 0)),
                  _layer_spec(whg, layer), _layer_spec(wglu, layer), _layer_spec(wbr, layer),
                  _layer_spec(wout, layer)],
        out_specs=pl.BlockSpec((1, TOK_TILE, d), lambda b, t: (b, t, 0)),
        out_shape=jax.ShapeDtypeStruct(h.shape, F32),
        scratch_shapes=[pltpu.VMEM((ngrp * cg // 128, TOK_TILE, 128), F32)],
        input_output_aliases={0: 0},
        compiler_params=pltpu.CompilerParams(
            dimension_semantics=("parallel", "parallel"), vmem_limit_bytes=VMEM_LIMIT),
        name="merge",
    )(h, yh, yz, p_gate, mods, whg, wglu, wbr, wout)


def _ffn_kernel(h_ref, mod_ref, g_ref, wab_ref, cw_ref, cb_ref, wd_ref, fg_ref,
                o_ref, xn_scr, up_scr, act_scr, *, n_tok, width, cw, mrows, final):
    j = pl.program_id(1)
    nj = pl.num_programs(1)
    vertical = n_tok > width
    nblk = n_tok // mrows
    rpb = mrows // width
    pad = width

    @pl.when(j == 0)
    def _():
        def norm_body(i, carry):
            r0 = pl.multiple_of(i * mrows, mrows)
            xn_scr[pl.ds(r0, mrows), :] = _modulated_norm(
                h_ref[0, pl.ds(r0, mrows), :], g_ref[...], mod_ref[0, 3:4, :], mod_ref[0, 4:5, :]).astype(BF16)
            o_ref[0, pl.ds(r0, mrows), :] = jnp.zeros((mrows, o_ref.shape[2]), F32)
            return carry

        lax.fori_loop(0, nblk, norm_body, 0)
        xn_scr[n_tok:n_tok + width, :] = jnp.zeros((width, xn_scr.shape[1]), BF16)
        for copy in range(3):
            up_scr[copy, 0:pad, :] = jnp.zeros((pad, 2 * cw), F32)
            up_scr[copy, pad + n_tok:pad + n_tok + width, :] = jnp.zeros((width, 2 * cw), F32)

    def up_rows(row0, m):
        up = _dot(xn_scr[pl.ds(row0, m), :], wab_ref[...])
        col = lax.broadcasted_iota(jnp.int32, (m, 1), 0) % width
        up_scr[0, pl.ds(pad + row0, m), :] = jnp.where(col == 0, 0.0, pltpu.roll(up, 1, axis=0))
        up_scr[1, pl.ds(pad + row0, m), :] = up
        up_scr[2, pl.ds(pad + row0, m), :] = jnp.where(col == width - 1, 0.0, pltpu.roll(up, m - 1, axis=0))

    def conv_row(row0, lane0):
        acc = cb_ref[:, lane0:lane0 + 128]
        for dr in ((-1, 0, 1) if vertical else (0,)):
            for dc in (-1, 0, 1):
                tap = (dr + 1) * 3 + (dc + 1)
                src = up_scr[dc + 1, pl.ds(pl.multiple_of(pad + row0 + dr * width, 8), width), lane0:lane0 + 128]
                acc = acc + cw_ref[tap:tap + 1, lane0:lane0 + 128] * src
        return acc

    def conv_block(blk):
        r0 = blk * mrows
        for lt in range(cw // 128):
            for r in range(rpb):
                row0 = r0 + r * width
                ca = conv_row(row0, lt * 128)
                cb = conv_row(row0, cw + lt * 128)
                act_scr[pl.ds(pl.multiple_of(row0, 16), width), lt * 128:(lt + 1) * 128] = (
                    ca * _sigmoid(ca) * cb).astype(BF16)

    def down_rows(row0):
        o_ref[0, pl.ds(row0, mrows), :] += _dot(act_scr[pl.ds(row0, mrows), :], wd_ref[...])

    if nblk == 1:
        up_rows(0, mrows)
        conv_block(0)
        down_rows(0)
    else:
        up_rows(0, width)

        def body(k, carry):
            r0 = pl.multiple_of(k * mrows, mrows)
            up_rows(r0 + width, mrows)
            conv_block(k)
            down_rows(r0)
            return carry

        lax.fori_loop(0, nblk, body, 0)

    @pl.when(j == nj - 1)
    def _():
        def fin_body(i, carry):
            r0 = pl.multiple_of(i * mrows, mrows)
            out = h_ref[0, pl.ds(r0, mrows), :] + mod_ref[0, 5:6, :] * o_ref[0, pl.ds(r0, mrows), :]
            if final:
                out = out * lax.rsqrt(jnp.mean(out * out, axis=-1, keepdims=True) + RMS_EPS) * fg_ref[...]
            o_ref[0, pl.ds(r0, mrows), :] = out
            return carry

        lax.fori_loop(0, nblk, fin_body, 0)


def _ffn(h, mods, mod_row_fn, norm_g, ffn_w, layer, final_g, width, final):
    bsz, n_tok, d = h.shape
    wab, cwf, cbf, wd = ffn_w
    cw = FFN_TILE
    nj = wd.shape[1] // cw
    return pl.pallas_call(
        functools.partial(_ffn_kernel, n_tok=n_tok, width=width, cw=cw, mrows=TOK_TILE, final=final),
        grid=(bsz, nj),
        in_specs=[pl.BlockSpec((1, n_tok, d), lambda b, j: (b, 0, 0), pipeline_mode=pl.Buffered(1)),
                  pl.BlockSpec((1, 6, d), lambda b, j: (mod_row_fn(b), 0, 0)),
                  pl.BlockSpec((1, d), lambda b, j: (0, 0)),
                  pl.BlockSpec((None, d, 2 * cw), lambda b, j: (layer, 0, j)),
                  pl.BlockSpec((None, 9, 2 * cw), lambda b, j: (layer, 0, j)),
                  pl.BlockSpec((None, 1, 2 * cw), lambda b, j: (layer, 0, j)),
                  pl.BlockSpec((None, cw, d), lambda b, j: (layer, j, 0)),
                  pl.BlockSpec((1, d), lambda b, j: (0, 0))],
        out_specs=pl.BlockSpec((1, n_tok, d), lambda b, j: (b, 0, 0)),
        out_shape=jax.ShapeDtypeStruct(h.shape, F32),
        scratch_shapes=[pltpu.VMEM((n_tok + width, d), BF16),
                        pltpu.VMEM((3, n_tok + 2 * width, 2 * cw), F32),
                        pltpu.VMEM((n_tok, cw), BF16)],
        input_output_aliases={0: 0},
        compiler_params=pltpu.CompilerParams(
            dimension_semantics=("parallel", "arbitrary"), vmem_limit_bytes=VMEM_LIMIT),
        name="ffn",
    )(h, mods, norm_g.reshape(1, d), wab, cwf, cbf, wd, final_g.reshape(1, d))


def _ffn_weights(w_up, conv_w, conv_b, w_down):
    depth, ffn, _ = w_down.shape
    cw = FFN_TILE
    fp = ((ffn + cw - 1) // cw) * cw
    nj = fp // cw

    def tiles(x):
        lead = x.shape[:-1]
        halves = []
        for part in (x[..., :ffn], x[..., ffn:]):
            part = jnp.pad(part, [(0, 0)] * len(lead) + [(0, fp - ffn)])
            halves.append(part.reshape(lead + (nj, cw)))
        return jnp.concatenate(halves, axis=-1).reshape(lead + (nj * 2 * cw,))

    wd = jnp.pad(w_down.astype(BF16), [(0, 0), (0, fp - ffn), (0, 0)])
    return (tiles(w_up.astype(BF16)), tiles(conv_w.reshape(depth, 9, 2 * ffn)),
            tiles(conv_b.reshape(depth, 1, 2 * ffn)), wd)


def _lower_bounds(lb_raw):
    p = jax.nn.softmax(lb_raw.astype(F32), axis=0)
    cs = jnp.cumsum(p, axis=0)
    return cs - cs[0:1]


def kernel(x, c, ctx, c_ctx, ada_w, ada_b, norm1_g, w_in, hg_lb_raw, hg_norm_g, w_hg_br, s5_a_re, s5_a_im, s5_log_dt, s5_b_re, s5_b_im, s5_c_re, s5_c_im, s5_d, w_s5_glu, w_s5_br, w_out, norm2_g, w_up, ffn_conv_w, ffn_conv_b, w_down, final_g):
    bsz, n_lat, d = x.shape
    n_ctx = ctx.shape[1]
    depth = ada_w.shape[0]
    dk = hg_norm_g.shape[1]
    hg_w = hg_lb_raw.shape[2]
    ngrp, nst, cg = s5_b_re.shape[1:]
    assert n_lat % TOK_TILE == 0 and n_ctx % TOK_TILE == 0 and n_lat % GLA_C == 0 and n_ctx % GLA_C == 0
    assert 5 * hg_w + ngrp * cg + 2 * d == w_in.shape[2] and S5_T * cg == 256 and 128 % cg == 0
    assert ngrp % (128 // cg) == 0

    rows = ((bsz + 1 + 7) // 8) * 8
    cond = jnp.zeros((rows, d), F32).at[:bsz].set(c).at[bsz].set(c_ctx)
    mods = _ada_mod(cond, ada_w, ada_b).reshape(depth, rows, 6, d)
    lat_row = lambda b: b
    ctx_row = lambda b: bsz
    lbs = _lower_bounds(hg_lb_raw)
    lat_blk = n_lat // TOK_TILE

    w_in_b, whg_b, wglu_b, wbr_b, wout_b = (w.astype(BF16) for w in (w_in, w_hg_br, w_s5_glu, w_s5_br, w_out))
    ffn_w = _ffn_weights(w_up, ffn_conv_w, ffn_conv_b, w_down)
    s5_tabs = jax.vmap(_s5_tables)(s5_a_re, s5_a_im, s5_log_dt, s5_b_re, s5_b_im, s5_c_re, s5_c_im)
    skip = jnp.broadcast_to(s5_d.astype(F32).reshape(depth, ngrp, 1, cg), (depth, ngrp, S5_T, cg))
    skip = skip.reshape(depth, 1, ngrp * S5_T * cg)

    h_lat, h_ctx = x, ctx
    for l in range(depth):
        last = l == depth - 1
        p_hg, z, p_gate = _proj_in(h_lat, h_ctx, mods[l], norm1_g[l], w_in_b, l, 5 * hg_w, ngrp, cg)

        o_fwd = _gla(p_hg, lbs[l, 0], hg_norm_g[l], None, n_lat, n_ctx, hg_w, dk, rev=False)
        yh = _gla(p_hg, lbs[l, 1], hg_norm_g[l], o_fwd, n_lat, n_ctx, hg_w, dk, rev=True)
        yz = _s5(z, s5_tabs, l, skip[l], n_lat // S5_T, n_ctx // S5_T)

        wts = (whg_b, wglu_b, wbr_b, wout_b, l, ngrp, cg)
        h_lat = _merge(h_lat, yh, yz, p_gate, mods[l], lat_row, *wts, 0)
        h_lat = _ffn(h_lat, mods[l], lat_row, norm2_g[l], ffn_w, l, final_g, GRID_W, last)
        if not last:
            h_ctx = _merge(h_ctx, yh, yz, p_gate, mods[l], ctx_row, *wts, lat_blk)
            h_ctx = _ffn(h_ctx, mods[l], ctx_row, norm2_g[l], ffn_w, l, final_g, n_ctx, False)
    return h_lat
```

```python
import functools
import math

import numpy as np
import jax
import jax.numpy as jnp
from jax import lax
from jax.experimental import pallas as pl
from jax.experimental.pallas import tpu as pltpu

GRID_W = 64
RMS_EPS = 1e-6
S5_MAX_RE = -1e-4
S5_T = 16
GLA_C = 128
HEADS_PER_ITER = 4
TOK_TILE = 256
FFN_TILE = 256
FFN_STREAMS = 2
VMEM_LIMIT = 56 * 1024 * 1024

F32 = jnp.float32
BF16 = jnp.bfloat16
_HI = lax.Precision.HIGHEST


def _nt_dot(a, b):
    return lax.dot_general(a, b, (((1,), (1,)), ((), ())), preferred_element_type=F32)


def _tn_dot(a, b):
    return lax.dot_general(a, b, (((0,), (0,)), ((), ())), preferred_element_type=F32)


def _dot(a, b):
    return jnp.dot(a, b, preferred_element_type=F32)


def _sigmoid(x):
    return 1.0 / (1.0 + jnp.exp(-x))


def _const_spec(shape):
    nd = len(shape)
    return pl.BlockSpec(shape, lambda *_: (0,) * nd, pipeline_mode=pl.Buffered(1))


def _layer_spec(stacked, layer):
    rest = stacked.shape[1:]
    return pl.BlockSpec((None,) + rest, lambda *_: (layer,) + (0,) * len(rest), pipeline_mode=pl.Buffered(1))


def _ada_kernel(cond_ref, w_ref, b_ref, o_ref):
    cnd = cond_ref[...]
    s = cnd * _sigmoid(cnd)
    o_ref[0] = jnp.dot(s, w_ref[0], preferred_element_type=F32, precision=_HI) + b_ref[0]


def _ada_mod(cond, ada_w, ada_b):
    depth, d, n6 = ada_w.shape
    rows = cond.shape[0]
    tn = 512
    return pl.pallas_call(
        _ada_kernel,
        grid=(depth, n6 // tn),
        in_specs=[
            pl.BlockSpec((rows, d), lambda l, j: (0, 0)),
            pl.BlockSpec((1, d, tn), lambda l, j: (l, 0, j)),
            pl.BlockSpec((1, 1, tn), lambda l, j: (l, 0, j)),
        ],
        out_specs=pl.BlockSpec((1, rows, tn), lambda l, j: (l, 0, j)),
        out_shape=jax.ShapeDtypeStruct((depth, rows, n6), F32),
        name="ada_mod",
    )(cond, ada_w, ada_b.reshape(depth, 1, n6))


def _modulated_norm(x, g, shift, scale):
    ms = jnp.mean(x * x, axis=-1, keepdims=True)
    y = x * lax.rsqrt(ms + RMS_EPS) * g
    return y * (1.0 + scale) + shift


def _strip_masks(n_rows, cg):
    lane = lax.broadcasted_iota(jnp.int32, (n_rows, 128), 1)
    return [(lane >= j * cg) & (lane < (j + 1) * cg) for j in range(128 // cg)]


def _chunk_flatten(u_scr, z_ref, n_chunks, ngrp, cg):
    gpt = 128 // cg
    masks = _strip_masks(n_chunks, cg)
    for g in range(ngrp):
        k, i = divmod(g, gpt)
        for hf in range(S5_T // gpt):
            acc = None
            for j in range(gpt):
                src = u_scr[k, pl.ds(hf * gpt + j, n_chunks, stride=S5_T), :]
                shift = ((j - i) * cg) % 128
                if shift:
                    src = pltpu.roll(src, shift, axis=1)
                acc = src if acc is None else jnp.where(masks[j], src, acc)
            tile = g * (S5_T // gpt) + hf
            z_ref[0, :, tile * 128:(tile + 1) * 128] = acc.astype(z_ref.dtype)


def _chunk_unflatten(yz_ref, scr, n_chunks, ngrp, cg):
    gpt = 128 // cg
    masks = _strip_masks(n_chunks, cg)
    for k in range(ngrp // gpt):
        for hf in range(S5_T // gpt):
            srcs = []
            for i in range(gpt):
                tile = (k * gpt + i) * (S5_T // gpt) + hf
                srcs.append(yz_ref[0, :, tile * 128:(tile + 1) * 128].astype(F32))
            for j in range(gpt):
                acc = None
                for i in range(gpt):
                    shift = ((i - j) * cg) % 128
                    src = pltpu.roll(srcs[i], shift, axis=1) if shift else srcs[i]
                    acc = src if acc is None else jnp.where(masks[i], src, acc)
                t = hf * gpt + j
                scr[k, t * n_chunks:(t + 1) * n_chunks, :] = acc
    return jnp.concatenate(
        [jnp.concatenate([scr[k, pl.ds(ch, S5_T, stride=n_chunks), :] for k in range(scr.shape[0])], axis=1)
         for ch in range(n_chunks)], axis=0)


def _proj_in_kernel(hl_ref, hc_ref, mod_ref, g_ref, w_ref, hg_ref, z_ref, gate_ref, xn_scr, u_scr,
                    *, nt_lat, ngrp, cg):
    def norm(h_ref):
        xn_scr[...] = _modulated_norm(h_ref[0], g_ref[...], mod_ref[0, 0:1, :], mod_ref[0, 1:2, :]).astype(BF16)

    pl.when(pl.program_id(1) < nt_lat)(lambda: norm(hl_ref))
    pl.when(pl.program_id(1) >= nt_lat)(lambda: norm(hc_ref))
    xn = xn_scr[...]
    hg_w, s5_w = hg_ref.shape[2], ngrp * cg

    def project(o_ref, col0):
        width = o_ref.shape[2]
        step = math.gcd(512, width)
        for c in range(0, width, step):
            o_ref[0, :, c:c + step] = _dot(xn, w_ref[:, col0 + c:col0 + c + step]).astype(o_ref.dtype)

    project(hg_ref, 0)
    project(gate_ref, hg_w + s5_w)
    u = _dot(xn, w_ref[:, hg_w:hg_w + s5_w])
    for k in range(s5_w // 128):
        u_scr[k] = u[:, k * 128:(k + 1) * 128]
    _chunk_flatten(u_scr, z_ref, xn_scr.shape[0] // S5_T, ngrp, cg)


def _proj_in(h_lat, h_ctx, mods, norm_g, w_stack, layer, hg_cols, ngrp, cg):
    bsz, n_lat, d = h_lat.shape
    n_ctx = h_ctx.shape[1]
    total = n_lat + n_ctx
    nt_lat, nt_ctx = n_lat // TOK_TILE, n_ctx // TOK_TILE
    s5_w = ngrp * cg
    gate_cols = w_stack.shape[2] - hg_cols - s5_w
    ct = TOK_TILE // S5_T
    return pl.pallas_call(
        functools.partial(_proj_in_kernel, nt_lat=nt_lat, ngrp=ngrp, cg=cg),
        grid=(bsz, nt_lat + nt_ctx),
        in_specs=[
            pl.BlockSpec((1, TOK_TILE, d), lambda b, t: (b, jnp.minimum(t, nt_lat - 1), 0)),
            pl.BlockSpec((1, TOK_TILE, d), lambda b, t: (b, jnp.maximum(t - nt_lat, 0), 0)),
            pl.BlockSpec((1, 6, d), lambda b, t: (jnp.where(t < nt_lat, b, bsz), 0, 0)),
            pl.BlockSpec((1, d), lambda b, t: (0, 0)),
            _layer_spec(w_stack, layer),
        ],
        out_specs=[pl.BlockSpec((1, TOK_TILE, hg_cols), lambda b, t: (b, t, 0)),
                   pl.BlockSpec((1, ct, s5_w * S5_T), lambda b, t: (b, t, 0)),
                   pl.BlockSpec((1, TOK_TILE, gate_cols), lambda b, t: (b, t, 0))],
        out_shape=[jax.ShapeDtypeStruct((bsz, total, hg_cols), BF16),
                   jax.ShapeDtypeStruct((bsz, total // S5_T, s5_w * S5_T), BF16),
                   jax.ShapeDtypeStruct((bsz, total, gate_cols), BF16)],
        scratch_shapes=[pltpu.VMEM((TOK_TILE, d), BF16), pltpu.VMEM((s5_w // 128, TOK_TILE, 128), F32)],
        compiler_params=pltpu.CompilerParams(
            dimension_semantics=("parallel", "arbitrary"), vmem_limit_bytes=VMEM_LIMIT),
        name="proj_in",
    )(h_lat, h_ctx, mods, norm_g.reshape(1, d), w_stack)


def _gla_tables(c, rev):
    t = np.arange(c)[:, None]
    s = np.arange(c)[None, :]
    x = t ^ s
    lev = np.full((c, c), -2, np.int32)
    causal = (t < s) if rev else (t > s)
    with np.errstate(divide="ignore"):
        hb = np.floor(np.log2(np.maximum(x, 1))).astype(np.int32)
    lev = np.where(causal, hb, lev)
    lev = np.where(t == s, -1, lev).astype(np.int32)
    tri = ((t <= s) if rev else (t >= s)).astype(np.float32)
    return jnp.asarray(lev), jnp.asarray(tri, dtype=BF16)


def _gla_kernel(*refs, rev, readout, c, nheads, dk):
    if readout:
        (q_ref, fz_ref, i_ref, lb_ref, lev_ref, tri_ref, g_ref, of_ref, ng_ref,
         out_ref, st_ref, b_scr_all) = refs
    else:
        q_ref, fz_ref, i_ref, lb_ref, lev_ref, tri_ref, out_ref, st_ref, b_scr_all = refs
    nlev = int(math.log2(c))

    @pl.when(pl.program_id(1) == 0)
    def _():
        st_ref[...] = jnp.zeros_like(st_ref)

    lev = lev_ref[...]
    tri = tri_ref[...]
    row = lax.broadcasted_iota(jnp.int32, (c, 1), 0)

    def head(hd, b_scr):
        sl = pl.ds(pl.multiple_of(hd * dk, dk), dk)
        z = fz_ref[0, :, sl].astype(F32)
        qz = q_ref[0, :, sl].astype(F32)
        v = i_ref[0, :, sl]
        lb = lb_ref[0:1, sl]

        e = jnp.exp(-jnp.abs(z))
        den = 1.0 + e
        inv = 1.0 / den
        pos = z >= 0
        sig_pos = jnp.where(pos, 1.0, e) * inv
        sig_neg = jnp.where(pos, e, 1.0) * inv
        f = lb + (1.0 - lb) * sig_pos
        a = jnp.where(lb > 0, jnp.log(f), jnp.minimum(z, 0.0) - jnp.log(den))
        k = (1.0 - lb) * sig_neg
        q = qz * _sigmoid(qz)

        a_hi = a.astype(BF16)
        a_lo = (a - a_hi.astype(F32)).astype(BF16)
        bb = _dot(tri, jnp.concatenate([a_hi, a_lo], axis=1))
        b = bb[:, :dk] + bb[:, dk:]
        b_scr[...] = b

        odd = (row & 1) == 1
        f_prev = pltpu.roll(f, 1, axis=0)
        f_next = pltpu.roll(f, c - 1, axis=0)
        qf = q * f
        scores = jnp.zeros((c, c), F32)
        for level in range(nlev):
            half = 1 << level
            if level == 0:
                xq, xk = qf, k
            elif level == 1:
                if rev:
                    xq, xk = qf * jnp.where(odd, 1.0, f_next), k * jnp.where(odd, f_prev, 1.0)
                else:
                    xq, xk = qf * jnp.where(odd, f_prev, 1.0), k * jnp.where(odd, 1.0, f_next)
            else:
                blk = 2 * half
                pieces = []
                for j in range(c // blk):
                    brow = b_scr[pl.ds(j * blk + (half if rev else half - 1), 1), :]
                    pieces.append(jnp.broadcast_to(brow, (blk, dk)))
                ref_b = jnp.concatenate(pieces, axis=0) if len(pieces) > 1 else pieces[0]
                decay = jnp.exp(-jnp.abs(b - ref_b))
                xq, xk = q * decay, k * decay
            scores = jnp.where(lev == level, _nt_dot(xq.astype(BF16), xk.astype(BF16)), scores)
        scores = jnp.where(lev == -1, jnp.sum(q * k, axis=-1, keepdims=True), scores)
        o = _dot(scores.astype(BF16), v)

        st = st_ref[hd]
        b_end = b_scr[pl.ds(0 if rev else c - 1, 1), :]
        o = o + _nt_dot((q * jnp.exp(b)).astype(BF16), st.astype(BF16))
        k_out = (k * jnp.exp(b_end - b)).astype(BF16)
        st_ref[hd] = jnp.exp(b_end) * st + _tn_dot(v, k_out)

        if readout:
            o = o + of_ref[0, :, sl].astype(F32)
            o = o * lax.rsqrt(jnp.mean(o * o, axis=-1, keepdims=True) + RMS_EPS) * ng_ref[...]
            gz = g_ref[0, :, sl].astype(F32)
            o = o * (gz * _sigmoid(gz))
        out_ref[0, :, sl] = o.astype(out_ref.dtype)

    per_iter = b_scr_all.shape[0]

    def head_group(i, carry):
        for slot in range(per_iter):
            head(i * per_iter + slot, b_scr_all.at[slot])
        return carry

    lax.fori_loop(0, nheads // per_iter, head_group, 0)


def _gla(p_hg, lb_row, norm_g, o_fwd, n_lat, n_ctx, width, dk, rev):
    bsz, total, _ = p_hg.shape
    c = GLA_C
    nl, nc = n_lat // c, n_ctx // c
    nheads = width // dk
    readout = rev

    if rev:
        def chunk(i):
            return jnp.where(i < nc, nl + nc - 1 - i, nl + nc - 1 - i)
    else:
        def chunk(i):
            return jnp.where(i < nc, nl + i, i - nc)

    def col_spec(colblk):
        return pl.BlockSpec((1, c, width), lambda b, i: (b, chunk(i), colblk))

    lev, tri = _gla_tables(c, rev)
    in_specs = [col_spec(0), col_spec(2 if rev else 1), col_spec(3),
                pl.BlockSpec((1, width), lambda b, i: (0, 0)),
                pl.BlockSpec((c, c), lambda b, i: (0, 0)),
                pl.BlockSpec((c, c), lambda b, i: (0, 0))]
    args = [p_hg, p_hg, p_hg, lb_row.reshape(1, width), lev, tri]
    if readout:
        in_specs += [col_spec(4),
                     pl.BlockSpec((1, c, width), lambda b, i: (b, chunk(i), 0)),
                     pl.BlockSpec((1, dk), lambda b, i: (0, 0))]
        args += [p_hg, o_fwd, norm_g.reshape(1, dk)]
    return pl.pallas_call(
        functools.partial(_gla_kernel, rev=rev, readout=readout, c=c, nheads=nheads, dk=dk),
        grid=(bsz, nl + nc),
        in_specs=in_specs,
        out_specs=pl.BlockSpec((1, c, width), lambda b, i: (b, chunk(i), 0)),
        out_shape=jax.ShapeDtypeStruct((bsz, total, width), BF16),
        scratch_shapes=[pltpu.VMEM((nheads, dk, dk), F32), pltpu.VMEM((math.gcd(HEADS_PER_ITER, nheads), c, dk), F32)],
        compiler_params=pltpu.CompilerParams(
            dimension_semantics=("parallel", "arbitrary"), vmem_limit_bytes=VMEM_LIMIT),
        name="gla_bwd" if rev else "gla_fwd",
    )(*args)


def _s5_tables(a_re, a_im, log_dt, b_re, b_im, c_re, c_im):
    t = S5_T
    ngrp, nst, cg = b_re.shape
    a_re = jnp.minimum(a_re.astype(F32), S5_MAX_RE)
    a_im = a_im.astype(F32)
    dt = jnp.exp(log_dt.astype(F32))[..., None]
    mag = jnp.exp(dt * a_re)
    abr, abi = mag * jnp.cos(dt * a_im), mag * jnp.sin(dt * a_im)
    den = a_re * a_re + a_im * a_im
    nr, ni = abr - 1.0, abi
    f_re = ((nr * a_re + ni * a_im) / den)[..., None]
    f_im = ((ni * a_re - nr * a_im) / den)[..., None]
    b_re, b_im = b_re.astype(F32)[None], b_im.astype(F32)[None]
    bb_re = f_re * b_re - f_im * b_im
    bb_im = f_re * b_im + f_im * b_re
    pr, pi = [jnp.ones_like(abr)], [jnp.zeros_like(abr)]
    for _ in range(t):
        pr.append(pr[-1] * abr - pi[-1] * abi)
        pi.append(pr[-2] * abi + pi[-1] * abr)
    pr, pi = jnp.stack(pr), jnp.stack(pi)
    c_re, c_im = c_re.astype(F32), c_im.astype(F32)
    l_re = c_re[None, None] * pr[:, :, :, None, :] - c_im[None, None] * pi[:, :, :, None, :]
    l_im = c_re[None, None] * pi[:, :, :, None, :] + c_im[None, None] * pr[:, :, :, None, :]
    kern = (jnp.einsum("drgcn,rgnk->drgck", l_re[:t], bb_re, precision=_HI)
            - jnp.einsum("drgcn,rgnk->drgck", l_im[:t], bb_im, precision=_HI))
    lag = np.arange(t)[:, None] - np.arange(t)[None, :]
    sel = np.stack([lag[:, :, None] == np.arange(t), -lag[:, :, None] == np.arange(t)])
    mt = jnp.einsum("rtsd,drgck->gsktc", jnp.asarray(sel, BF16), kern.astype(BF16),
                    preferred_element_type=F32).reshape(ngrp, t * cg, t * cg)
    pf_r, pf_i = pr[t - 1 - np.arange(t), 0], pi[t - 1 - np.arange(t), 0]
    pb_r, pb_i = pr[np.arange(t), 1], pi[np.arange(t), 1]

    def drive(p_r, p_i, r):
        g_r = p_r[..., None] * bb_re[r][None] - p_i[..., None] * bb_im[r][None]
        g_i = p_r[..., None] * bb_im[r][None] + p_i[..., None] * bb_re[r][None]
        to = lambda x: x.transpose(1, 0, 3, 2).reshape(ngrp, t * cg, nst)
        return to(g_r), to(g_i)

    gf_r, gf_i = drive(pf_r, pf_i, 0)
    gb_r, gb_i = drive(pb_r, pb_i, 1)
    def read(idx, r):
        rr = l_re[idx, r].transpose(1, 3, 0, 2).reshape(ngrp, nst, t * cg)
        ri = -l_im[idx, r].transpose(1, 3, 0, 2).reshape(ngrp, nst, t * cg)
        return rr, ri

    of_r, of_i = read(np.arange(t) + 1, 0)
    ob_r, ob_i = read(t - np.arange(t), 1)

    eye = jnp.eye(2, dtype=F32)
    g4 = jnp.stack([gf_r, gf_i, gb_r, gb_i], axis=2).reshape(ngrp // 2, 2, t * cg, 4, 1, nst)
    gin = (g4 * eye[None, :, None, None, :, None]).reshape(ngrp // 2, 2 * t * cg, 8 * nst)
    o4 = jnp.stack([of_r, of_i, ob_r, ob_i], axis=1).reshape(ngrp // 2, 2, 4, nst, 1, t * cg)
    o4 = o4.transpose(0, 2, 1, 3, 4, 5)
    gout = (o4 * eye[None, None, :, None, :, None]).reshape(ngrp // 2, 8 * nst, 2 * t * cg)
    a_chunk = jnp.stack([pr[t, 0], pi[t, 0], pr[t, 1], pi[t, 1]]).reshape(4, ngrp * nst)
    return mt.astype(BF16), gin.astype(BF16), gout.astype(BF16), a_chunk


def _s5_kernel(z_ref, mt_ref, gin_ref, gout_ref, ac_ref, skip_ref, y_ref, d_scr, x_scr,
               *, n_lat, n_ctx, npairs, pw):
    nrows = n_lat + n_ctx
    gw = pw // 2
    tw = pw // 4
    for p in range(npairs):
        d = _dot(z_ref[0, :, p * pw:(p + 1) * pw], gin_ref[p])
        for j in range(4):
            d_scr[j, :, p * tw:(p + 1) * tw] = d[:, j * tw:(j + 1) * tw]

    def scan(plane, forward):
        ar = ac_ref[plane:plane + 1, :]
        ai = ac_ref[plane + 1:plane + 2, :]

        def body(i, carry):
            xr, xi = carry
            if forward:
                n = jnp.where(i < n_ctx, n_lat + i, i - n_ctx)
            else:
                n = nrows - 1 - i
            x_scr[plane, pl.ds(n, 1), :] = xr
            x_scr[plane + 1, pl.ds(n, 1), :] = xi
            dr = d_scr[plane, pl.ds(n, 1), :]
            di = d_scr[plane + 1, pl.ds(n, 1), :]
            return ar * xr - ai * xi + dr, ar * xi + ai * xr + di

        zero = jnp.zeros((1, d_scr.shape[2]), F32)
        lax.fori_loop(0, nrows, body, (zero, zero))

    scan(0, True)
    scan(2, False)

    for p in range(npairs):
        xp = jnp.concatenate([x_scr[j, :, p * tw:(p + 1) * tw] for j in range(4)], axis=1).astype(BF16)
        y = _dot(xp, gout_ref[p])
        for g in range(2):
            lo = p * pw + g * gw
            zg = z_ref[0, :, lo:lo + gw]
            yg = y[:, g * gw:(g + 1) * gw] + _dot(zg, mt_ref[2 * p + g]) + skip_ref[:, lo:lo + gw] * zg.astype(F32)
            y_ref[0, :, lo:lo + gw] = yg.astype(y_ref.dtype)


def _s5(zf, tables, layer, skip_flat, n_lat_chunks, n_ctx_chunks):
    bsz, nrows, wide = zf.shape
    mt, gin, gout, a_chunk = tables
    npairs, pw = gin.shape[1:3]
    planes = a_chunk.shape[2]
    return pl.pallas_call(
        functools.partial(_s5_kernel, n_lat=n_lat_chunks, n_ctx=n_ctx_chunks, npairs=npairs, pw=pw),
        grid=(bsz,),
        in_specs=[pl.BlockSpec((1, nrows, wide), lambda b: (b, 0, 0)),
                  _layer_spec(mt, layer), _layer_spec(gin, layer), _layer_spec(gout, layer),
                  _layer_spec(a_chunk, layer), _const_spec(skip_flat.shape)],
        out_specs=pl.BlockSpec((1, nrows, wide), lambda b: (b, 0, 0)),
        out_shape=jax.ShapeDtypeStruct((bsz, nrows, wide), BF16),
        scratch_shapes=[pltpu.VMEM((4, nrows, planes), F32), pltpu.VMEM((4, nrows, planes), F32)],
        compiler_params=pltpu.CompilerParams(
            dimension_semantics=("parallel",), vmem_limit_bytes=VMEM_LIMIT),
        name="s5",
    )(zf, mt, gin, gout, a_chunk, skip_flat)


def _gelu_tanh(x):
    return 0.5 * x * (1.0 + jnp.tanh(math.sqrt(2.0 / math.pi) * (x + 0.044715 * (x * x * x))))


def _merge_kernel(h_ref, yh_ref, yz_ref, gate_ref, mod_ref, whg_ref, wglu_ref, wbr_ref, wout_ref, o_ref, y_scr,
                  *, ngrp, cg):
    d = h_ref.shape[2]
    y_hg = _dot(yh_ref[0], whg_ref[...])
    ys = _gelu_tanh(_chunk_unflatten(yz_ref, y_scr, yz_ref.shape[1], ngrp, cg))
    glu = _sigmoid(_dot(ys.astype(BF16), wglu_ref[...]))
    y_s5 = _dot((ys * glu).astype(BF16), wbr_ref[...])
    gate = gate_ref[0].astype(F32)
    merged = _sigmoid(gate[:, :d]) * y_hg + _sigmoid(gate[:, d:]) * y_s5
    o_ref[0] = h_ref[0] + mod_ref[0, 2:3, :] * _dot(merged.astype(BF16), wout_ref[...])


def _merge(h, yh, yz, p_gate, mods, mod_row_fn, whg, wglu, wbr, wout, layer, ngrp, cg, tok_off_blocks):
    bsz, length, d = h.shape
    nt = length // TOK_TILE
    ct = TOK_TILE // S5_T
    seq = lambda rows, w: pl.BlockSpec((1, rows, w), lambda b, t: (b, t + tok_off_blocks, 0))
    return pl.pallas_call(
        functools.partial(_merge_kernel, ngrp=ngrp, cg=cg),
        grid=(bsz, nt),
        in_specs=[pl.BlockSpec((1, TOK_TILE, d), lambda b, t: (b, t, 0)),
                  seq(TOK_TILE, yh.shape[2]), seq(ct, yz.shape[2]), seq(TOK_TILE, p_gate.shape[2]),
                  pl.BlockSpec((1, 6, d), lambda b, t: (mod_row_fn(b), 0, 0)),
                  _layer_spec(whg, layer), _layer_spec(wglu, layer), _layer_spec(wbr, layer),
                  _layer_spec(wout, layer)],
        out_specs=pl.BlockSpec((1, TOK_TILE, d), lambda b, t: (b, t, 0)),
        out_shape=jax.ShapeDtypeStruct(h.shape, F32),
        scratch_shapes=[pltpu.VMEM((ngrp * cg // 128, TOK_TILE, 128), F32)],
        input_output_aliases={0: 0},
        compiler_params=pltpu.CompilerParams(
            dimension_semantics=("parallel", "parallel"), vmem_limit_bytes=VMEM_LIMIT),
        name="merge",
    )(h, yh, yz, p_gate, mods, whg, wglu, wbr, wout)


def _ffn_kernel(h_ref, mod_ref, g_ref, wab_ref, cw_ref, cb_ref, wd_ref, fg_ref,
                o_ref, xn_scr, up_scr, act_scr, *, n_tok, width, cw, mrows, final):
    j = pl.program_id(1)
    nj = pl.num_programs(1)
    vertical = n_tok > width
    nblk = n_tok // mrows
    rpb = mrows // width
    pad = width

    @pl.when(j == 0)
    def _():
        def norm_body(i, carry):
            r0 = pl.multiple_of(i * mrows, mrows)
            xn_scr[pl.ds(r0, mrows), :] = _modulated_norm(
                h_ref[0, pl.ds(r0, mrows), :], g_ref[...], mod_ref[0, 3:4, :], mod_ref[0, 4:5, :]).astype(BF16)
            o_ref[0, pl.ds(r0, mrows), :] = jnp.zeros((mrows, o_ref.shape[2]), F32)
            return carry

        lax.fori_loop(0, nblk, norm_body, 0)
        xn_scr[n_tok:n_tok + width, :] = jnp.zeros((width, xn_scr.shape[1]), BF16)
        for copy in range(3):
            up_scr[copy, 0:pad, :] = jnp.zeros((pad, 2 * cw), F32)
            up_scr[copy, pad + n_tok:pad + n_tok + width, :] = jnp.zeros((width, 2 * cw), F32)

    def up_rows(row0, m):
        up = _dot(xn_scr[pl.ds(row0, m), :], wab_ref[...])
        col = lax.broadcasted_iota(jnp.int32, (m, 1), 0) % width
        up_scr[0, pl.ds(pad + row0, m), :] = jnp.where(col == 0, 0.0, pltpu.roll(up, 1, axis=0))
        up_scr[1, pl.ds(pad + row0, m), :] = up
        up_scr[2, pl.ds(pad + row0, m), :] = jnp.where(col == width - 1, 0.0, pltpu.roll(up, m - 1, axis=0))

    def conv_row(row0, lane0):
        acc = cb_ref[:, lane0:lane0 + 128]
        for dr in ((-1, 0, 1) if vertical else (0,)):
            for dc in (-1, 0, 1):
                tap = (dr + 1) * 3 + (dc + 1)
                src = up_scr[dc + 1, pl.ds(pl.multiple_of(pad + row0 + dr * width, 8), width), lane0:lane0 + 128]
                acc = acc + cw_ref[tap:tap + 1, lane0:lane0 + 128] * src
        return acc

    def conv_block(blk):
        r0 = blk * mrows
        for lt in range(cw // 128):
            for r in range(rpb):
                row0 = r0 + r * width
                ca = conv_row(row0, lt * 128)
                cb = conv_row(row0, cw + lt * 128)
                act_scr[pl.ds(pl.multiple_of(row0, 16), width), lt * 128:(lt + 1) * 128] = (
                    ca * _sigmoid(ca) * cb).astype(BF16)

    def down_rows(row0):
        o_ref[0, pl.ds(row0, mrows), :] += _dot(act_scr[pl.ds(row0, mrows), :], wd_ref[...])

    if nblk == 1:
        up_rows(0, mrows)
        conv_block(0)
        down_rows(0)
    else:
        streams = math.gcd(FFN_STREAMS, nblk)
        per = nblk // streams
        up_rows(0, width)
        for s in range(1, streams):
            up_rows(s * per * mrows - width, 2 * width)

        def body(k, carry):
            for s in range(streams):
                r0 = pl.multiple_of((k + s * per) * mrows, mrows)
                up_rows(r0 + width, mrows)
            for s in range(streams):
                conv_block(k + s * per)
            for s in range(streams):
                down_rows(pl.multiple_of((k + s * per) * mrows, mrows))
            return carry

        lax.fori_loop(0, per, body, 0)

    @pl.when(j == nj - 1)
    def _():
        def fin_body(i, carry):
            r0 = pl.multiple_of(i * mrows, mrows)
            out = h_ref[0, pl.ds(r0, mrows), :] + mod_ref[0, 5:6, :] * o_ref[0, pl.ds(r0, mrows), :]
            if final:
                out = out * lax.rsqrt(jnp.mean(out * out, axis=-1, keepdims=True) + RMS_EPS) * fg_ref[...]
            o_ref[0, pl.ds(r0, mrows), :] = out
            return carry

        lax.fori_loop(0, nblk, fin_body, 0)


def _ffn(h, mods, mod_row_fn, norm_g, ffn_w, layer, final_g, width, final):
    bsz, n_tok, d = h.shape
    wab, cwf, cbf, wd = ffn_w
    cw = FFN_TILE
    nj = wd.shape[1] // cw
    return pl.pallas_call(
        functools.partial(_ffn_kernel, n_tok=n_tok, width=width, cw=cw, mrows=TOK_TILE, final=final),
        grid=(bsz, nj),
        in_specs=[pl.BlockSpec((1, n_tok, d), lambda b, j: (b, 0, 0), pipeline_mode=pl.Buffered(1)),
                  pl.BlockSpec((1, 6, d), lambda b, j: (mod_row_fn(b), 0, 0)),
                  pl.BlockSpec((1, d), lambda b, j: (0, 0)),
                  pl.BlockSpec((None, d, 2 * cw), lambda b, j: (layer, 0, j)),
                  pl.BlockSpec((None, 9, 2 * cw), lambda b, j: (layer, 0, j)),
                  pl.BlockSpec((None, 1, 2 * cw), lambda b, j: (layer, 0, j)),
                  pl.BlockSpec((None, cw, d), lambda b, j: (layer, j, 0)),
                  pl.BlockSpec((1, d), lambda b, j: (0, 0))],
        out_specs=pl.BlockSpec((1, n_tok, d), lambda b, j: (b, 0, 0)),
        out_shape=jax.ShapeDtypeStruct(h.shape, F32),
        scratch_shapes=[pltpu.VMEM((n_tok + width, d), BF16),
                        pltpu.VMEM((3, n_tok + 2 * width, 2 * cw), F32),
                        pltpu.VMEM((n_tok, cw), BF16)],
        input_output_aliases={0: 0},
        compiler_params=pltpu.CompilerParams(
            dimension_semantics=("parallel", "arbitrary"), vmem_limit_bytes=VMEM_LIMIT),
        name="ffn",
    )(h, mods, norm_g.reshape(1, d), wab, cwf, cbf, wd, final_g.reshape(1, d))


def _ffn_weights(w_up, conv_w, conv_b, w_down):
    depth, ffn, _ = w_down.shape
    cw = FFN_TILE
    fp = ((ffn + cw - 1) // cw) * cw
    nj = fp // cw

    def tiles(x):
        lead = x.shape[:-1]
        halves = []
        for part in (x[..., :ffn], x[..., ffn:]):
            part = jnp.pad(part, [(0, 0)] * len(lead) + [(0, fp - ffn)])
            halves.append(part.reshape(lead + (nj, cw)))
        return jnp.concatenate(halves, axis=-1).reshape(lead + (nj * 2 * cw,))

    wd = jnp.pad(w_down.astype(BF16), [(0, 0), (0, fp - ffn), (0, 0)])
    return (tiles(w_up.astype(BF16)), tiles(conv_w.reshape(depth, 9, 2 * ffn)),
            tiles(conv_b.reshape(depth, 1, 2 * ffn)), wd)


def _lower_bounds(lb_raw):
    p = jax.nn.softmax(lb_raw.astype(F32), axis=0)
    cs = jnp.cumsum(p, axis=0)
    return cs - cs[0:1]


def kernel(x, c, ctx, c_ctx, ada_w, ada_b, norm1_g, w_in, hg_lb_raw, hg_norm_g, w_hg_br, s5_a_re, s5_a_im, s5_log_dt, s5_b_re, s5_b_im, s5_c_re, s5_c_im, s5_d, w_s5_glu, w_s5_br, w_out, norm2_g, w_up, ffn_conv_w, ffn_conv_b, w_down, final_g):
    bsz, n_lat, d = x.shape
    n_ctx = ctx.shape[1]
    depth = ada_w.shape[0]
    dk = hg_norm_g.shape[1]
    hg_w = hg_lb_raw.shape[2]
    ngrp, nst, cg = s5_b_re.shape[1:]
    assert n_lat % TOK_TILE == 0 and n_ctx % TOK_TILE == 0 and n_lat % GLA_C == 0 and n_ctx % GLA_C == 0
    assert 5 * hg_w + ngrp * cg + 2 * d == w_in.shape[2] and S5_T * cg == 256 and 128 % cg == 0
    assert ngrp % (128 // cg) == 0

    rows = ((bsz + 1 + 7) // 8) * 8
    cond = jnp.zeros((rows, d), F32).at[:bsz].set(c).at[bsz].set(c_ctx)
    mods = _ada_mod(cond, ada_w, ada_b).reshape(depth, rows, 6, d)
    lat_row = lambda b: b
    ctx_row = lambda b: bsz
    lbs = _lower_bounds(hg_lb_raw)
    lat_blk = n_lat // TOK_TILE

    w_in_b, whg_b, wglu_b, wbr_b, wout_b = (w.astype(BF16) for w in (w_in, w_hg_br, w_s5_glu, w_s5_br, w_out))
    ffn_w = _ffn_weights(w_up, ffn_conv_w, ffn_conv_b, w_down)
    s5_tabs = jax.vmap(_s5_tables)(s5_a_re, s5_a_im, s5_log_dt, s5_b_re, s5_b_im, s5_c_re, s5_c_im)
    skip = jnp.broadcast_to(s5_d.astype(F32).reshape(depth, ngrp, 1, cg), (depth, ngrp, S5_T, cg))
    skip = skip.reshape(depth, 1, ngrp * S5_T * cg)

    h_lat, h_ctx = x, ctx
    for l in range(depth):
        last = l == depth - 1
        p_hg, z, p_gate = _proj_in(h_lat, h_ctx, mods[l], norm1_g[l], w_in_b, l, 5 * hg_w, ngrp, cg)

        o_fwd = _gla(p_hg, lbs[l, 0], hg_norm_g[l], None, n_lat, n_ctx, hg_w, dk, rev=False)
        yh = _gla(p_hg, lbs[l, 1], hg_norm_g[l], o_fwd, n_lat, n_ctx, hg_w, dk, rev=True)
        yz = _s5(z, s5_tabs, l, skip[l], n_lat // S5_T, n_ctx // S5_T)

        wts = (whg_b, wglu_b, wbr_b, wout_b, l, ngrp, cg)
        h_lat = _merge(h_lat, yh, yz, p_gate, mods[l], lat_row, *wts, 0)
        h_lat = _ffn(h_lat, mods[l], lat_row, norm2_g[l], ffn_w, l, final_g, GRID_W, last)
        if not last:
            h_ctx = _merge(h_ctx, yh, yz, p_gate, mods[l], ctx_row, *wts, lat_blk)
            h_ctx = _ffn(h_ctx, mods[l], ctx_row, norm2_g[l], ffn_w, l, final_g, n_ctx, False)
    return h_lat
```

```python
import functools
import math

import numpy as np
import jax
import jax.numpy as jnp
from jax import lax
from jax.experimental import pallas as pl
from jax.experimental.pallas import tpu as pltpu

GRID_W = 64
RMS_EPS = 1e-6
S5_MAX_RE = -1e-4
S5_T = 16
GLA_C = 128
HEADS_PER_ITER = 8
TOK_TILE = 256
FFN_TILE = 256
FFN_STREAMS = 4
VMEM_LIMIT = 56 * 1024 * 1024

F32 = jnp.float32
BF16 = jnp.bfloat16
_HI = lax.Precision.HIGHEST


def _nt_dot(a, b):
    return lax.dot_general(a, b, (((1,), (1,)), ((), ())), preferred_element_type=F32)


def _tn_dot(a, b):
    return lax.dot_general(a, b, (((0,), (0,)), ((), ())), preferred_element_type=F32)


def _dot(a, b):
    return jnp.dot(a, b, preferred_element_type=F32)


def _sigmoid(x):
    return 1.0 / (1.0 + jnp.exp(-x))


def _const_spec(shape):
    nd = len(shape)
    return pl.BlockSpec(shape, lambda *_: (0,) * nd, pipeline_mode=pl.Buffered(1))


def _layer_spec(stacked, layer):
    rest = stacked.shape[1:]
    return pl.BlockSpec((None,) + rest, lambda *_: (layer,) + (0,) * len(rest), pipeline_mode=pl.Buffered(1))


def _ada_kernel(cond_ref, w_ref, b_ref, o_ref):
    cnd = cond_ref[...]
    s = cnd * _sigmoid(cnd)
    o_ref[0] = jnp.dot(s, w_ref[0], preferred_element_type=F32, precision=_HI) + b_ref[0]


def _ada_mod(cond, ada_w, ada_b):
    depth, d, n6 = ada_w.shape
    rows = cond.shape[0]
    tn = 512
    return pl.pallas_call(
        _ada_kernel,
        grid=(depth, n6 // tn),
        in_specs=[
            pl.BlockSpec((rows, d), lambda l, j: (0, 0)),
            pl.BlockSpec((1, d, tn), lambda l, j: (l, 0, j)),
            pl.BlockSpec((1, 1, tn), lambda l, j: (l, 0, j)),
        ],
        out_specs=pl.BlockSpec((1, rows, tn), lambda l, j: (l, 0, j)),
        out_shape=jax.ShapeDtypeStruct((depth, rows, n6), F32),
        name="ada_mod",
    )(cond, ada_w, ada_b.reshape(depth, 1, n6))


def _modulated_norm(x, g, shift, scale):
    ms = jnp.mean(x * x, axis=-1, keepdims=True)
    y = x * lax.rsqrt(ms + RMS_EPS) * g
    return y * (1.0 + scale) + shift


def _strip_masks(n_rows, cg):
    lane = lax.broadcasted_iota(jnp.int32, (n_rows, 128), 1)
    return [(lane >= j * cg) & (lane < (j + 1) * cg) for j in range(128 // cg)]


def _chunk_flatten(u_scr, z_ref, n_chunks, ngrp, cg):
    gpt = 128 // cg
    masks = _strip_masks(n_chunks, cg)
    for g in range(ngrp):
        k, i = divmod(g, gpt)
        for hf in range(S5_T // gpt):
            acc = None
            for j in range(gpt):
                src = u_scr[k, pl.ds(hf * gpt + j, n_chunks, stride=S5_T), :]
                shift = ((j - i) * cg) % 128
                if shift:
                    src = pltpu.roll(src, shift, axis=1)
                acc = src if acc is None else jnp.where(masks[j], src, acc)
            tile = g * (S5_T // gpt) + hf
            z_ref[0, :, tile * 128:(tile + 1) * 128] = acc.astype(z_ref.dtype)


def _chunk_unflatten(yz_ref, scr, n_chunks, ngrp, cg):
    gpt = 128 // cg
    masks = _strip_masks(n_chunks, cg)
    for k in range(ngrp // gpt):
        for hf in range(S5_T // gpt):
            srcs = []
            for i in range(gpt):
                tile = (k * gpt + i) * (S5_T // gpt) + hf
                srcs.append(yz_ref[0, :, tile * 128:(tile + 1) * 128].astype(F32))
            for j in range(gpt):
                acc = None
                for i in range(gpt):
                    shift = ((i - j) * cg) % 128
                    src = pltpu.roll(srcs[i], shift, axis=1) if shift else srcs[i]
                    acc = src if acc is None else jnp.where(masks[i], src, acc)
                t = hf * gpt + j
                scr[k, t * n_chunks:(t + 1) * n_chunks, :] = acc
    return jnp.concatenate(
        [jnp.concatenate([scr[k, pl.ds(ch, S5_T, stride=n_chunks), :] for k in range(scr.shape[0])], axis=1)
         for ch in range(n_chunks)], axis=0)


def _proj_in_kernel(hl_ref, hc_ref, mod_ref, g_ref, w_ref, hg_ref, z_ref, gate_ref, xn_scr, u_scr,
                    *, nt_lat, ngrp, cg):
    def norm(h_ref):
        xn_scr[...] = _modulated_norm(h_ref[0], g_ref[...], mod_ref[0, 0:1, :], mod_ref[0, 1:2, :]).astype(BF16)

    pl.when(pl.program_id(1) < nt_lat)(lambda: norm(hl_ref))
    pl.when(pl.program_id(1) >= nt_lat)(lambda: norm(hc_ref))
    xn = xn_scr[...]
    hg_w, s5_w = hg_ref.shape[2], ngrp * cg

    def project(o_ref, col0):
        width = o_ref.shape[2]
        step = math.gcd(512, width)
        for c in range(0, width, step):
            o_ref[0, :, c:c + step] = _dot(xn, w_ref[:, col0 + c:col0 + c + step]).astype(o_ref.dtype)

    u = _dot(xn, w_ref[:, hg_w:hg_w + s5_w])
    for k in range(s5_w // 128):
        u_scr[k] = u[:, k * 128:(k + 1) * 128]
    _chunk_flatten(u_scr, z_ref, xn_scr.shape[0] // S5_T, ngrp, cg)
    project(hg_ref, 0)
    project(gate_ref, hg_w + s5_w)


def _proj_in(h_lat, h_ctx, mods, norm_g, w_stack, layer, hg_cols, ngrp, cg):
    bsz, n_lat, d = h_lat.shape
    n_ctx = h_ctx.shape[1]
    total = n_lat + n_ctx
    nt_lat, nt_ctx = n_lat // TOK_TILE, n_ctx // TOK_TILE
    s5_w = ngrp * cg
    gate_cols = w_stack.shape[2] - hg_cols - s5_w
    ct = TOK_TILE // S5_T
    return pl.pallas_call(
        functools.partial(_proj_in_kernel, nt_lat=nt_lat, ngrp=ngrp, cg=cg),
        grid=(bsz, nt_lat + nt_ctx),
        in_specs=[
            pl.BlockSpec((1, TOK_TILE, d), lambda b, t: (b, jnp.minimum(t, nt_lat - 1), 0)),
            pl.BlockSpec((1, TOK_TILE, d), lambda b, t: (b, jnp.maximum(t - nt_lat, 0), 0)),
            pl.BlockSpec((1, 6, d), lambda b, t: (jnp.where(t < nt_lat, b, bsz), 0, 0)),
            pl.BlockSpec((1, d), lambda b, t: (0, 0)),
            _layer_spec(w_stack, layer),
        ],
        out_specs=[pl.BlockSpec((1, TOK_TILE, hg_cols), lambda b, t: (b, t, 0)),
                   pl.BlockSpec((1, ct, s5_w * S5_T), lambda b, t: (b, t, 0)),
                   pl.BlockSpec((1, TOK_TILE, gate_cols), lambda b, t: (b, t, 0))],
        out_shape=[jax.ShapeDtypeStruct((bsz, total, hg_cols), BF16),
                   jax.ShapeDtypeStruct((bsz, total // S5_T, s5_w * S5_T), BF16),
                   jax.ShapeDtypeStruct((bsz, total, gate_cols), BF16)],
        scratch_shapes=[pltpu.VMEM((TOK_TILE, d), BF16), pltpu.VMEM((s5_w // 128, TOK_TILE, 128), F32)],
        compiler_params=pltpu.CompilerParams(
            dimension_semantics=("parallel", "arbitrary"), vmem_limit_bytes=VMEM_LIMIT),
        name="proj_in",
    )(h_lat, h_ctx, mods, norm_g.reshape(1, d), w_stack)


def _gla_tables(c, rev):
    t = np.arange(c)[:, None]
    s = np.arange(c)[None, :]
    x = t ^ s
    lev = np.full((c, c), -2, np.int32)
    causal = (t < s) if rev else (t > s)
    with np.errstate(divide="ignore"):
        hb = np.floor(np.log2(np.maximum(x, 1))).astype(np.int32)
    lev = np.where(causal, hb, lev)
    lev = np.where(t == s, -1, lev).astype(np.int32)
    tri = ((t <= s) if rev else (t >= s)).astype(np.float32)
    return jnp.asarray(lev), jnp.asarray(tri, dtype=BF16)


def _gla_kernel(*refs, rev, readout, c, nheads, dk):
    if readout:
        (q_ref, fz_ref, i_ref, lb_ref, lev_ref, tri_ref, g_ref, of_ref, ng_ref,
         out_ref, st_ref, b_scr_all) = refs
    else:
        q_ref, fz_ref, i_ref, lb_ref, lev_ref, tri_ref, out_ref, st_ref, b_scr_all = refs
    nlev = int(math.log2(c))

    @pl.when(pl.program_id(1) == 0)
    def _():
        st_ref[...] = jnp.zeros_like(st_ref)

    lev = lev_ref[...]
    tri = tri_ref[...]
    row = lax.broadcasted_iota(jnp.int32, (c, 1), 0)

    def head(hd, b_scr):
        sl = pl.ds(pl.multiple_of(hd * dk, dk), dk)
        z = fz_ref[0, :, sl].astype(F32)
        qz = q_ref[0, :, sl].astype(F32)
        v = i_ref[0, :, sl]
        lb = lb_ref[0:1, sl]

        e = jnp.exp(-jnp.abs(z))
        den = 1.0 + e
        inv = 1.0 / den
        pos = z >= 0
        sig_pos = jnp.where(pos, 1.0, e) * inv
        sig_neg = jnp.where(pos, e, 1.0) * inv
        f = lb + (1.0 - lb) * sig_pos
        a = jnp.where(lb > 0, jnp.log(f), jnp.minimum(z, 0.0) - jnp.log(den))
        k = (1.0 - lb) * sig_neg
        q = qz * _sigmoid(qz)

        a_hi = a.astype(BF16)
        a_lo = (a - a_hi.astype(F32)).astype(BF16)
        bb = _dot(tri, jnp.concatenate([a_hi, a_lo], axis=1))
        b = bb[:, :dk] + bb[:, dk:]
        b_scr[...] = b

        odd = (row & 1) == 1
        f_prev = pltpu.roll(f, 1, axis=0)
        f_next = pltpu.roll(f, c - 1, axis=0)
        qf = q * f
        scores = jnp.zeros((c, c), F32)
        for level in range(nlev):
            half = 1 << level
            if level == 0:
                xq, xk = qf, k
            elif level == 1:
                if rev:
                    xq, xk = qf * jnp.where(odd, 1.0, f_next), k * jnp.where(odd, f_prev, 1.0)
                else:
                    xq, xk = qf * jnp.where(odd, f_prev, 1.0), k * jnp.where(odd, 1.0, f_next)
            else:
                blk = 2 * half
                pieces = []
                for j in range(c // blk):
                    brow = b_scr[pl.ds(j * blk + (half if rev else half - 1), 1), :]
                    pieces.append(jnp.broadcast_to(brow, (blk, dk)))
                ref_b = jnp.concatenate(pieces, axis=0) if len(pieces) > 1 else pieces[0]
                decay = jnp.exp(-jnp.abs(b - ref_b))
                xq, xk = q * decay, k * decay
            scores = jnp.where(lev == level, _nt_dot(xq.astype(BF16), xk.astype(BF16)), scores)
        scores = jnp.where(lev == -1, jnp.sum(q * k, axis=-1, keepdims=True), scores)
        o = _dot(scores.astype(BF16), v)

        st = st_ref[hd]
        b_end = b_scr[pl.ds(0 if rev else c - 1, 1), :]
        o = o + _nt_dot((q * jnp.exp(b)).astype(BF16), st.astype(BF16))
        k_out = (k * jnp.exp(b_end - b)).astype(BF16)
        st_ref[hd] = jnp.exp(b_end) * st + _tn_dot(v, k_out)

        if readout:
            o = o + of_ref[0, :, sl].astype(F32)
            o = o * lax.rsqrt(jnp.mean(o * o, axis=-1, keepdims=True) + RMS_EPS) * ng_ref[...]
            gz = g_ref[0, :, sl].astype(F32)
            o = o * (gz * _sigmoid(gz))
        out_ref[0, :, sl] = o.astype(out_ref.dtype)

    per_iter = b_scr_all.shape[0]

    def head_group(i, carry):
        for slot in range(per_iter):
            head(i * per_iter + slot, b_scr_all.at[slot])
        return carry

    lax.fori_loop(0, nheads // per_iter, head_group, 0)


def _gla(p_hg, lb_row, norm_g, o_fwd, n_lat, n_ctx, width, dk, rev):
    bsz, total, _ = p_hg.shape
    c = GLA_C
    nl, nc = n_lat // c, n_ctx // c
    nheads = width // dk
    readout = rev

    if rev:
        def chunk(i):
            return jnp.where(i < nc, nl + nc - 1 - i, nl + nc - 1 - i)
    else:
        def chunk(i):
            return jnp.where(i < nc, nl + i, i - nc)

    def col_spec(colblk):
        return pl.BlockSpec((1, c, width), lambda b, i: (b, chunk(i), colblk))

    lev, tri = _gla_tables(c, rev)
    in_specs = [col_spec(0), col_spec(2 if rev else 1), col_spec(3),
                pl.BlockSpec((1, width), lambda b, i: (0, 0)),
                pl.BlockSpec((c, c), lambda b, i: (0, 0)),
                pl.BlockSpec((c, c), lambda b, i: (0, 0))]
    args = [p_hg, p_hg, p_hg, lb_row.reshape(1, width), lev, tri]
    if readout:
        in_specs += [col_spec(4),
                     pl.BlockSpec((1, c, width), lambda b, i: (b, chunk(i), 0)),
                     pl.BlockSpec((1, dk), lambda b, i: (0, 0))]
        args += [p_hg, o_fwd, norm_g.reshape(1, dk)]
    return pl.pallas_call(
        functools.partial(_gla_kernel, rev=rev, readout=readout, c=c, nheads=nheads, dk=dk),
        grid=(bsz, nl + nc),
        in_specs=in_specs,
        out_specs=pl.BlockSpec((1, c, width), lambda b, i: (b, chunk(i), 0)),
        out_shape=jax.ShapeDtypeStruct((bsz, total, width), BF16),
        scratch_shapes=[pltpu.VMEM((nheads, dk, dk), F32), pltpu.VMEM((math.gcd(HEADS_PER_ITER, nheads), c, dk), F32)],
        compiler_params=pltpu.CompilerParams(
            dimension_semantics=("parallel", "arbitrary"), vmem_limit_bytes=VMEM_LIMIT),
        name="gla_bwd" if rev else "gla_fwd",
    )(*args)


def _s5_tables(a_re, a_im, log_dt, b_re, b_im, c_re, c_im):
    t = S5_T
    ngrp, nst, cg = b_re.shape
    a_re = jnp.minimum(a_re.astype(F32), S5_MAX_RE)
    a_im = a_im.astype(F32)
    dt = jnp.exp(log_dt.astype(F32))[..., None]
    mag = jnp.exp(dt * a_re)
    abr, abi = mag * jnp.cos(dt * a_im), mag * jnp.sin(dt * a_im)
    den = a_re * a_re + a_im * a_im
    nr, ni = abr - 1.0, abi
    f_re = ((nr * a_re + ni * a_im) / den)[..., None]
    f_im = ((ni * a_re - nr * a_im) / den)[..., None]
    b_re, b_im = b_re.astype(F32)[None], b_im.astype(F32)[None]
    bb_re = f_re * b_re - f_im * b_im
    bb_im = f_re * b_im + f_im * b_re
    pr, pi = [jnp.ones_like(abr)], [jnp.zeros_like(abr)]
    for _ in range(t):
        pr.append(pr[-1] * abr - pi[-1] * abi)
        pi.append(pr[-2] * abi + pi[-1] * abr)
    pr, pi = jnp.stack(pr), jnp.stack(pi)
    c_re, c_im = c_re.astype(F32), c_im.astype(F32)
    l_re = c_re[None, None] * pr[:, :, :, None, :] - c_im[None, None] * pi[:, :, :, None, :]
    l_im = c_re[None, None] * pi[:, :, :, None, :] + c_im[None, None] * pr[:, :, :, None, :]
    kern = (jnp.einsum("drgcn,rgnk->drgck", l_re[:t], bb_re, precision=_HI)
            - jnp.einsum("drgcn,rgnk->drgck", l_im[:t], bb_im, precision=_HI))
    lag = np.arange(t)[:, None] - np.arange(t)[None, :]
    sel = np.stack([lag[:, :, None] == np.arange(t), -lag[:, :, None] == np.arange(t)])
    mt = jnp.einsum("rtsd,drgck->gsktc", jnp.asarray(sel, BF16), kern.astype(BF16),
                    preferred_element_type=F32).reshape(ngrp, t * cg, t * cg)
    pf_r, pf_i = pr[t - 1 - np.arange(t), 0], pi[t - 1 - np.arange(t), 0]
    pb_r, pb_i = pr[np.arange(t), 1], pi[np.arange(t), 1]

    def drive(p_r, p_i, r):
        g_r = p_r[..., None] * bb_re[r][None] - p_i[..., None] * bb_im[r][None]
        g_i = p_r[..., None] * bb_im[r][None] + p_i[..., None] * bb_re[r][None]
        to = lambda x: x.transpose(1, 0, 3, 2).reshape(ngrp, t * cg, nst)
        return to(g_r), to(g_i)

    gf_r, gf_i = drive(pf_r, pf_i, 0)
    gb_r, gb_i = drive(pb_r, pb_i, 1)
    def read(idx, r):
        rr = l_re[idx, r].transpose(1, 3, 0, 2).reshape(ngrp, nst, t * cg)
        ri = -l_im[idx, r].transpose(1, 3, 0, 2).reshape(ngrp, nst, t * cg)
        return rr, ri

    of_r, of_i = read(np.arange(t) + 1, 0)
    ob_r, ob_i = read(t - np.arange(t), 1)

    eye = jnp.eye(2, dtype=F32)
    g4 = jnp.stack([gf_r, gf_i, gb_r, gb_i], axis=2).reshape(ngrp // 2, 2, t * cg, 4, 1, nst)
    gin = (g4 * eye[None, :, None, None, :, None]).reshape(ngrp // 2, 2 * t * cg, 8 * nst)
    o4 = jnp.stack([of_r, of_i, ob_r, ob_i], axis=1).reshape(ngrp // 2, 2, 4, nst, 1, t * cg)
    o4 = o4.transpose(0, 2, 1, 3, 4, 5)
    gout = (o4 * eye[None, None, :, None, :, None]).reshape(ngrp // 2, 8 * nst, 2 * t * cg)
    a_chunk = jnp.stack([pr[t, 0], pi[t, 0], pr[t, 1], pi[t, 1]]).reshape(4, ngrp * nst)
    return mt.astype(BF16), gin.astype(BF16), gout.astype(BF16), a_chunk


def _s5_kernel(z_ref, mt_ref, gin_ref, gout_ref, ac_ref, skip_ref, y_ref, d_scr, x_scr,
               *, n_lat, n_ctx, npairs, pw):
    nrows = n_lat + n_ctx
    gw = pw // 2
    tw = pw // 4
    for p in range(npairs):
        d = _dot(z_ref[0, :, p * pw:(p + 1) * pw], gin_ref[p])
        for j in range(4):
            d_scr[j, :, p * tw:(p + 1) * tw] = d[:, j * tw:(j + 1) * tw]

    def scan(plane, forward):
        ar = ac_ref[plane:plane + 1, :]
        ai = ac_ref[plane + 1:plane + 2, :]

        def body(i, carry):
            xr, xi = carry
            if forward:
                n = jnp.where(i < n_ctx, n_lat + i, i - n_ctx)
            else:
                n = nrows - 1 - i
            x_scr[plane, pl.ds(n, 1), :] = xr
            x_scr[plane + 1, pl.ds(n, 1), :] = xi
            dr = d_scr[plane, pl.ds(n, 1), :]
            di = d_scr[plane + 1, pl.ds(n, 1), :]
            return ar * xr - ai * xi + dr, ar * xi + ai * xr + di

        zero = jnp.zeros((1, d_scr.shape[2]), F32)
        lax.fori_loop(0, nrows, body, (zero, zero))

    scan(0, True)
    scan(2, False)

    for p in range(npairs):
        xp = jnp.concatenate([x_scr[j, :, p * tw:(p + 1) * tw] for j in range(4)], axis=1).astype(BF16)
        y = _dot(xp, gout_ref[p])
        for g in range(2):
            lo = p * pw + g * gw
            zg = z_ref[0, :, lo:lo + gw]
            yg = y[:, g * gw:(g + 1) * gw] + _dot(zg, mt_ref[2 * p + g]) + skip_ref[:, lo:lo + gw] * zg.astype(F32)
            y_ref[0, :, lo:lo + gw] = yg.astype(y_ref.dtype)


def _s5(zf, tables, layer, skip_flat, n_lat_chunks, n_ctx_chunks):
    bsz, nrows, wide = zf.shape
    mt, gin, gout, a_chunk = tables
    npairs, pw = gin.shape[1:3]
    planes = a_chunk.shape[2]
    return pl.pallas_call(
        functools.partial(_s5_kernel, n_lat=n_lat_chunks, n_ctx=n_ctx_chunks, npairs=npairs, pw=pw),
        grid=(bsz,),
        in_specs=[pl.BlockSpec((1, nrows, wide), lambda b: (b, 0, 0)),
                  _layer_spec(mt, layer), _layer_spec(gin, layer), _layer_spec(gout, layer),
                  _layer_spec(a_chunk, layer), _const_spec(skip_flat.shape)],
        out_specs=pl.BlockSpec((1, nrows, wide), lambda b: (b, 0, 0)),
        out_shape=jax.ShapeDtypeStruct((bsz, nrows, wide), BF16),
        scratch_shapes=[pltpu.VMEM((4, nrows, planes), F32), pltpu.VMEM((4, nrows, planes), F32)],
        compiler_params=pltpu.CompilerParams(
            dimension_semantics=("parallel",), vmem_limit_bytes=VMEM_LIMIT),
        name="s5",
    )(zf, mt, gin, gout, a_chunk, skip_flat)


def _gelu_tanh(x):
    return 0.5 * x * (1.0 + jnp.tanh(math.sqrt(2.0 / math.pi) * (x + 0.044715 * (x * x * x))))


def _merge_kernel(h_ref, yh_ref, yz_ref, gate_ref, mod_ref, whg_ref, wglu_ref, wbr_ref, wout_ref, o_ref, y_scr,
                  *, ngrp, cg):
    d = h_ref.shape[2]
    y_hg = _dot(yh_ref[0], whg_ref[...])
    ys = _gelu_tanh(_chunk_unflatten(yz_ref, y_scr, yz_ref.shape[1], ngrp, cg))
    glu = _sigmoid(_dot(ys.astype(BF16), wglu_ref[...]))
    y_s5 = _dot((ys * glu).astype(BF16), wbr_ref[...])
    gate = gate_ref[0].astype(F32)
    merged = _sigmoid(gate[:, :d]) * y_hg + _sigmoid(gate[:, d:]) * y_s5
    o_ref[0] = h_ref[0] + mod_ref[0, 2:3, :] * _dot(merged.astype(BF16), wout_ref[...])


def _merge(h, yh, yz, p_gate, mods, mod_row_fn, whg, wglu, wbr, wout, layer, ngrp, cg, tok_off_blocks):
    bsz, length, d = h.shape
    nt = length // TOK_TILE
    ct = TOK_TILE // S5_T
    seq = lambda rows, w: pl.BlockSpec((1, rows, w), lambda b, t: (b, t + tok_off_blocks, 0))
    return pl.pallas_call(
        functools.partial(_merge_kernel, ngrp=ngrp, cg=cg),
        grid=(bsz, nt),
        in_specs=[pl.BlockSpec((1, TOK_TILE, d), lambda b, t: (b, t, 0)),
                  seq(TOK_TILE, yh.shape[2]), seq(ct, yz.shape[2]), seq(TOK_TILE, p_gate.shape[2]),
                  pl.BlockSpec((1, 6, d), lambda b, t: (mod_row_fn(b), 0, 0)),
                  _layer_spec(whg, layer), _layer_spec(wglu, layer), _layer_spec(wbr, layer),
                  _layer_spec(wout, layer)],
        out_specs=pl.BlockSpec((1, TOK_TILE, d), lambda b, t: (b, t, 0)),
        out_shape=jax.ShapeDtypeStruct(h.shape, F32),
        scratch_shapes=[pltpu.VMEM((ngrp * cg // 128, TOK_TILE, 128), F32)],
        input_output_aliases={0: 0},
        compiler_params=pltpu.CompilerParams(
            dimension_semantics=("parallel", "parallel"), vmem_limit_bytes=VMEM_LIMIT),
        name="merge",
    )(h, yh, yz, p_gate, mods, whg, wglu, wbr, wout)


def _ffn_kernel(h_ref, mod_ref, g_ref, wab_ref, cw_ref, cb_ref, wd_ref, fg_ref,
                o_ref, xn_scr, up_scr, act_scr, *, n_tok, width, cw, mrows, final):
    j = pl.program_id(1)
    nj = pl.num_programs(1)
    vertical = n_tok > width
    nblk = n_tok // mrows
    rpb = mrows // width
    pad = width

    @pl.when(j == 0)
    def _():
        def norm_body(i, carry):
            r0 = pl.multiple_of(i * mrows, mrows)
            xn_scr[pl.ds(r0, mrows), :] = _modulated_norm(
                h_ref[0, pl.ds(r0, mrows), :], g_ref[...], mod_ref[0, 3:4, :], mod_ref[0, 4:5, :]).astype(BF16)
            o_ref[0, pl.ds(r0, mrows), :] = jnp.zeros((mrows, o_ref.shape[2]), F32)
            return carry

        lax.fori_loop(0, nblk, norm_body, 0)
        xn_scr[n_tok:n_tok + width, :] = jnp.zeros((width, xn_scr.shape[1]), BF16)
        for copy in range(3):
            up_scr[copy, 0:pad, :] = jnp.zeros((pad, 2 * cw), F32)
            up_scr[copy, pad + n_tok:pad + n_tok + width, :] = jnp.zeros((width, 2 * cw), F32)

    def up_rows(row0, m):
        up = _dot(xn_scr[pl.ds(row0, m), :], wab_ref[...])
        col = lax.broadcasted_iota(jnp.int32, (m, 1), 0) % width
        up_scr[0, pl.ds(pad + row0, m), :] = jnp.where(col == 0, 0.0, pltpu.roll(up, 1, axis=0))
        up_scr[1, pl.ds(pad + row0, m), :] = up
        up_scr[2, pl.ds(pad + row0, m), :] = jnp.where(col == width - 1, 0.0, pltpu.roll(up, m - 1, axis=0))

    def conv_row(row0, lane0):
        acc = cb_ref[:, lane0:lane0 + 128]
        for dr in ((-1, 0, 1) if vertical else (0,)):
            for dc in (-1, 0, 1):
                tap = (dr + 1) * 3 + (dc + 1)
                src = up_scr[dc + 1, pl.ds(pl.multiple_of(pad + row0 + dr * width, 8), width), lane0:lane0 + 128]
                acc = acc + cw_ref[tap:tap + 1, lane0:lane0 + 128] * src
        return acc

    def conv_block(blk):
        r0 = blk * mrows
        for lt in range(cw // 128):
            for r in range(rpb):
                row0 = r0 + r * width
                ca = conv_row(row0, lt * 128)
                cb = conv_row(row0, cw + lt * 128)
                act_scr[pl.ds(pl.multiple_of(row0, 16), width), lt * 128:(lt + 1) * 128] = (
                    ca * _sigmoid(ca) * cb).astype(BF16)

    def down_rows(row0):
        o_ref[0, pl.ds(row0, mrows), :] += _dot(act_scr[pl.ds(row0, mrows), :], wd_ref[...])

    if nblk == 1:
        up_rows(0, mrows)
        conv_block(0)
        down_rows(0)
    else:
        streams = math.gcd(FFN_STREAMS, nblk)
        per = nblk // streams
        up_rows(0, width)
        for s in range(1, streams):
            up_rows(s * per * mrows - width, 2 * width)

        def body(k, carry):
            for s in range(streams):
                r0 = pl.multiple_of((k + s * per) * mrows, mrows)
                up_rows(r0 + width, mrows)
            for s in range(streams):
                conv_block(k + s * per)
            for s in range(streams):
                down_rows(pl.multiple_of((k + s * per) * mrows, mrows))
            return carry

        lax.fori_loop(0, per, body, 0)

    @pl.when(j == nj - 1)
    def _():
        def fin_body(i, carry):
            r0 = pl.multiple_of(i * mrows, mrows)
            out = h_ref[0, pl.ds(r0, mrows), :] + mod_ref[0, 5:6, :] * o_ref[0, pl.ds(r0, mrows), :]
            if final:
                out = out * lax.rsqrt(jnp.mean(out * out, axis=-1, keepdims=True) + RMS_EPS) * fg_ref[...]
            o_ref[0, pl.ds(r0, mrows), :] = out
            return carry

        lax.fori_loop(0, nblk, fin_body, 0)


def _ffn(h, mods, mod_row_fn, norm_g, ffn_w, layer, final_g, width, final):
    bsz, n_tok, d = h.shape
    wab, cwf, cbf, wd = ffn_w
    cw = FFN_TILE
    nj = wd.shape[1] // cw
    return pl.pallas_call(
        functools.partial(_ffn_kernel, n_tok=n_tok, width=width, cw=cw, mrows=TOK_TILE, final=final),
        grid=(bsz, nj),
        in_specs=[pl.BlockSpec((1, n_tok, d), lambda b, j: (b, 0, 0), pipeline_mode=pl.Buffered(1)),
                  pl.BlockSpec((1, 6, d), lambda b, j: (mod_row_fn(b), 0, 0)),
                  pl.BlockSpec((1, d), lambda b, j: (0, 0)),
                  pl.BlockSpec((None, d, 2 * cw), lambda b, j: (layer, 0, j)),
                  pl.BlockSpec((None, 9, 2 * cw), lambda b, j: (layer, 0, j)),
                  pl.BlockSpec((None, 1, 2 * cw), lambda b, j: (layer, 0, j)),
                  pl.BlockSpec((None, cw, d), lambda b, j: (layer, j, 0)),
                  pl.BlockSpec((1, d), lambda b, j: (0, 0))],
        out_specs=pl.BlockSpec((1, n_tok, d), lambda b, j: (b, 0, 0)),
        out_shape=jax.ShapeDtypeStruct(h.shape, F32),
        scratch_shapes=[pltpu.VMEM((n_tok + width, d), BF16),
                        pltpu.VMEM((3, n_tok + 2 * width, 2 * cw), F32),
                        pltpu.VMEM((n_tok, cw), BF16)],
        input_output_aliases={0: 0},
        compiler_params=pltpu.CompilerParams(
            dimension_semantics=("parallel", "arbitrary"), vmem_limit_bytes=VMEM_LIMIT),
        name="ffn",
    )(h, mods, norm_g.reshape(1, d), wab, cwf, cbf, wd, final_g.reshape(1, d))


def _ffn_weights(w_up, conv_w, conv_b, w_down):
    depth, ffn, _ = w_down.shape
    cw = FFN_TILE
    fp = ((ffn + cw - 1) // cw) * cw
    nj = fp // cw

    def tiles(x):
        lead = x.shape[:-1]
        halves = []
        for part in (x[..., :ffn], x[..., ffn:]):
            part = jnp.pad(part, [(0, 0)] * len(lead) + [(0, fp - ffn)])
            halves.append(part.reshape(lead + (nj, cw)))
        return jnp.concatenate(halves, axis=-1).reshape(lead + (nj * 2 * cw,))

    wd = jnp.pad(w_down.astype(BF16), [(0, 0), (0, fp - ffn), (0, 0)])
    return (tiles(w_up.astype(BF16)), tiles(conv_w.reshape(depth, 9, 2 * ffn)),
            tiles(conv_b.reshape(depth, 1, 2 * ffn)), wd)


def _lower_bounds(lb_raw):
    p = jax.nn.softmax(lb_raw.astype(F32), axis=0)
    cs = jnp.cumsum(p, axis=0)
    return cs - cs[0:1]


def kernel(x, c, ctx, c_ctx, ada_w, ada_b, norm1_g, w_in, hg_lb_raw, hg_norm_g, w_hg_br, s5_a_re, s5_a_im, s5_log_dt, s5_b_re, s5_b_im, s5_c_re, s5_c_im, s5_d, w_s5_glu, w_s5_br, w_out, norm2_g, w_up, ffn_conv_w, ffn_conv_b, w_down, final_g):
    bsz, n_lat, d = x.shape
    n_ctx = ctx.shape[1]
    depth = ada_w.shape[0]
    dk = hg_norm_g.shape[1]
    hg_w = hg_lb_raw.shape[2]
    ngrp, nst, cg = s5_b_re.shape[1:]
    assert n_lat % TOK_TILE == 0 and n_ctx % TOK_TILE == 0 and n_lat % GLA_C == 0 and n_ctx % GLA_C == 0
    assert 5 * hg_w + ngrp * cg + 2 * d == w_in.shape[2] and S5_T * cg == 256 and 128 % cg == 0
    assert ngrp % (128 // cg) == 0

    rows = ((bsz + 1 + 7) // 8) * 8
    cond = jnp.zeros((rows, d), F32).at[:bsz].set(c).at[bsz].set(c_ctx)
    mods = _ada_mod(cond, ada_w, ada_b).reshape(depth, rows, 6, d)
    lat_row = lambda b: b
    ctx_row = lambda b: bsz
    lbs = _lower_bounds(hg_lb_raw)
    lat_blk = n_lat // TOK_TILE

    w_in_b, whg_b, wglu_b, wbr_b, wout_b = (w.astype(BF16) for w in (w_in, w_hg_br, w_s5_glu, w_s5_br, w_out))
    ffn_w = _ffn_weights(w_up, ffn_conv_w, ffn_conv_b, w_down)
    s5_tabs = jax.vmap(_s5_tables)(s5_a_re, s5_a_im, s5_log_dt, s5_b_re, s5_b_im, s5_c_re, s5_c_im)
    skip = jnp.broadcast_to(s5_d.astype(F32).reshape(depth, ngrp, 1, cg), (depth, ngrp, S5_T, cg))
    skip = skip.reshape(depth, 1, ngrp * S5_T * cg)

    h_lat, h_ctx = x, ctx
    for l in range(depth):
        last = l == depth - 1
        p_hg, z, p_gate = _proj_in(h_lat, h_ctx, mods[l], norm1_g[l], w_in_b, l, 5 * hg_w, ngrp, cg)

        o_fwd = _gla(p_hg, lbs[l, 0], hg_norm_g[l], None, n_lat, n_ctx, hg_w, dk, rev=False)
        yh = _gla(p_hg, lbs[l, 1], hg_norm_g[l], o_fwd, n_lat, n_ctx, hg_w, dk, rev=True)
        yz = _s5(z, s5_tabs, l, skip[l], n_lat // S5_T, n_ctx // S5_T)

        wts = (whg_b, wglu_b, wbr_b, wout_b, l, ngrp, cg)
        h_lat = _merge(h_lat, yh, yz, p_gate, mods[l], lat_row, *wts, 0)
        h_lat = _ffn(h_lat, mods[l], lat_row, norm2_g[l], ffn_w, l, final_g, GRID_W, last)
        if not last:
            h_ctx = _merge(h_ctx, yh, yz, p_gate, mods[l], ctx_row, *wts, lat_blk)
            h_ctx = _ffn(h_ctx, mods[l], ctx_row, norm2_g[l], ffn_w, l, final_g, n_ctx, False)
    return h_lat
```

```python
import functools
import math

import numpy as np
import jax
import jax.numpy as jnp
from jax import lax
from jax.experimental import pallas as pl
from jax.experimental.pallas import tpu as pltpu

GRID_W = 64
RMS_EPS = 1e-6
S5_MAX_RE = -1e-4
S5_T = 16
GLA_C = 128
HEADS_PER_ITER = 8
TOK_TILE = 256
FFN_TILE = 256
FFN_STREAMS = 4
VMEM_LIMIT = 56 * 1024 * 1024

F32 = jnp.float32
BF16 = jnp.bfloat16
_HI = lax.Precision.HIGHEST


def _nt_dot(a, b):
    return lax.dot_general(a, b, (((1,), (1,)), ((), ())), preferred_element_type=F32)


def _tn_dot(a, b):
    return lax.dot_general(a, b, (((0,), (0,)), ((), ())), preferred_element_type=F32)


def _dot(a, b):
    return jnp.dot(a, b, preferred_element_type=F32)


def _sigmoid(x):
    return 1.0 / (1.0 + jnp.exp(-x))


def _const_spec(shape):
    nd = len(shape)
    return pl.BlockSpec(shape, lambda *_: (0,) * nd, pipeline_mode=pl.Buffered(1))


def _layer_spec(stacked, layer):
    rest = stacked.shape[1:]
    return pl.BlockSpec((None,) + rest, lambda *_: (layer,) + (0,) * len(rest), pipeline_mode=pl.Buffered(1))


def _ada_kernel(cond_ref, w_ref, b_ref, o_ref):
    cnd = cond_ref[...]
    s = cnd * _sigmoid(cnd)
    o_ref[0] = jnp.dot(s, w_ref[0], preferred_element_type=F32, precision=_HI) + b_ref[0]


def _ada_mod(cond, ada_w, ada_b):
    depth, d, n6 = ada_w.shape
    rows = cond.shape[0]
    tn = 512
    return pl.pallas_call(
        _ada_kernel,
        grid=(depth, n6 // tn),
        in_specs=[
            pl.BlockSpec((rows, d), lambda l, j: (0, 0)),
            pl.BlockSpec((1, d, tn), lambda l, j: (l, 0, j)),
            pl.BlockSpec((1, 1, tn), lambda l, j: (l, 0, j)),
        ],
        out_specs=pl.BlockSpec((1, rows, tn), lambda l, j: (l, 0, j)),
        out_shape=jax.ShapeDtypeStruct((depth, rows, n6), F32),
        name="ada_mod",
    )(cond, ada_w, ada_b.reshape(depth, 1, n6))


def _modulated_norm(x, g, shift, scale):
    ms = jnp.mean(x * x, axis=-1, keepdims=True)
    y = x * lax.rsqrt(ms + RMS_EPS) * g
    return y * (1.0 + scale) + shift


def _strip_masks(n_rows, cg):
    lane = lax.broadcasted_iota(jnp.int32, (n_rows, 128), 1)
    return [(lane >= j * cg) & (lane < (j + 1) * cg) for j in range(128 // cg)]


def _chunk_flatten(u_scr, z_ref, n_chunks, ngrp, cg):
    gpt = 128 // cg
    masks = _strip_masks(n_chunks, cg)
    for g in range(ngrp):
        k, i = divmod(g, gpt)
        for hf in range(S5_T // gpt):
            acc = None
            for j in range(gpt):
                src = u_scr[k, pl.ds(hf * gpt + j, n_chunks, stride=S5_T), :]
                shift = ((j - i) * cg) % 128
                if shift:
                    src = pltpu.roll(src, shift, axis=1)
                acc = src if acc is None else jnp.where(masks[j], src, acc)
            tile = g * (S5_T // gpt) + hf
            z_ref[0, :, tile * 128:(tile + 1) * 128] = acc.astype(z_ref.dtype)


def _chunk_unflatten(yz_ref, scr, n_chunks, ngrp, cg):
    gpt = 128 // cg
    masks = _strip_masks(n_chunks, cg)
    for k in range(ngrp // gpt):
        for hf in range(S5_T // gpt):
            srcs = []
            for i in range(gpt):
                tile = (k * gpt + i) * (S5_T // gpt) + hf
                srcs.append(yz_ref[0, :, tile * 128:(tile + 1) * 128].astype(F32))
            for j in range(gpt):
                acc = None
                for i in range(gpt):
                    shift = ((i - j) * cg) % 128
                    src = pltpu.roll(srcs[i], shift, axis=1) if shift else srcs[i]
                    acc = src if acc is None else jnp.where(masks[i], src, acc)
                t = hf * gpt + j
                scr[k, t * n_chunks:(t + 1) * n_chunks, :] = acc
    return jnp.concatenate(
        [jnp.concatenate([scr[k, pl.ds(ch, S5_T, stride=n_chunks), :] for k in range(scr.shape[0])], axis=1)
         for ch in range(n_chunks)], axis=0)


def _proj_in_kernel(hl_ref, hc_ref, mod_ref, g_ref, w_ref, hg_ref, z_ref, gate_ref, xn_scr, u_scr,
                    *, nt_lat, ngrp, cg):
    def norm(h_ref):
        xn_scr[...] = _modulated_norm(h_ref[0], g_ref[...], mod_ref[0, 0:1, :], mod_ref[0, 1:2, :]).astype(BF16)

    pl.when(pl.program_id(1) < nt_lat)(lambda: norm(hl_ref))
    pl.when(pl.program_id(1) >= nt_lat)(lambda: norm(hc_ref))
    xn = xn_scr[...]
    hg_w, s5_w = hg_ref.shape[2], ngrp * cg

    def project(o_ref, col0):
        width = o_ref.shape[2]
        step = math.gcd(512, width)
        for c in range(0, width, step):
            o_ref[0, :, c:c + step] = _dot(xn, w_ref[:, col0 + c:col0 + c + step]).astype(o_ref.dtype)

    u = _dot(xn, w_ref[:, hg_w:hg_w + s5_w])
    for k in range(s5_w // 128):
        u_scr[k] = u[:, k * 128:(k + 1) * 128]
    _chunk_flatten(u_scr, z_ref, xn_scr.shape[0] // S5_T, ngrp, cg)
    project(hg_ref, 0)
    project(gate_ref, hg_w + s5_w)


def _proj_in(h_lat, h_ctx, mods, norm_g, w_stack, layer, hg_cols, ngrp, cg):
    bsz, n_lat, d = h_lat.shape
    n_ctx = h_ctx.shape[1]
    total = n_lat + n_ctx
    nt_lat, nt_ctx = n_lat // TOK_TILE, n_ctx // TOK_TILE
    s5_w = ngrp * cg
    gate_cols = w_stack.shape[2] - hg_cols - s5_w
    ct = TOK_TILE // S5_T
    return pl.pallas_call(
        functools.partial(_proj_in_kernel, nt_lat=nt_lat, ngrp=ngrp, cg=cg),
        grid=(bsz, nt_lat + nt_ctx),
        in_specs=[
            pl.BlockSpec((1, TOK_TILE, d), lambda b, t: (b, jnp.minimum(t, nt_lat - 1), 0)),
            pl.BlockSpec((1, TOK_TILE, d), lambda b, t: (b, jnp.maximum(t - nt_lat, 0), 0)),
            pl.BlockSpec((1, 6, d), lambda b, t: (jnp.where(t < nt_lat, b, bsz), 0, 0)),
            pl.BlockSpec((1, d), lambda b, t: (0, 0)),
            _layer_spec(w_stack, layer),
        ],
        out_specs=[pl.BlockSpec((1, TOK_TILE, hg_cols), lambda b, t: (b, t, 0)),
                   pl.BlockSpec((1, ct, s5_w * S5_T), lambda b, t: (b, t, 0)),
                   pl.BlockSpec((1, TOK_TILE, gate_cols), lambda b, t: (b, t, 0))],
        out_shape=[jax.ShapeDtypeStruct((bsz, total, hg_cols), BF16),
                   jax.ShapeDtypeStruct((bsz, total // S5_T, s5_w * S5_T), BF16),
                   jax.ShapeDtypeStruct((bsz, total, gate_cols), BF16)],
        scratch_shapes=[pltpu.VMEM((TOK_TILE, d), BF16), pltpu.VMEM((s5_w // 128, TOK_TILE, 128), F32)],
        compiler_params=pltpu.CompilerParams(
            dimension_semantics=("parallel", "arbitrary"), vmem_limit_bytes=VMEM_LIMIT),
        name="proj_in",
    )(h_lat, h_ctx, mods, norm_g.reshape(1, d), w_stack)


def _gla_tables(c, rev):
    t = np.arange(c)[:, None]
    s = np.arange(c)[None, :]
    x = t ^ s
    lev = np.full((c, c), -2, np.int32)
    causal = (t < s) if rev else (t > s)
    with np.errstate(divide="ignore"):
        hb = np.floor(np.log2(np.maximum(x, 1))).astype(np.int32)
    lev = np.where(causal, hb, lev)
    lev = np.where(t == s, -1, lev).astype(np.int32)
    tri = ((t <= s) if rev else (t >= s)).astype(np.float32)
    return jnp.asarray(lev), jnp.asarray(tri, dtype=BF16)


def _gla_kernel(*refs, rev, readout, c, nheads, dk, state_only_steps):
    if readout:
        (q_ref, fz_ref, i_ref, lb_ref, lev_ref, tri_ref, g_ref, of_ref, ng_ref,
         out_ref, st_ref, b_scr_all) = refs
    else:
        q_ref, fz_ref, i_ref, lb_ref, lev_ref, tri_ref, out_ref, st_ref, b_scr_all = refs
    nlev = int(math.log2(c))

    @pl.when(pl.program_id(1) == 0)
    def _():
        st_ref[...] = jnp.zeros_like(st_ref)

    lev = lev_ref[...]
    tri = tri_ref[...]
    row = lax.broadcasted_iota(jnp.int32, (c, 1), 0)

    def head(hd, b_scr, full):
        sl = pl.ds(pl.multiple_of(hd * dk, dk), dk)
        z = fz_ref[0, :, sl].astype(F32)
        v = i_ref[0, :, sl]
        lb = lb_ref[0:1, sl]

        e = jnp.exp(-jnp.abs(z))
        den = 1.0 + e
        inv = 1.0 / den
        pos = z >= 0
        sig_pos = jnp.where(pos, 1.0, e) * inv
        sig_neg = jnp.where(pos, e, 1.0) * inv
        f = lb + (1.0 - lb) * sig_pos
        a = jnp.where(lb > 0, jnp.log(f), jnp.minimum(z, 0.0) - jnp.log(den))
        k = (1.0 - lb) * sig_neg

        a_hi = a.astype(BF16)
        a_lo = (a - a_hi.astype(F32)).astype(BF16)
        bb = _dot(tri, jnp.concatenate([a_hi, a_lo], axis=1))
        b = bb[:, :dk] + bb[:, dk:]
        b_scr[...] = b

        st = st_ref[hd]
        b_end = b_scr[pl.ds(0 if rev else c - 1, 1), :]
        k_out = (k * jnp.exp(b_end - b)).astype(BF16)
        st_ref[hd] = jnp.exp(b_end) * st + _tn_dot(v, k_out)
        if not full:
            out_ref[0, :, sl] = jnp.zeros((c, dk), out_ref.dtype)
            return
        qz = q_ref[0, :, sl].astype(F32)
        q = qz * _sigmoid(qz)

        odd = (row & 1) == 1
        f_prev = pltpu.roll(f, 1, axis=0)
        f_next = pltpu.roll(f, c - 1, axis=0)
        qf = q * f
        scores = jnp.zeros((c, c), F32)
        for level in range(nlev):
            half = 1 << level
            if level == 0:
                xq, xk = qf, k
            elif level == 1:
                if rev:
                    xq, xk = qf * jnp.where(odd, 1.0, f_next), k * jnp.where(odd, f_prev, 1.0)
                else:
                    xq, xk = qf * jnp.where(odd, f_prev, 1.0), k * jnp.where(odd, 1.0, f_next)
            else:
                blk = 2 * half
                pieces = []
                for j in range(c // blk):
                    brow = b_scr[pl.ds(j * blk + (half if rev else half - 1), 1), :]
                    pieces.append(jnp.broadcast_to(brow, (blk, dk)))
                ref_b = jnp.concatenate(pieces, axis=0) if len(pieces) > 1 else pieces[0]
                decay = jnp.exp(-jnp.abs(b - ref_b))
                xq, xk = q * decay, k * decay
            scores = jnp.where(lev == level, _nt_dot(xq.astype(BF16), xk.astype(BF16)), scores)
        scores = jnp.where(lev == -1, jnp.sum(q * k, axis=-1, keepdims=True), scores)
        o = _dot(scores.astype(BF16), v) + _nt_dot((q * jnp.exp(b)).astype(BF16), st.astype(BF16))

        if readout:
            o = o + of_ref[0, :, sl].astype(F32)
            o = o * lax.rsqrt(jnp.mean(o * o, axis=-1, keepdims=True) + RMS_EPS) * ng_ref[...]
            gz = g_ref[0, :, sl].astype(F32)
            o = o * (gz * _sigmoid(gz))
        out_ref[0, :, sl] = o.astype(out_ref.dtype)

    per_iter = b_scr_all.shape[0]

    def all_heads(full):
        def head_group(i, carry):
            for slot in range(per_iter):
                head(i * per_iter + slot, b_scr_all.at[slot], full)
            return carry

        lax.fori_loop(0, nheads // per_iter, head_group, 0)

    if state_only_steps:
        pl.when(pl.program_id(1) < state_only_steps)(lambda: all_heads(False))
        pl.when(pl.program_id(1) >= state_only_steps)(lambda: all_heads(True))
    else:
        all_heads(True)


def _gla(p_hg, lb_row, norm_g, o_fwd, n_lat, n_ctx, width, dk, rev, ctx_out):
    bsz, total, _ = p_hg.shape
    c = GLA_C
    nl, nc = n_lat // c, n_ctx // c
    nheads = width // dk
    readout = rev

    if rev:
        def chunk(i):
            return jnp.where(i < nc, nl + nc - 1 - i, nl + nc - 1 - i)
    else:
        def chunk(i):
            return jnp.where(i < nc, nl + i, i - nc)

    def col_spec(colblk):
        return pl.BlockSpec((1, c, width), lambda b, i: (b, chunk(i), colblk))

    lev, tri = _gla_tables(c, rev)
    in_specs = [col_spec(0), col_spec(2 if rev else 1), col_spec(3),
                pl.BlockSpec((1, width), lambda b, i: (0, 0)),
                pl.BlockSpec((c, c), lambda b, i: (0, 0)),
                pl.BlockSpec((c, c), lambda b, i: (0, 0))]
    args = [p_hg, p_hg, p_hg, lb_row.reshape(1, width), lev, tri]
    if readout:
        in_specs += [col_spec(4),
                     pl.BlockSpec((1, c, width), lambda b, i: (b, chunk(i), 0)),
                     pl.BlockSpec((1, dk), lambda b, i: (0, 0))]
        args += [p_hg, o_fwd, norm_g.reshape(1, dk)]
    return pl.pallas_call(
        functools.partial(_gla_kernel, rev=rev, readout=readout, c=c, nheads=nheads, dk=dk,
                          state_only_steps=0 if ctx_out else nc),
        grid=(bsz, nl + nc),
        in_specs=in_specs,
        out_specs=pl.BlockSpec((1, c, width), lambda b, i: (b, chunk(i), 0)),
        out_shape=jax.ShapeDtypeStruct((bsz, total, width), BF16),
        scratch_shapes=[pltpu.VMEM((nheads, dk, dk), F32), pltpu.VMEM((math.gcd(HEADS_PER_ITER, nheads), c, dk), F32)],
        compiler_params=pltpu.CompilerParams(
            dimension_semantics=("parallel", "arbitrary"), vmem_limit_bytes=VMEM_LIMIT),
        name="gla_bwd" if rev else "gla_fwd",
    )(*args)


def _s5_tables(a_re, a_im, log_dt, b_re, b_im, c_re, c_im):
    t = S5_T
    ngrp, nst, cg = b_re.shape
    a_re = jnp.minimum(a_re.astype(F32), S5_MAX_RE)
    a_im = a_im.astype(F32)
    dt = jnp.exp(log_dt.astype(F32))[..., None]
    mag = jnp.exp(dt * a_re)
    abr, abi = mag * jnp.cos(dt * a_im), mag * jnp.sin(dt * a_im)
    den = a_re * a_re + a_im * a_im
    nr, ni = abr - 1.0, abi
    f_re = ((nr * a_re + ni * a_im) / den)[..., None]
    f_im = ((ni * a_re - nr * a_im) / den)[..., None]
    b_re, b_im = b_re.astype(F32)[None], b_im.astype(F32)[None]
    bb_re = f_re * b_re - f_im * b_im
    bb_im = f_re * b_im + f_im * b_re
    pr, pi = [jnp.ones_like(abr)], [jnp.zeros_like(abr)]
    for _ in range(t):
        pr.append(pr[-1] * abr - pi[-1] * abi)
        pi.append(pr[-2] * abi + pi[-1] * abr)
    pr, pi = jnp.stack(pr), jnp.stack(pi)
    c_re, c_im = c_re.astype(F32), c_im.astype(F32)
    l_re = c_re[None, None] * pr[:, :, :, None, :] - c_im[None, None] * pi[:, :, :, None, :]
    l_im = c_re[None, None] * pi[:, :, :, None, :] + c_im[None, None] * pr[:, :, :, None, :]
    kern = (jnp.einsum("drgcn,rgnk->drgck", l_re[:t], bb_re, precision=_HI)
            - jnp.einsum("drgcn,rgnk->drgck", l_im[:t], bb_im, precision=_HI))
    lag = np.arange(t)[:, None] - np.arange(t)[None, :]
    sel = np.stack([lag[:, :, None] == np.arange(t), -lag[:, :, None] == np.arange(t)])
    mt = jnp.einsum("rtsd,drgck->gsktc", jnp.asarray(sel, BF16), kern.astype(BF16),
                    preferred_element_type=F32).reshape(ngrp, t * cg, t * cg)
    pf_r, pf_i = pr[t - 1 - np.arange(t), 0], pi[t - 1 - np.arange(t), 0]
    pb_r, pb_i = pr[np.arange(t), 1], pi[np.arange(t), 1]

    def drive(p_r, p_i, r):
        g_r = p_r[..., None] * bb_re[r][None] - p_i[..., None] * bb_im[r][None]
        g_i = p_r[..., None] * bb_im[r][None] + p_i[..., None] * bb_re[r][None]
        to = lambda x: x.transpose(1, 0, 3, 2).reshape(ngrp, t * cg, nst)
        return to(g_r), to(g_i)

    gf_r, gf_i = drive(pf_r, pf_i, 0)
    gb_r, gb_i = drive(pb_r, pb_i, 1)
    def read(idx, r):
        rr = l_re[idx, r].transpose(1, 3, 0, 2).reshape(ngrp, nst, t * cg)
        ri = -l_im[idx, r].transpose(1, 3, 0, 2).reshape(ngrp, nst, t * cg)
        return rr, ri

    of_r, of_i = read(np.arange(t) + 1, 0)
    ob_r, ob_i = read(t - np.arange(t), 1)

    eye = jnp.eye(2, dtype=F32)
    g4 = jnp.stack([gf_r, gf_i, gb_r, gb_i], axis=2).reshape(ngrp // 2, 2, t * cg, 4, 1, nst)
    gin = (g4 * eye[None, :, None, None, :, None]).reshape(ngrp // 2, 2 * t * cg, 8 * nst)
    o4 = jnp.stack([of_r, of_i, ob_r, ob_i], axis=1).reshape(ngrp // 2, 2, 4, nst, 1, t * cg)
    o4 = o4.transpose(0, 2, 1, 3, 4, 5)
    gout = (o4 * eye[None, None, :, None, :, None]).reshape(ngrp // 2, 8 * nst, 2 * t * cg)
    a_chunk = jnp.stack([pr[t, 0], pi[t, 0], pr[t, 1], pi[t, 1]]).reshape(4, ngrp * nst)
    return mt.astype(BF16), gin.astype(BF16), gout.astype(BF16), a_chunk


def _s5_kernel(z_ref, mt_ref, gin_ref, gout_ref, ac_ref, skip_ref, y_ref, d_scr, x_scr,
               *, n_lat, n_ctx, npairs, pw):
    nrows = n_lat + n_ctx
    gw = pw // 2
    tw = pw // 4
    for p in range(npairs):
        d = _dot(z_ref[0, :, p * pw:(p + 1) * pw], gin_ref[p])
        for j in range(4):
            d_scr[j, :, p * tw:(p + 1) * tw] = d[:, j * tw:(j + 1) * tw]

    def scan(plane, forward):
        ar = ac_ref[plane:plane + 1, :]
        ai = ac_ref[plane + 1:plane + 2, :]

        def body(i, carry):
            xr, xi = carry
            if forward:
                n = jnp.where(i < n_ctx, n_lat + i, i - n_ctx)
            else:
                n = nrows - 1 - i
            x_scr[plane, pl.ds(n, 1), :] = xr
            x_scr[plane + 1, pl.ds(n, 1), :] = xi
            dr = d_scr[plane, pl.ds(n, 1), :]
            di = d_scr[plane + 1, pl.ds(n, 1), :]
            return ar * xr - ai * xi + dr, ar * xi + ai * xr + di

        zero = jnp.zeros((1, d_scr.shape[2]), F32)
        lax.fori_loop(0, nrows, body, (zero, zero))

    scan(0, True)
    scan(2, False)

    for p in range(npairs):
        xp = jnp.concatenate([x_scr[j, :, p * tw:(p + 1) * tw] for j in range(4)], axis=1).astype(BF16)
        y = _dot(xp, gout_ref[p])
        for g in range(2):
            lo = p * pw + g * gw
            zg = z_ref[0, :, lo:lo + gw]
            yg = y[:, g * gw:(g + 1) * gw] + _dot(zg, mt_ref[2 * p + g]) + skip_ref[:, lo:lo + gw] * zg.astype(F32)
            y_ref[0, :, lo:lo + gw] = yg.astype(y_ref.dtype)


def _s5(zf, tables, layer, skip_flat, n_lat_chunks, n_ctx_chunks):
    bsz, nrows, wide = zf.shape
    mt, gin, gout, a_chunk = tables
    npairs, pw = gin.shape[1:3]
    planes = a_chunk.shape[2]
    return pl.pallas_call(
        functools.partial(_s5_kernel, n_lat=n_lat_chunks, n_ctx=n_ctx_chunks, npairs=npairs, pw=pw),
        grid=(bsz,),
        in_specs=[pl.BlockSpec((1, nrows, wide), lambda b: (b, 0, 0)),
                  _layer_spec(mt, layer), _layer_spec(gin, layer), _layer_spec(gout, layer),
                  _layer_spec(a_chunk, layer), _const_spec(skip_flat.shape)],
        out_specs=pl.BlockSpec((1, nrows, wide), lambda b: (b, 0, 0)),
        out_shape=jax.ShapeDtypeStruct((bsz, nrows, wide), BF16),
        scratch_shapes=[pltpu.VMEM((4, nrows, planes), F32), pltpu.VMEM((4, nrows, planes), F32)],
        compiler_params=pltpu.CompilerParams(
            dimension_semantics=("parallel",), vmem_limit_bytes=VMEM_LIMIT),
        name="s5",
    )(zf, mt, gin, gout, a_chunk, skip_flat)


def _gelu_tanh(x):
    return 0.5 * x * (1.0 + jnp.tanh(math.sqrt(2.0 / math.pi) * (x + 0.044715 * (x * x * x))))


def _merge_kernel(h_ref, yh_ref, yz_ref, gate_ref, mod_ref, whg_ref, wglu_ref, wbr_ref, wout_ref, o_ref, y_scr,
                  *, ngrp, cg):
    d = h_ref.shape[2]
    y_hg = _dot(yh_ref[0], whg_ref[...])
    ys = _gelu_tanh(_chunk_unflatten(yz_ref, y_scr, yz_ref.shape[1], ngrp, cg))
    glu = _sigmoid(_dot(ys.astype(BF16), wglu_ref[...]))
    y_s5 = _dot((ys * glu).astype(BF16), wbr_ref[...])
    gate = gate_ref[0].astype(F32)
    merged = _sigmoid(gate[:, :d]) * y_hg + _sigmoid(gate[:, d:]) * y_s5
    o_ref[0] = h_ref[0] + mod_ref[0, 2:3, :] * _dot(merged.astype(BF16), wout_ref[...])


def _merge(h, yh, yz, p_gate, mods, mod_row_fn, whg, wglu, wbr, wout, layer, ngrp, cg, tok_off_blocks):
    bsz, length, d = h.shape
    nt = length // TOK_TILE
    ct = TOK_TILE // S5_T
    seq = lambda rows, w: pl.BlockSpec((1, rows, w), lambda b, t: (b, t + tok_off_blocks, 0))
    return pl.pallas_call(
        functools.partial(_merge_kernel, ngrp=ngrp, cg=cg),
        grid=(bsz, nt),
        in_specs=[pl.BlockSpec((1, TOK_TILE, d), lambda b, t: (b, t, 0)),
                  seq(TOK_TILE, yh.shape[2]), seq(ct, yz.shape[2]), seq(TOK_TILE, p_gate.shape[2]),
                  pl.BlockSpec((1, 6, d), lambda b, t: (mod_row_fn(b), 0, 0)),
                  _layer_spec(whg, layer), _layer_spec(wglu, layer), _layer_spec(wbr, layer),
                  _layer_spec(wout, layer)],
        out_specs=pl.BlockSpec((1, TOK_TILE, d), lambda b, t: (b, t, 0)),
        out_shape=jax.ShapeDtypeStruct(h.shape, F32),
        scratch_shapes=[pltpu.VMEM((ngrp * cg // 128, TOK_TILE, 128), F32)],
        input_output_aliases={0: 0},
        compiler_params=pltpu.CompilerParams(
            dimension_semantics=("parallel", "parallel"), vmem_limit_bytes=VMEM_LIMIT),
        name="merge",
    )(h, yh, yz, p_gate, mods, whg, wglu, wbr, wout)


def _ffn_kernel(h_ref, mod_ref, g_ref, wab_ref, cw_ref, cb_ref, wd_ref, fg_ref,
                o_ref, xn_scr, up_scr, act_scr, *, n_tok, width, vertical, cw, mrows, final):
    j = pl.program_id(1)
    nj = pl.num_programs(1)
    nblk = n_tok // mrows
    rpb = mrows // width
    pad = width

    @pl.when(j == 0)
    def _():
        def norm_body(i, carry):
            r0 = pl.multiple_of(i * mrows, mrows)
            xn_scr[pl.ds(r0, mrows), :] = _modulated_norm(
                h_ref[0, pl.ds(r0, mrows), :], g_ref[...], mod_ref[0, 3:4, :], mod_ref[0, 4:5, :]).astype(BF16)
            o_ref[0, pl.ds(r0, mrows), :] = jnp.zeros((mrows, o_ref.shape[2]), F32)
            return carry

        lax.fori_loop(0, nblk, norm_body, 0)
        xn_scr[n_tok:n_tok + width, :] = jnp.zeros((width, xn_scr.shape[1]), BF16)
        for copy in range(3):
            up_scr[copy, 0:pad, :] = jnp.zeros((pad, 2 * cw), F32)
            up_scr[copy, pad + n_tok:pad + n_tok + width, :] = jnp.zeros((width, 2 * cw), F32)

    def up_rows(row0, m):
        up = _dot(xn_scr[pl.ds(row0, m), :], wab_ref[...])
        col = lax.broadcasted_iota(jnp.int32, (m, 1), 0) % width
        up_scr[0, pl.ds(pad + row0, m), :] = jnp.where(col == 0, 0.0, pltpu.roll(up, 1, axis=0))
        up_scr[1, pl.ds(pad + row0, m), :] = up
        up_scr[2, pl.ds(pad + row0, m), :] = jnp.where(col == width - 1, 0.0, pltpu.roll(up, m - 1, axis=0))

    def conv_row(row0, lane0):
        acc = cb_ref[:, lane0:lane0 + 128]
        for dr in ((-1, 0, 1) if vertical else (0,)):
            for dc in (-1, 0, 1):
                tap = (dr + 1) * 3 + (dc + 1)
                src = up_scr[dc + 1, pl.ds(pl.multiple_of(pad + row0 + dr * width, 8), width), lane0:lane0 + 128]
                acc = acc + cw_ref[tap:tap + 1, lane0:lane0 + 128] * src
        return acc

    def conv_block(blk):
        r0 = blk * mrows
        for lt in range(cw // 128):
            for r in range(rpb):
                row0 = r0 + r * width
                ca = conv_row(row0, lt * 128)
                cb = conv_row(row0, cw + lt * 128)
                act_scr[pl.ds(pl.multiple_of(row0, 16), width), lt * 128:(lt + 1) * 128] = (
                    ca * _sigmoid(ca) * cb).astype(BF16)

    def down_rows(row0):
        o_ref[0, pl.ds(row0, mrows), :] += _dot(act_scr[pl.ds(row0, mrows), :], wd_ref[...])

    if nblk == 1:
        up_rows(0, mrows)
        conv_block(0)
        down_rows(0)
    else:
        streams = math.gcd(FFN_STREAMS, nblk)
        per = nblk // streams
        ahead = width if vertical else 0
        if vertical:
            up_rows(0, width)
            for s in range(1, streams):
                up_rows(s * per * mrows - width, 2 * width)

        def body(k, carry):
            for s in range(streams):
                r0 = pl.multiple_of((k + s * per) * mrows, mrows)
                up_rows(r0 + ahead, mrows)
            for s in range(streams):
                conv_block(k + s * per)
            for s in range(streams):
                down_rows(pl.multiple_of((k + s * per) * mrows, mrows))
            return carry

        lax.fori_loop(0, per, body, 0)

    @pl.when(j == nj - 1)
    def _():
        def fin_body(i, carry):
            r0 = pl.multiple_of(i * mrows, mrows)
            out = h_ref[0, pl.ds(r0, mrows), :] + mod_ref[0, 5:6, :] * o_ref[0, pl.ds(r0, mrows), :]
            if final:
                out = out * lax.rsqrt(jnp.mean(out * out, axis=-1, keepdims=True) + RMS_EPS) * fg_ref[...]
            o_ref[0, pl.ds(r0, mrows), :] = out
            return carry

        lax.fori_loop(0, nblk, fin_body, 0)


def _ffn(h, mods, mod_row_fn, norm_g, ffn_w, layer, final_g, width, vertical, final):
    bsz, n_tok, d = h.shape
    wab, cwf, cbf, wd = ffn_w
    cw = FFN_TILE
    nj = wd.shape[1] // cw
    return pl.pallas_call(
        functools.partial(_ffn_kernel, n_tok=n_tok, width=width, vertical=vertical, cw=cw, mrows=TOK_TILE,
                          final=final),
        grid=(bsz, nj),
        in_specs=[pl.BlockSpec((1, n_tok, d), lambda b, j: (b, 0, 0), pipeline_mode=pl.Buffered(1)),
                  pl.BlockSpec((1, 6, d), lambda b, j: (mod_row_fn(b), 0, 0)),
                  pl.BlockSpec((1, d), lambda b, j: (0, 0)),
                  pl.BlockSpec((None, d, 2 * cw), lambda b, j: (layer, 0, j)),
                  pl.BlockSpec((None, 9, 2 * cw), lambda b, j: (layer, 0, j)),
                  pl.BlockSpec((None, 1, 2 * cw), lambda b, j: (layer, 0, j)),
                  pl.BlockSpec((None, cw, d), lambda b, j: (layer, j, 0)),
                  pl.BlockSpec((1, d), lambda b, j: (0, 0))],
        out_specs=pl.BlockSpec((1, n_tok, d), lambda b, j: (b, 0, 0)),
        out_shape=jax.ShapeDtypeStruct(h.shape, F32),
        scratch_shapes=[pltpu.VMEM((n_tok + width, d), BF16),
                        pltpu.VMEM((3, n_tok + 2 * width, 2 * cw), F32),
                        pltpu.VMEM((n_tok, cw), BF16)],
        input_output_aliases={0: 0},
        compiler_params=pltpu.CompilerParams(
            dimension_semantics=("parallel", "arbitrary"), vmem_limit_bytes=VMEM_LIMIT),
        name="ffn",
    )(h, mods, norm_g.reshape(1, d), wab, cwf, cbf, wd, final_g.reshape(1, d))


def _ffn_weights(w_up, conv_w, conv_b, w_down):
    depth, ffn, _ = w_down.shape
    cw = FFN_TILE
    fp = ((ffn + cw - 1) // cw) * cw
    nj = fp // cw

    def tiles(x):
        lead = x.shape[:-1]
        halves = []
        for part in (x[..., :ffn], x[..., ffn:]):
            part = jnp.pad(part, [(0, 0)] * len(lead) + [(0, fp - ffn)])
            halves.append(part.reshape(lead + (nj, cw)))
        return jnp.concatenate(halves, axis=-1).reshape(lead + (nj * 2 * cw,))

    wd = jnp.pad(w_down.astype(BF16), [(0, 0), (0, fp - ffn), (0, 0)])
    return (tiles(w_up.astype(BF16)), tiles(conv_w.reshape(depth, 9, 2 * ffn)),
            tiles(conv_b.reshape(depth, 1, 2 * ffn)), wd)


def _lower_bounds(lb_raw):
    p = jax.nn.softmax(lb_raw.astype(F32), axis=0)
    cs = jnp.cumsum(p, axis=0)
    return cs - cs[0:1]


def kernel(x, c, ctx, c_ctx, ada_w, ada_b, norm1_g, w_in, hg_lb_raw, hg_norm_g, w_hg_br, s5_a_re, s5_a_im, s5_log_dt, s5_b_re, s5_b_im, s5_c_re, s5_c_im, s5_d, w_s5_glu, w_s5_br, w_out, norm2_g, w_up, ffn_conv_w, ffn_conv_b, w_down, final_g):
    bsz, n_lat, d = x.shape
    n_ctx = ctx.shape[1]
    depth = ada_w.shape[0]
    dk = hg_norm_g.shape[1]
    hg_w = hg_lb_raw.shape[2]
    ngrp, nst, cg = s5_b_re.shape[1:]
    assert n_lat % TOK_TILE == 0 and n_ctx % TOK_TILE == 0 and n_lat % GLA_C == 0 and n_ctx % GLA_C == 0
    assert 5 * hg_w + ngrp * cg + 2 * d == w_in.shape[2] and S5_T * cg == 256 and 128 % cg == 0
    assert ngrp % (128 // cg) == 0

    rows = ((bsz + 1 + 7) // 8) * 8
    cond = jnp.zeros((rows, d), F32).at[:bsz].set(c).at[bsz].set(c_ctx)
    mods = _ada_mod(cond, ada_w, ada_b).reshape(depth, rows, 6, d)
    lat_row = lambda b: b
    ctx_row = lambda b: bsz
    lbs = _lower_bounds(hg_lb_raw)
    lat_blk = n_lat // TOK_TILE

    w_in_b, whg_b, wglu_b, wbr_b, wout_b = (w.astype(BF16) for w in (w_in, w_hg_br, w_s5_glu, w_s5_br, w_out))
    ffn_w = _ffn_weights(w_up, ffn_conv_w, ffn_conv_b, w_down)
    s5_tabs = jax.vmap(_s5_tables)(s5_a_re, s5_a_im, s5_log_dt, s5_b_re, s5_b_im, s5_c_re, s5_c_im)
    skip = jnp.broadcast_to(s5_d.astype(F32).reshape(depth, ngrp, 1, cg), (depth, ngrp, S5_T, cg))
    skip = skip.reshape(depth, 1, ngrp * S5_T * cg)

    h_lat, h_ctx = x, ctx
    for l in range(depth):
        last = l == depth - 1
        p_hg, z, p_gate = _proj_in(h_lat, h_ctx, mods[l], norm1_g[l], w_in_b, l, 5 * hg_w, ngrp, cg)

        o_fwd = _gla(p_hg, lbs[l, 0], hg_norm_g[l], None, n_lat, n_ctx, hg_w, dk, rev=False, ctx_out=not last)
        yh = _gla(p_hg, lbs[l, 1], hg_norm_g[l], o_fwd, n_lat, n_ctx, hg_w, dk, rev=True, ctx_out=not last)
        yz = _s5(z, s5_tabs, l, skip[l], n_lat // S5_T, n_ctx // S5_T)

        wts = (whg_b, wglu_b, wbr_b, wout_b, l, ngrp, cg)
        h_lat = _merge(h_lat, yh, yz, p_gate, mods[l], lat_row, *wts, 0)
        h_lat = _ffn(h_lat, mods[l], lat_row, norm2_g[l], ffn_w, l, final_g, GRID_W, True, last)
        if not last:
            h_ctx = _merge(h_ctx, yh, yz, p_gate, mods[l], ctx_row, *wts, lat_blk)
            stack = math.gcd(bsz, max(1, n_lat // n_ctx))
            h_ctx = _ffn(h_ctx.reshape(bsz // stack, stack * n_ctx, d), mods[l], ctx_row, norm2_g[l], ffn_w, l,
                         final_g, n_ctx, False, False).reshape(bsz, n_ctx, d)
    return h_lat
```

```python
import functools
import math

import numpy as np
import jax
import jax.numpy as jnp
from jax import lax
from jax.experimental import pallas as pl
from jax.experimental.pallas import tpu as pltpu

GRID_W = 64
RMS_EPS = 1e-6
S5_MAX_RE = -1e-4
LOG2_E = 1.4426950408889634
S5_T = 16
GLA_C = 128
HEADS_PER_ITER = 8
TOK_TILE = 256
FFN_TILE = 256
FFN_STREAMS = 4
VMEM_LIMIT = 56 * 1024 * 1024

F32 = jnp.float32
BF16 = jnp.bfloat16
_HI = lax.Precision.HIGHEST


def _nt_dot(a, b):
    return lax.dot_general(a, b, (((1,), (1,)), ((), ())), preferred_element_type=F32)


def _tn_dot(a, b):
    return lax.dot_general(a, b, (((0,), (0,)), ((), ())), preferred_element_type=F32)


def _dot(a, b):
    return jnp.dot(a, b, preferred_element_type=F32)


def _sigmoid(x):
    return 1.0 / (1.0 + jnp.exp(-x))


def _const_spec(shape):
    nd = len(shape)
    return pl.BlockSpec(shape, lambda *_: (0,) * nd, pipeline_mode=pl.Buffered(1))


def _layer_spec(stacked, layer):
    rest = stacked.shape[1:]
    return pl.BlockSpec((None,) + rest, lambda *_: (layer,) + (0,) * len(rest), pipeline_mode=pl.Buffered(1))


def _cast_kernel(x_ref, o_ref):
    o_ref[...] = x_ref[...].astype(o_ref.dtype)


def _to_bf16(w):
    cols = w.shape[-1]
    x = w.reshape(-1, cols)
    rows = x.shape[0]
    blk = math.gcd(rows, TOK_TILE)
    out = pl.pallas_call(
        _cast_kernel,
        grid=(rows // blk,),
        in_specs=[pl.BlockSpec((blk, cols), lambda i: (i, 0))],
        out_specs=pl.BlockSpec((blk, cols), lambda i: (i, 0)),
        out_shape=jax.ShapeDtypeStruct(x.shape, BF16),
        compiler_params=pltpu.CompilerParams(dimension_semantics=("parallel",), vmem_limit_bytes=VMEM_LIMIT),
        name="cast_bf16",
    )(x)
    return out.reshape(w.shape)


def _ada_kernel(cond_ref, w_ref, b_ref, o_ref):
    cnd = cond_ref[...]
    s = cnd * _sigmoid(cnd)
    o_ref[0] = jnp.dot(s, w_ref[0], preferred_element_type=F32, precision=_HI) + b_ref[0]


def _ada_mod(cond, ada_w, ada_b):
    depth, d, n6 = ada_w.shape
    rows = cond.shape[0]
    tn = 512
    return pl.pallas_call(
        _ada_kernel,
        grid=(depth, n6 // tn),
        in_specs=[
            pl.BlockSpec((rows, d), lambda l, j: (0, 0)),
            pl.BlockSpec((1, d, tn), lambda l, j: (l, 0, j)),
            pl.BlockSpec((1, 1, tn), lambda l, j: (l, 0, j)),
        ],
        out_specs=pl.BlockSpec((1, rows, tn), lambda l, j: (l, 0, j)),
        out_shape=jax.ShapeDtypeStruct((depth, rows, n6), F32),
        name="ada_mod",
    )(cond, ada_w, ada_b.reshape(depth, 1, n6))


def _modulated_norm(x, g, shift, scale):
    ms = jnp.mean(x * x, axis=-1, keepdims=True)
    y = x * lax.rsqrt(ms + RMS_EPS) * g
    return y * (1.0 + scale) + shift


def _strip_masks(n_rows, cg):
    lane = lax.broadcasted_iota(jnp.int32, (n_rows, 128), 1)
    return [(lane >= j * cg) & (lane < (j + 1) * cg) for j in range(128 // cg)]


def _chunk_flatten(u_scr, z_ref, n_chunks, ngrp, cg):
    gpt = 128 // cg
    masks = _strip_masks(n_chunks, cg)
    for g in range(ngrp):
        k, i = divmod(g, gpt)
        for hf in range(S5_T // gpt):
            acc = None
            for j in range(gpt):
                src = u_scr[k, pl.ds(hf * gpt + j, n_chunks, stride=S5_T), :]
                shift = ((j - i) * cg) % 128
                if shift:
                    src = pltpu.roll(src, shift, axis=1)
                acc = src if acc is None else jnp.where(masks[j], src, acc)
            tile = g * (S5_T // gpt) + hf
            z_ref[0, :, tile * 128:(tile + 1) * 128] = acc.astype(z_ref.dtype)


def _chunk_unflatten(yz_ref, scr, n_chunks, ngrp, cg):
    gpt = 128 // cg
    masks = _strip_masks(n_chunks, cg)
    for k in range(ngrp // gpt):
        for hf in range(S5_T // gpt):
            srcs = []
            for i in range(gpt):
                tile = (k * gpt + i) * (S5_T // gpt) + hf
                srcs.append(yz_ref[0, :, tile * 128:(tile + 1) * 128].astype(F32))
            for j in range(gpt):
                acc = None
                for i in range(gpt):
                    shift = ((i - j) * cg) % 128
                    src = pltpu.roll(srcs[i], shift, axis=1) if shift else srcs[i]
                    acc = src if acc is None else jnp.where(masks[i], src, acc)
                t = hf * gpt + j
                scr[k, t * n_chunks:(t + 1) * n_chunks, :] = acc
    return jnp.concatenate(
        [jnp.concatenate([scr[k, pl.ds(ch, S5_T, stride=n_chunks), :] for k in range(scr.shape[0])], axis=1)
         for ch in range(n_chunks)], axis=0)


def _proj_in_kernel(hl_ref, hc_ref, mod_ref, g_ref, w_ref, hg_ref, z_ref, gate_ref, xn_scr, u_scr,
                    *, nt_lat, ngrp, cg):
    def norm(h_ref):
        xn_scr[...] = _modulated_norm(h_ref[0], g_ref[...], mod_ref[0, 0:1, :], mod_ref[0, 1:2, :]).astype(BF16)

    pl.when(pl.program_id(1) < nt_lat)(lambda: norm(hl_ref))
    pl.when(pl.program_id(1) >= nt_lat)(lambda: norm(hc_ref))
    xn = xn_scr[...]
    hg_w, s5_w = hg_ref.shape[2], ngrp * cg

    def project(o_ref, col0):
        width = o_ref.shape[2]
        step = math.gcd(512, width)
        for c in range(0, width, step):
            o_ref[0, :, c:c + step] = _dot(xn, w_ref[:, col0 + c:col0 + c + step]).astype(o_ref.dtype)

    u = _dot(xn, w_ref[:, hg_w:hg_w + s5_w])
    for k in range(s5_w // 128):
        u_scr[k] = u[:, k * 128:(k + 1) * 128]
    _chunk_flatten(u_scr, z_ref, xn_scr.shape[0] // S5_T, ngrp, cg)
    project(hg_ref, 0)
    project(gate_ref, hg_w + s5_w)


def _proj_in(h_lat, h_ctx, mods, norm_g, w_stack, layer, hg_cols, ngrp, cg):
    bsz, n_lat, d = h_lat.shape
    n_ctx = h_ctx.shape[1]
    total = n_lat + n_ctx
    nt_lat, nt_ctx = n_lat // TOK_TILE, n_ctx // TOK_TILE
    s5_w = ngrp * cg
    gate_cols = w_stack.shape[2] - hg_cols - s5_w
    ct = TOK_TILE // S5_T
    return pl.pallas_call(
        functools.partial(_proj_in_kernel, nt_lat=nt_lat, ngrp=ngrp, cg=cg),
        grid=(bsz, nt_lat + nt_ctx),
        in_specs=[
            pl.BlockSpec((1, TOK_TILE, d), lambda b, t: (b, jnp.minimum(t, nt_lat - 1), 0)),
            pl.BlockSpec((1, TOK_TILE, d), lambda b, t: (b, jnp.maximum(t - nt_lat, 0), 0)),
            pl.BlockSpec((1, 6, d), lambda b, t: (jnp.where(t < nt_lat, b, bsz), 0, 0)),
            pl.BlockSpec((1, d), lambda b, t: (0, 0)),
            _layer_spec(w_stack, layer),
        ],
        out_specs=[pl.BlockSpec((1, TOK_TILE, hg_cols), lambda b, t: (b, t, 0)),
                   pl.BlockSpec((1, ct, s5_w * S5_T), lambda b, t: (b, t, 0)),
                   pl.BlockSpec((1, TOK_TILE, gate_cols), lambda b, t: (b, t, 0))],
        out_shape=[jax.ShapeDtypeStruct((bsz, total, hg_cols), BF16),
                   jax.ShapeDtypeStruct((bsz, total // S5_T, s5_w * S5_T), BF16),
                   jax.ShapeDtypeStruct((bsz, total, gate_cols), BF16)],
        scratch_shapes=[pltpu.VMEM((TOK_TILE, d), BF16), pltpu.VMEM((s5_w // 128, TOK_TILE, 128), F32)],
        compiler_params=pltpu.CompilerParams(
            dimension_semantics=("parallel", "arbitrary"), vmem_limit_bytes=VMEM_LIMIT),
        name="proj_in",
    )(h_lat, h_ctx, mods, norm_g.reshape(1, d), w_stack)


def _gla_tables(c, rev):
    t = np.arange(c)[:, None]
    s = np.arange(c)[None, :]
    x = t ^ s
    lev = np.full((c, c), -2, np.int32)
    causal = (t < s) if rev else (t > s)
    with np.errstate(divide="ignore"):
        hb = np.floor(np.log2(np.maximum(x, 1))).astype(np.int32)
    lev = np.where(causal, hb, lev)
    lev = np.where(t == s, -1, lev).astype(np.int32)
    tri = ((t <= s) if rev else (t >= s)).astype(np.float32)
    return jnp.asarray(lev), jnp.asarray(tri, dtype=BF16)


def _gla_kernel(*refs, rev, readout, c, nheads, dk, state_only_steps):
    if readout:
        (q_ref, fz_ref, i_ref, lb_ref, lev_ref, tri_ref, g_ref, of_ref, ng_ref,
         out_ref, st_ref, b_scr_all) = refs
    else:
        q_ref, fz_ref, i_ref, lb_ref, lev_ref, tri_ref, out_ref, st_ref, b_scr_all = refs
    nlev = int(math.log2(c))

    @pl.when(pl.program_id(1) == 0)
    def _():
        st_ref[...] = jnp.zeros_like(st_ref)

    lev = lev_ref[...]
    tri = tri_ref[...]
    row = lax.broadcasted_iota(jnp.int32, (c, 1), 0)

    def head(hd, b_scr, full):
        sl = pl.ds(pl.multiple_of(hd * dk, dk), dk)
        z = fz_ref[0, :, sl].astype(F32)
        v = i_ref[0, :, sl]
        lb = lb_ref[0:1, sl]

        e = jnp.exp(-jnp.abs(z))
        den = 1.0 + e
        inv = 1.0 / den
        pos = z >= 0
        sig_pos = jnp.where(pos, 1.0, e) * inv
        sig_neg = jnp.where(pos, e, 1.0) * inv
        f = lb + (1.0 - lb) * sig_pos
        a = jnp.where(lb > 0, jnp.log(f), jnp.minimum(z, 0.0) - jnp.log(den)) * LOG2_E
        k = (1.0 - lb) * sig_neg

        a_hi = a.astype(BF16)
        a_lo = (a - a_hi.astype(F32)).astype(BF16)
        bb = _dot(tri, jnp.concatenate([a_hi, a_lo], axis=1))
        b = bb[:, :dk] + bb[:, dk:]
        b_scr[...] = b

        st = st_ref[hd]
        b_end = b_scr[pl.ds(0 if rev else c - 1, 1), :]
        k_out = (k * jnp.exp2(b_end - b)).astype(BF16)
        st_ref[hd] = jnp.exp2(b_end) * st + _tn_dot(v, k_out)
        if not full:
            out_ref[0, :, sl] = jnp.zeros((c, dk), out_ref.dtype)
            return
        qz = q_ref[0, :, sl].astype(F32)
        q = qz * _sigmoid(qz)

        odd = (row & 1) == 1
        f_prev = pltpu.roll(f, 1, axis=0)
        f_next = pltpu.roll(f, c - 1, axis=0)
        qf = q * f
        scores = jnp.zeros((c, c), F32)
        for level in range(nlev):
            half = 1 << level
            if level == 0:
                xq, xk = qf, k
            elif level == 1:
                if rev:
                    xq, xk = qf * jnp.where(odd, 1.0, f_next), k * jnp.where(odd, f_prev, 1.0)
                else:
                    xq, xk = qf * jnp.where(odd, f_prev, 1.0), k * jnp.where(odd, 1.0, f_next)
            else:
                blk = 2 * half
                pieces = []
                for j in range(c // blk):
                    brow = b_scr[pl.ds(j * blk + (half if rev else half - 1), 1), :]
                    pieces.append(jnp.broadcast_to(brow, (blk, dk)))
                ref_b = jnp.concatenate(pieces, axis=0) if len(pieces) > 1 else pieces[0]
                decay = jnp.exp2(-jnp.abs(b - ref_b))
                xq, xk = q * decay, k * decay
            scores = jnp.where(lev == level, _nt_dot(xq.astype(BF16), xk.astype(BF16)), scores)
        scores = jnp.where(lev == -1, jnp.sum(q * k, axis=-1, keepdims=True), scores)
        o = _dot(scores.astype(BF16), v) + _nt_dot((q * jnp.exp2(b)).astype(BF16), st.astype(BF16))

        if readout:
            o = o + of_ref[0, :, sl].astype(F32)
            o = o * lax.rsqrt(jnp.mean(o * o, axis=-1, keepdims=True) + RMS_EPS) * ng_ref[...]
            gz = g_ref[0, :, sl].astype(F32)
            o = o * (gz * _sigmoid(gz))
        out_ref[0, :, sl] = o.astype(out_ref.dtype)

    per_iter = b_scr_all.shape[0]

    def all_heads(full):
        def head_group(i, carry):
            for slot in range(per_iter):
                head(i * per_iter + slot, b_scr_all.at[slot], full)
            return carry

        lax.fori_loop(0, nheads // per_iter, head_group, 0)

    if state_only_steps:
        pl.when(pl.program_id(1) < state_only_steps)(lambda: all_heads(False))
        pl.when(pl.program_id(1) >= state_only_steps)(lambda: all_heads(True))
    else:
        all_heads(True)


def _gla(p_hg, lb_row, norm_g, o_fwd, n_lat, n_ctx, width, dk, rev, ctx_out):
    bsz, total, _ = p_hg.shape
    c = GLA_C
    nl, nc = n_lat // c, n_ctx // c
    nheads = width // dk
    readout = rev

    if rev:
        def chunk(i):
            return jnp.where(i < nc, nl + nc - 1 - i, nl + nc - 1 - i)
    else:
        def chunk(i):
            return jnp.where(i < nc, nl + i, i - nc)

    def col_spec(colblk):
        return pl.BlockSpec((1, c, width), lambda b, i: (b, chunk(i), colblk))

    lev, tri = _gla_tables(c, rev)
    in_specs = [col_spec(0), col_spec(2 if rev else 1), col_spec(3),
                pl.BlockSpec((1, width), lambda b, i: (0, 0)),
                pl.BlockSpec((c, c), lambda b, i: (0, 0)),
                pl.BlockSpec((c, c), lambda b, i: (0, 0))]
    args = [p_hg, p_hg, p_hg, lb_row.reshape(1, width), lev, tri]
    if readout:
        in_specs += [col_spec(4),
                     pl.BlockSpec((1, c, width), lambda b, i: (b, chunk(i), 0)),
                     pl.BlockSpec((1, dk), lambda b, i: (0, 0))]
        args += [p_hg, o_fwd, norm_g.reshape(1, dk)]
    return pl.pallas_call(
        functools.partial(_gla_kernel, rev=rev, readout=readout, c=c, nheads=nheads, dk=dk,
                          state_only_steps=0 if ctx_out else nc),
        grid=(bsz, nl + nc),
        in_specs=in_specs,
        out_specs=pl.BlockSpec((1, c, width), lambda b, i: (b, chunk(i), 0)),
        out_shape=jax.ShapeDtypeStruct((bsz, total, width), BF16),
        scratch_shapes=[pltpu.VMEM((nheads, dk, dk), F32), pltpu.VMEM((math.gcd(HEADS_PER_ITER, nheads), c, dk), F32)],
        compiler_params=pltpu.CompilerParams(
            dimension_semantics=("parallel", "arbitrary"), vmem_limit_bytes=VMEM_LIMIT),
        name="gla_bwd" if rev else "gla_fwd",
    )(*args)


def _s5_tables(a_re, a_im, log_dt, b_re, b_im, c_re, c_im):
    t = S5_T
    ngrp, nst, cg = b_re.shape
    a_re = jnp.minimum(a_re.astype(F32), S5_MAX_RE)
    a_im = a_im.astype(F32)
    dt = jnp.exp(log_dt.astype(F32))[..., None]
    mag = jnp.exp(dt * a_re)
    abr, abi = mag * jnp.cos(dt * a_im), mag * jnp.sin(dt * a_im)
    den = a_re * a_re + a_im * a_im
    nr, ni = abr - 1.0, abi
    f_re = ((nr * a_re + ni * a_im) / den)[..., None]
    f_im = ((ni * a_re - nr * a_im) / den)[..., None]
    b_re, b_im = b_re.astype(F32)[None], b_im.astype(F32)[None]
    bb_re = f_re * b_re - f_im * b_im
    bb_im = f_re * b_im + f_im * b_re
    pr, pi = [jnp.ones_like(abr)], [jnp.zeros_like(abr)]
    for _ in range(t):
        pr.append(pr[-1] * abr - pi[-1] * abi)
        pi.append(pr[-2] * abi + pi[-1] * abr)
    pr, pi = jnp.stack(pr), jnp.stack(pi)
    c_re, c_im = c_re.astype(F32), c_im.astype(F32)
    l_re = c_re[None, None] * pr[:, :, :, None, :] - c_im[None, None] * pi[:, :, :, None, :]
    l_im = c_re[None, None] * pi[:, :, :, None, :] + c_im[None, None] * pr[:, :, :, None, :]
    kern = (jnp.einsum("drgcn,rgnk->drgck", l_re[:t], bb_re, precision=_HI)
            - jnp.einsum("drgcn,rgnk->drgck", l_im[:t], bb_im, precision=_HI))
    lag = np.arange(t)[:, None] - np.arange(t)[None, :]
    sel = np.stack([lag[:, :, None] == np.arange(t), -lag[:, :, None] == np.arange(t)])
    mt = jnp.einsum("rtsd,drgck->gsktc", jnp.asarray(sel, BF16), kern.astype(BF16),
                    preferred_element_type=F32).reshape(ngrp, t * cg, t * cg)
    pf_r, pf_i = pr[t - 1 - np.arange(t), 0], pi[t - 1 - np.arange(t), 0]
    pb_r, pb_i = pr[np.arange(t), 1], pi[np.arange(t), 1]

    def drive(p_r, p_i, r):
        g_r = p_r[..., None] * bb_re[r][None] - p_i[..., None] * bb_im[r][None]
        g_i = p_r[..., None] * bb_im[r][None] + p_i[..., None] * bb_re[r][None]
        to = lambda x: x.transpose(1, 0, 3, 2).reshape(ngrp, t * cg, nst)
        return to(g_r), to(g_i)

    gf_r, gf_i = drive(pf_r, pf_i, 0)
    gb_r, gb_i = drive(pb_r, pb_i, 1)
    def read(idx, r):
        rr = l_re[idx, r].transpose(1, 3, 0, 2).reshape(ngrp, nst, t * cg)
        ri = -l_im[idx, r].transpose(1, 3, 0, 2).reshape(ngrp, nst, t * cg)
        return rr, ri

    of_r, of_i = read(np.arange(t) + 1, 0)
    ob_r, ob_i = read(t - np.arange(t), 1)

    eye = jnp.eye(2, dtype=F32)
    g4 = jnp.stack([gf_r, gf_i, gb_r, gb_i], axis=2).reshape(ngrp // 2, 2, t * cg, 4, 1, nst)
    gin = (g4 * eye[None, :, None, None, :, None]).reshape(ngrp // 2, 2 * t * cg, 8 * nst)
    o4 = jnp.stack([of_r, of_i, ob_r, ob_i], axis=1).reshape(ngrp // 2, 2, 4, nst, 1, t * cg)
    o4 = o4.transpose(0, 2, 1, 3, 4, 5)
    gout = (o4 * eye[None, None, :, None, :, None]).reshape(ngrp // 2, 8 * nst, 2 * t * cg)
    a_chunk = jnp.stack([pr[t, 0], pi[t, 0], pr[t, 1], pi[t, 1]]).reshape(4, ngrp * nst)
    return mt.astype(BF16), gin.astype(BF16), gout.astype(BF16), a_chunk


def _s5_kernel(z_ref, mt_ref, gin_ref, gout_ref, ac_ref, skip_ref, y_ref, d_scr, x_scr,
               *, n_lat, n_ctx, npairs, pw):
    nrows = n_lat + n_ctx
    gw = pw // 2
    tw = pw // 4
    for p in range(npairs):
        d = _dot(z_ref[0, :, p * pw:(p + 1) * pw], gin_ref[p])
        for j in range(4):
            d_scr[j, :, p * tw:(p + 1) * tw] = d[:, j * tw:(j + 1) * tw]

    def scan(plane, forward):
        ar = ac_ref[plane:plane + 1, :]
        ai = ac_ref[plane + 1:plane + 2, :]

        def body(i, carry):
            xr, xi = carry
            if forward:
                n = jnp.where(i < n_ctx, n_lat + i, i - n_ctx)
            else:
                n = nrows - 1 - i
            x_scr[plane, pl.ds(n, 1), :] = xr
            x_scr[plane + 1, pl.ds(n, 1), :] = xi
            dr = d_scr[plane, pl.ds(n, 1), :]
            di = d_scr[plane + 1, pl.ds(n, 1), :]
            return ar * xr - ai * xi + dr, ar * xi + ai * xr + di

        zero = jnp.zeros((1, d_scr.shape[2]), F32)
        lax.fori_loop(0, nrows, body, (zero, zero))

    scan(0, True)
    scan(2, False)

    for p in range(npairs):
        xp = jnp.concatenate([x_scr[j, :, p * tw:(p + 1) * tw] for j in range(4)], axis=1).astype(BF16)
        y = _dot(xp, gout_ref[p])
        for g in range(2):
            lo = p * pw + g * gw
            zg = z_ref[0, :, lo:lo + gw]
            yg = y[:, g * gw:(g + 1) * gw] + _dot(zg, mt_ref[2 * p + g]) + skip_ref[:, lo:lo + gw] * zg.astype(F32)
            y_ref[0, :, lo:lo + gw] = yg.astype(y_ref.dtype)


def _s5(zf, tables, layer, skip_flat, n_lat_chunks, n_ctx_chunks):
    bsz, nrows, wide = zf.shape
    mt, gin, gout, a_chunk = tables
    npairs, pw = gin.shape[1:3]
    planes = a_chunk.shape[2]
    return pl.pallas_call(
        functools.partial(_s5_kernel, n_lat=n_lat_chunks, n_ctx=n_ctx_chunks, npairs=npairs, pw=pw),
        grid=(bsz,),
        in_specs=[pl.BlockSpec((1, nrows, wide), lambda b: (b, 0, 0)),
                  _layer_spec(mt, layer), _layer_spec(gin, layer), _layer_spec(gout, layer),
                  _layer_spec(a_chunk, layer), _const_spec(skip_flat.shape)],
        out_specs=pl.BlockSpec((1, nrows, wide), lambda b: (b, 0, 0)),
        out_shape=jax.ShapeDtypeStruct((bsz, nrows, wide), BF16),
        scratch_shapes=[pltpu.VMEM((4, nrows, planes), F32), pltpu.VMEM((4, nrows, planes), F32)],
        compiler_params=pltpu.CompilerParams(
            dimension_semantics=("parallel",), vmem_limit_bytes=VMEM_LIMIT),
        name="s5",
    )(zf, mt, gin, gout, a_chunk, skip_flat)


def _gelu_tanh(x):
    return 0.5 * x * (1.0 + jnp.tanh(math.sqrt(2.0 / math.pi) * (x + 0.044715 * (x * x * x))))


def _merge_kernel(h_ref, yh_ref, yz_ref, gate_ref, mod_ref, whg_ref, wglu_ref, wbr_ref, wout_ref, o_ref, y_scr,
                  *, ngrp, cg):
    d = h_ref.shape[2]
    y_hg = _dot(yh_ref[0], whg_ref[...])
    ys = _gelu_tanh(_chunk_unflatten(yz_ref, y_scr, yz_ref.shape[1], ngrp, cg))
    glu = _sigmoid(_dot(ys.astype(BF16), wglu_ref[...]))
    y_s5 = _dot((ys * glu).astype(BF16), wbr_ref[...])
    gate = gate_ref[0].astype(F32)
    merged = _sigmoid(gate[:, :d]) * y_hg + _sigmoid(gate[:, d:]) * y_s5
    o_ref[0] = h_ref[0] + mod_ref[0, 2:3, :] * _dot(merged.astype(BF16), wout_ref[...])


def _merge(h, yh, yz, p_gate, mods, mod_row_fn, whg, wglu, wbr, wout, layer, ngrp, cg, tok_off_blocks):
    bsz, length, d = h.shape
    nt = length // TOK_TILE
    ct = TOK_TILE // S5_T
    seq = lambda rows, w: pl.BlockSpec((1, rows, w), lambda b, t: (b, t + tok_off_blocks, 0))
    return pl.pallas_call(
        functools.partial(_merge_kernel, ngrp=ngrp, cg=cg),
        grid=(bsz, nt),
        in_specs=[pl.BlockSpec((1, TOK_TILE, d), lambda b, t: (b, t, 0)),
                  seq(TOK_TILE, yh.shape[2]), seq(ct, yz.shape[2]), seq(TOK_TILE, p_gate.shape[2]),
                  pl.BlockSpec((1, 6, d), lambda b, t: (mod_row_fn(b), 0, 0)),
                  _layer_spec(whg, layer), _layer_spec(wglu, layer), _layer_spec(wbr, layer),
                  _layer_spec(wout, layer)],
        out_specs=pl.BlockSpec((1, TOK_TILE, d), lambda b, t: (b, t, 0)),
        out_shape=jax.ShapeDtypeStruct(h.shape, F32),
        scratch_shapes=[pltpu.VMEM((ngrp * cg // 128, TOK_TILE, 128), F32)],
        input_output_aliases={0: 0},
        compiler_params=pltpu.CompilerParams(
            dimension_semantics=("parallel", "parallel"), vmem_limit_bytes=VMEM_LIMIT),
        name="merge",
    )(h, yh, yz, p_gate, mods, whg, wglu, wbr, wout)


def _ffn_kernel(h_ref, mod_ref, g_ref, wab_ref, cw_ref, cb_ref, wd_ref, fg_ref,
                o_ref, xn_scr, up_scr, act_scr, *, n_tok, width, vertical, cw, mrows, final):
    j = pl.program_id(1)
    nj = pl.num_programs(1)
    nblk = n_tok // mrows
    rpb = mrows // width
    pad = width

    @pl.when(j == 0)
    def _():
        def norm_body(i, carry):
            r0 = pl.multiple_of(i * mrows, mrows)
            xn_scr[pl.ds(r0, mrows), :] = _modulated_norm(
                h_ref[0, pl.ds(r0, mrows), :], g_ref[...], mod_ref[0, 3:4, :], mod_ref[0, 4:5, :]).astype(BF16)
            o_ref[0, pl.ds(r0, mrows), :] = jnp.zeros((mrows, o_ref.shape[2]), F32)
            return carry

        lax.fori_loop(0, nblk, norm_body, 0)
        xn_scr[n_tok:n_tok + width, :] = jnp.zeros((width, xn_scr.shape[1]), BF16)
        for copy in range(3):
            up_scr[copy, 0:pad, :] = jnp.zeros((pad, 2 * cw), F32)
            up_scr[copy, pad + n_tok:pad + n_tok + width, :] = jnp.zeros((width, 2 * cw), F32)

    def up_rows(row0, m):
        up = _dot(xn_scr[pl.ds(row0, m), :], wab_ref[...])
        col = lax.broadcasted_iota(jnp.int32, (m, 1), 0) % width
        up_scr[0, pl.ds(pad + row0, m), :] = jnp.where(col == 0, 0.0, pltpu.roll(up, 1, axis=0))
        up_scr[1, pl.ds(pad + row0, m), :] = up
        up_scr[2, pl.ds(pad + row0, m), :] = jnp.where(col == width - 1, 0.0, pltpu.roll(up, m - 1, axis=0))

    def conv_row(row0, lane0):
        acc = cb_ref[:, lane0:lane0 + 128]
        for dr in ((-1, 0, 1) if vertical else (0,)):
            for dc in (-1, 0, 1):
                tap = (dr + 1) * 3 + (dc + 1)
                src = up_scr[dc + 1, pl.ds(pl.multiple_of(pad + row0 + dr * width, 8), width), lane0:lane0 + 128]
                acc = acc + cw_ref[tap:tap + 1, lane0:lane0 + 128] * src
        return acc

    def conv_block(blk):
        r0 = blk * mrows
        for lt in range(cw // 128):
            for r in range(rpb):
                row0 = r0 + r * width
                ca = conv_row(row0, lt * 128)
                cb = conv_row(row0, cw + lt * 128)
                act_scr[pl.ds(pl.multiple_of(row0, 16), width), lt * 128:(lt + 1) * 128] = (
                    ca * _sigmoid(ca) * cb).astype(BF16)

    def down_rows(row0):
        o_ref[0, pl.ds(row0, mrows), :] += _dot(act_scr[pl.ds(row0, mrows), :], wd_ref[...])

    if nblk == 1:
        up_rows(0, mrows)
        conv_block(0)
        down_rows(0)
    else:
        streams = math.gcd(FFN_STREAMS, nblk)
        per = nblk // streams
        ahead = width if vertical else 0
        if vertical:
            up_rows(0, width)
            for s in range(1, streams):
                up_rows(s * per * mrows - width, 2 * width)

        def body(k, carry):
            for s in range(streams):
                r0 = pl.multiple_of((k + s * per) * mrows, mrows)
                up_rows(r0 + ahead, mrows)
            for s in range(streams):
                conv_block(k + s * per)
            for s in range(streams):
                down_rows(pl.multiple_of((k + s * per) * mrows, mrows))
            return carry

        lax.fori_loop(0, per, body, 0)

    @pl.when(j == nj - 1)
    def _():
        def fin_body(i, carry):
            r0 = pl.multiple_of(i * mrows, mrows)
            out = h_ref[0, pl.ds(r0, mrows), :] + mod_ref[0, 5:6, :] * o_ref[0, pl.ds(r0, mrows), :]
            if final:
                out = out * lax.rsqrt(jnp.mean(out * out, axis=-1, keepdims=True) + RMS_EPS) * fg_ref[...]
            o_ref[0, pl.ds(r0, mrows), :] = out
            return carry

        lax.fori_loop(0, nblk, fin_body, 0)


def _ffn(h, mods, mod_row_fn, norm_g, ffn_w, layer, final_g, width, vertical, final):
    bsz, n_tok, d = h.shape
    wab, cwf, cbf, wd = ffn_w
    cw = FFN_TILE
    nj = wd.shape[1] // cw
    return pl.pallas_call(
        functools.partial(_ffn_kernel, n_tok=n_tok, width=width, vertical=vertical, cw=cw, mrows=TOK_TILE,
                          final=final),
        grid=(bsz, nj),
        in_specs=[pl.BlockSpec((1, n_tok, d), lambda b, j: (b, 0, 0), pipeline_mode=pl.Buffered(1)),
                  pl.BlockSpec((1, 6, d), lambda b, j: (mod_row_fn(b), 0, 0)),
                  pl.BlockSpec((1, d), lambda b, j: (0, 0)),
                  pl.BlockSpec((None, d, 2 * cw), lambda b, j: (layer, 0, j)),
                  pl.BlockSpec((None, 9, 2 * cw), lambda b, j: (layer, 0, j)),
                  pl.BlockSpec((None, 1, 2 * cw), lambda b, j: (layer, 0, j)),
                  pl.BlockSpec((None, cw, d), lambda b, j: (layer, j, 0)),
                  pl.BlockSpec((1, d), lambda b, j: (0, 0))],
        out_specs=pl.BlockSpec((1, n_tok, d), lambda b, j: (b, 0, 0)),
        out_shape=jax.ShapeDtypeStruct(h.shape, F32),
        scratch_shapes=[pltpu.VMEM((n_tok + width, d), BF16),
                        pltpu.VMEM((3, n_tok + 2 * width, 2 * cw), F32),
                        pltpu.VMEM((n_tok, cw), BF16)],
        input_output_aliases={0: 0},
        compiler_params=pltpu.CompilerParams(
            dimension_semantics=("parallel", "arbitrary"), vmem_limit_bytes=VMEM_LIMIT),
        name="ffn",
    )(h, mods, norm_g.reshape(1, d), wab, cwf, cbf, wd, final_g.reshape(1, d))


def _ffn_weights(w_up, conv_w, conv_b, w_down):
    depth, ffn, _ = w_down.shape
    cw = FFN_TILE
    fp = ((ffn + cw - 1) // cw) * cw
    nj = fp // cw

    def tiles(x):
        lead = x.shape[:-1]
        halves = []
        for part in (x[..., :ffn], x[..., ffn:]):
            part = jnp.pad(part, [(0, 0)] * len(lead) + [(0, fp - ffn)])
            halves.append(part.reshape(lead + (nj, cw)))
        return jnp.concatenate(halves, axis=-1).reshape(lead + (nj * 2 * cw,))

    wd = jnp.pad(w_down.astype(BF16), [(0, 0), (0, fp - ffn), (0, 0)])
    return (tiles(w_up.astype(BF16)), tiles(conv_w.reshape(depth, 9, 2 * ffn)),
            tiles(conv_b.reshape(depth, 1, 2 * ffn)), wd)


def _lower_bounds(lb_raw):
    p = jax.nn.softmax(lb_raw.astype(F32), axis=0)
    cs = jnp.cumsum(p, axis=0)
    return cs - cs[0:1]


def kernel(x, c, ctx, c_ctx, ada_w, ada_b, norm1_g, w_in, hg_lb_raw, hg_norm_g, w_hg_br, s5_a_re, s5_a_im, s5_log_dt, s5_b_re, s5_b_im, s5_c_re, s5_c_im, s5_d, w_s5_glu, w_s5_br, w_out, norm2_g, w_up, ffn_conv_w, ffn_conv_b, w_down, final_g):
    bsz, n_lat, d = x.shape
    n_ctx = ctx.shape[1]
    depth = ada_w.shape[0]
    dk = hg_norm_g.shape[1]
    hg_w = hg_lb_raw.shape[2]
    ngrp, nst, cg = s5_b_re.shape[1:]
    assert n_lat % TOK_TILE == 0 and n_ctx % TOK_TILE == 0 and n_lat % GLA_C == 0 and n_ctx % GLA_C == 0
    assert 5 * hg_w + ngrp * cg + 2 * d == w_in.shape[2] and S5_T * cg == 256 and 128 % cg == 0
    assert ngrp % (128 // cg) == 0

    rows = ((bsz + 1 + 7) // 8) * 8
    cond = jnp.zeros((rows, d), F32).at[:bsz].set(c).at[bsz].set(c_ctx)
    mods = _ada_mod(cond, ada_w, ada_b).reshape(depth, rows, 6, d)
    lat_row = lambda b: b
    ctx_row = lambda b: bsz
    lbs = _lower_bounds(hg_lb_raw)
    lat_blk = n_lat // TOK_TILE

    w_in_b = _to_bf16(w_in)
    whg_b, wglu_b, wbr_b, wout_b = (w.astype(BF16) for w in (w_hg_br, w_s5_glu, w_s5_br, w_out))
    ffn_w = _ffn_weights(_to_bf16(w_up), ffn_conv_w, ffn_conv_b, _to_bf16(w_down))
    s5_tabs = jax.vmap(_s5_tables)(s5_a_re, s5_a_im, s5_log_dt, s5_b_re, s5_b_im, s5_c_re, s5_c_im)
    skip = jnp.broadcast_to(s5_d.astype(F32).reshape(depth, ngrp, 1, cg), (depth, ngrp, S5_T, cg))
    skip = skip.reshape(depth, 1, ngrp * S5_T * cg)

    h_lat, h_ctx = x, ctx
    for l in range(depth):
        last = l == depth - 1
        p_hg, z, p_gate = _proj_in(h_lat, h_ctx, mods[l], norm1_g[l], w_in_b, l, 5 * hg_w, ngrp, cg)

        o_fwd = _gla(p_hg, lbs[l, 0], hg_norm_g[l], None, n_lat, n_ctx, hg_w, dk, rev=False, ctx_out=not last)
        yh = _gla(p_hg, lbs[l, 1], hg_norm_g[l], o_fwd, n_lat, n_ctx, hg_w, dk, rev=True, ctx_out=not last)
        yz = _s5(z, s5_tabs, l, skip[l], n_lat // S5_T, n_ctx // S5_T)

        wts = (whg_b, wglu_b, wbr_b, wout_b, l, ngrp, cg)
        h_lat = _merge(h_lat, yh, yz, p_gate, mods[l], lat_row, *wts, 0)
        h_lat = _ffn(h_lat, mods[l], lat_row, norm2_g[l], ffn_w, l, final_g, GRID_W, True, last)
        if not last:
            h_ctx = _merge(h_ctx, yh, yz, p_gate, mods[l], ctx_row, *wts, lat_blk)
            stack = math.gcd(bsz, max(1, n_lat // n_ctx))
            h_ctx = _ffn(h_ctx.reshape(bsz // stack, stack * n_ctx, d), mods[l], ctx_row, norm2_g[l], ffn_w, l,
                         final_g, n_ctx, False, False).reshape(bsz, n_ctx, d)
    return h_lat
```

```python
import functools
import math

import numpy as np
import jax
import jax.numpy as jnp
from jax import lax
from jax.experimental import pallas as pl
from jax.experimental.pallas import tpu as pltpu

GRID_W = 64
RMS_EPS = 1e-6
S5_MAX_RE = -1e-4
LOG2_E = 1.4426950408889634
S5_T = 16
GLA_C = 128
HEADS_PER_ITER = 8
TOK_TILE = 256
FFN_TILE = 256
FFN_STREAMS = 4
VMEM_LIMIT = 56 * 1024 * 1024

F32 = jnp.float32
BF16 = jnp.bfloat16
_HI = lax.Precision.HIGHEST


def _nt_dot(a, b):
    return lax.dot_general(a, b, (((1,), (1,)), ((), ())), preferred_element_type=F32)


def _tn_dot(a, b):
    return lax.dot_general(a, b, (((0,), (0,)), ((), ())), preferred_element_type=F32)


def _dot(a, b):
    return jnp.dot(a, b, preferred_element_type=F32)


def _sigmoid(x):
    return 1.0 / (1.0 + jnp.exp(-x))


def _const_spec(shape):
    nd = len(shape)
    return pl.BlockSpec(shape, lambda *_: (0,) * nd, pipeline_mode=pl.Buffered(1))


def _layer_spec(stacked, layer):
    rest = stacked.shape[1:]
    return pl.BlockSpec((None,) + rest, lambda *_: (layer,) + (0,) * len(rest), pipeline_mode=pl.Buffered(1))


def _ada_kernel(cond_ref, w_ref, b_ref, o_ref):
    cnd = cond_ref[...]
    s = cnd * _sigmoid(cnd)
    o_ref[0] = jnp.dot(s, w_ref[0], preferred_element_type=F32, precision=_HI) + b_ref[0]


def _ada_mod(cond, ada_w, ada_b):
    depth, d, n6 = ada_w.shape
    rows = cond.shape[0]
    tn = 512
    return pl.pallas_call(
        _ada_kernel,
        grid=(depth, n6 // tn),
        in_specs=[
            pl.BlockSpec((rows, d), lambda l, j: (0, 0)),
            pl.BlockSpec((1, d, tn), lambda l, j: (l, 0, j)),
            pl.BlockSpec((1, 1, tn), lambda l, j: (l, 0, j)),
        ],
        out_specs=pl.BlockSpec((1, rows, tn), lambda l, j: (l, 0, j)),
        out_shape=jax.ShapeDtypeStruct((depth, rows, n6), F32),
        name="ada_mod",
    )(cond, ada_w, ada_b.reshape(depth, 1, n6))


def _modulated_norm(x, g, shift, scale):
    ms = jnp.mean(x * x, axis=-1, keepdims=True)
    y = x * lax.rsqrt(ms + RMS_EPS) * g
    return y * (1.0 + scale) + shift


def _strip_masks(n_rows, cg):
    lane = lax.broadcasted_iota(jnp.int32, (n_rows, 128), 1)
    return [(lane >= j * cg) & (lane < (j + 1) * cg) for j in range(128 // cg)]


def _chunk_flatten(u_scr, z_ref, n_chunks, ngrp, cg):
    gpt = 128 // cg
    masks = _strip_masks(n_chunks, cg)
    for g in range(ngrp):
        k, i = divmod(g, gpt)
        for hf in range(S5_T // gpt):
            acc = None
            for j in range(gpt):
                src = u_scr[k, pl.ds(hf * gpt + j, n_chunks, stride=S5_T), :]
                shift = ((j - i) * cg) % 128
                if shift:
                    src = pltpu.roll(src, shift, axis=1)
                acc = src if acc is None else jnp.where(masks[j], src, acc)
            tile = g * (S5_T // gpt) + hf
            z_ref[0, :, tile * 128:(tile + 1) * 128] = acc.astype(z_ref.dtype)


def _chunk_unflatten(yz_ref, scr, n_chunks, ngrp, cg):
    gpt = 128 // cg
    masks = _strip_masks(n_chunks, cg)
    for k in range(ngrp // gpt):
        for hf in range(S5_T // gpt):
            srcs = []
            for i in range(gpt):
                tile = (k * gpt + i) * (S5_T // gpt) + hf
                srcs.append(yz_ref[0, :, tile * 128:(tile + 1) * 128].astype(F32))
            for j in range(gpt):
                acc = None
                for i in range(gpt):
                    shift = ((i - j) * cg) % 128
                    src = pltpu.roll(srcs[i], shift, axis=1) if shift else srcs[i]
                    acc = src if acc is None else jnp.where(masks[i], src, acc)
                t = hf * gpt + j
                scr[k, t * n_chunks:(t + 1) * n_chunks, :] = acc
    return jnp.concatenate(
        [jnp.concatenate([scr[k, pl.ds(ch, S5_T, stride=n_chunks), :] for k in range(scr.shape[0])], axis=1)
         for ch in range(n_chunks)], axis=0)


def _proj_in_kernel(hl_ref, hc_ref, mod_ref, g_ref, w_ref, hg_ref, z_ref, gate_ref, xn_scr, u_scr,
                    *, nt_lat, ngrp, cg):
    def norm(h_ref):
        xn_scr[...] = _modulated_norm(h_ref[0], g_ref[...], mod_ref[0, 0:1, :], mod_ref[0, 1:2, :]).astype(BF16)

    pl.when(pl.program_id(1) < nt_lat)(lambda: norm(hl_ref))
    pl.when(pl.program_id(1) >= nt_lat)(lambda: norm(hc_ref))
    xn = xn_scr[...]
    hg_w, s5_w = hg_ref.shape[2], ngrp * cg

    def project(o_ref, col0):
        width = o_ref.shape[2]
        step = math.gcd(512, width)
        for c in range(0, width, step):
            o_ref[0, :, c:c + step] = _dot(xn, w_ref[:, col0 + c:col0 + c + step]).astype(o_ref.dtype)

    u = _dot(xn, w_ref[:, hg_w:hg_w + s5_w])
    for k in range(s5_w // 128):
        u_scr[k] = u[:, k * 128:(k + 1) * 128]
    _chunk_flatten(u_scr, z_ref, xn_scr.shape[0] // S5_T, ngrp, cg)
    project(hg_ref, 0)
    project(gate_ref, hg_w + s5_w)


def _proj_in(h_lat, h_ctx, mods, norm_g, w_stack, layer, hg_cols, ngrp, cg):
    bsz, n_lat, d = h_lat.shape
    n_ctx = h_ctx.shape[1]
    total = n_lat + n_ctx
    nt_lat, nt_ctx = n_lat // TOK_TILE, n_ctx // TOK_TILE
    s5_w = ngrp * cg
    gate_cols = w_stack.shape[2] - hg_cols - s5_w
    ct = TOK_TILE // S5_T
    return pl.pallas_call(
        functools.partial(_proj_in_kernel, nt_lat=nt_lat, ngrp=ngrp, cg=cg),
        grid=(bsz, nt_lat + nt_ctx),
        in_specs=[
            pl.BlockSpec((1, TOK_TILE, d), lambda b, t: (b, jnp.minimum(t, nt_lat - 1), 0)),
            pl.BlockSpec((1, TOK_TILE, d), lambda b, t: (b, jnp.maximum(t - nt_lat, 0), 0)),
            pl.BlockSpec((1, 6, d), lambda b, t: (jnp.where(t < nt_lat, b, bsz), 0, 0)),
            pl.BlockSpec((1, d), lambda b, t: (0, 0)),
            _layer_spec(w_stack, layer),
        ],
        out_specs=[pl.BlockSpec((1, TOK_TILE, hg_cols), lambda b, t: (b, t, 0)),
                   pl.BlockSpec((1, ct, s5_w * S5_T), lambda b, t: (b, t, 0)),
                   pl.BlockSpec((1, TOK_TILE, gate_cols), lambda b, t: (b, t, 0))],
        out_shape=[jax.ShapeDtypeStruct((bsz, total, hg_cols), BF16),
                   jax.ShapeDtypeStruct((bsz, total // S5_T, s5_w * S5_T), BF16),
                   jax.ShapeDtypeStruct((bsz, total, gate_cols), BF16)],
        scratch_shapes=[pltpu.VMEM((TOK_TILE, d), BF16), pltpu.VMEM((s5_w // 128, TOK_TILE, 128), F32)],
        compiler_params=pltpu.CompilerParams(
            dimension_semantics=("parallel", "arbitrary"), vmem_limit_bytes=VMEM_LIMIT),
        name="proj_in",
    )(h_lat, h_ctx, mods, norm_g.reshape(1, d), w_stack)


def _gla_tables(c, rev):
    t = np.arange(c)[:, None]
    s = np.arange(c)[None, :]
    x = t ^ s
    lev = np.full((c, c), -2, np.int32)
    causal = (t < s) if rev else (t > s)
    with np.errstate(divide="ignore"):
        hb = np.floor(np.log2(np.maximum(x, 1))).astype(np.int32)
    lev = np.where(causal, hb, lev)
    lev = np.where(t == s, -1, lev).astype(np.int32)
    tri = ((t <= s) if rev else (t >= s)).astype(np.float32)
    return jnp.asarray(lev), jnp.asarray(tri, dtype=BF16)


def _gla_kernel(*refs, rev, readout, c, nheads, dk, state_only_steps):
    if readout:
        (q_ref, fz_ref, i_ref, lb_ref, lev_ref, tri_ref, g_ref, of_ref, ng_ref,
         out_ref, st_ref, b_scr_all) = refs
    else:
        q_ref, fz_ref, i_ref, lb_ref, lev_ref, tri_ref, out_ref, st_ref, b_scr_all = refs
    nlev = int(math.log2(c))

    @pl.when(pl.program_id(1) == 0)
    def _():
        st_ref[...] = jnp.zeros_like(st_ref)

    lev = lev_ref[...]
    tri = tri_ref[...]
    row = lax.broadcasted_iota(jnp.int32, (c, 1), 0)

    def head(hd, b_scr, full):
        sl = pl.ds(pl.multiple_of(hd * dk, dk), dk)
        z = fz_ref[0, :, sl].astype(F32)
        v = i_ref[0, :, sl]
        lb = lb_ref[0:1, sl]

        e = jnp.exp(-jnp.abs(z))
        den = 1.0 + e
        inv = 1.0 / den
        pos = z >= 0
        sig_pos = jnp.where(pos, 1.0, e) * inv
        sig_neg = jnp.where(pos, e, 1.0) * inv
        f = lb + (1.0 - lb) * sig_pos
        a = jnp.where(lb > 0, jnp.log(f), jnp.minimum(z, 0.0) - jnp.log(den)) * LOG2_E
        k = (1.0 - lb) * sig_neg

        a_hi = a.astype(BF16)
        a_lo = (a - a_hi.astype(F32)).astype(BF16)
        bb = _dot(tri, jnp.concatenate([a_hi, a_lo], axis=1))
        b = bb[:, :dk] + bb[:, dk:]
        b_scr[...] = b

        st = st_ref[hd]
        b_end = b_scr[pl.ds(0 if rev else c - 1, 1), :]
        k_out = (k * jnp.exp2(b_end - b)).astype(BF16)
        st_ref[hd] = jnp.exp2(b_end) * st + _tn_dot(v, k_out)
        if not full:
            out_ref[0, :, sl] = jnp.zeros((c, dk), out_ref.dtype)
            return
        qz = q_ref[0, :, sl].astype(F32)
        q = qz * _sigmoid(qz)

        odd = (row & 1) == 1
        f_prev = pltpu.roll(f, 1, axis=0)
        f_next = pltpu.roll(f, c - 1, axis=0)
        qf = q * f
        scores = jnp.zeros((c, c), F32)
        for level in range(nlev):
            half = 1 << level
            if level == 0:
                xq, xk = qf, k
            elif level == 1:
                if rev:
                    xq, xk = qf * jnp.where(odd, 1.0, f_next), k * jnp.where(odd, f_prev, 1.0)
                else:
                    xq, xk = qf * jnp.where(odd, f_prev, 1.0), k * jnp.where(odd, 1.0, f_next)
            else:
                blk = 2 * half
                pieces = []
                for j in range(c // blk):
                    brow = b_scr[pl.ds(j * blk + (half if rev else half - 1), 1), :]
                    pieces.append(jnp.broadcast_to(brow, (blk, dk)))
                ref_b = jnp.concatenate(pieces, axis=0) if len(pieces) > 1 else pieces[0]
                decay = jnp.exp2(-jnp.abs(b - ref_b))
                xq, xk = q * decay, k * decay
            scores = jnp.where(lev == level, _nt_dot(xq.astype(BF16), xk.astype(BF16)), scores)
        scores = jnp.where(lev == -1, jnp.sum(q * k, axis=-1, keepdims=True), scores)
        o = _dot(scores.astype(BF16), v) + _nt_dot((q * jnp.exp2(b)).astype(BF16), st.astype(BF16))

        if readout:
            o = o + of_ref[0, :, sl].astype(F32)
            o = o * lax.rsqrt(jnp.mean(o * o, axis=-1, keepdims=True) + RMS_EPS) * ng_ref[...]
            gz = g_ref[0, :, sl].astype(F32)
            o = o * (gz * _sigmoid(gz))
        out_ref[0, :, sl] = o.astype(out_ref.dtype)

    per_iter = b_scr_all.shape[0]

    def all_heads(full):
        def head_group(i, carry):
            for slot in range(per_iter):
                head(i * per_iter + slot, b_scr_all.at[slot], full)
            return carry

        lax.fori_loop(0, nheads // per_iter, head_group, 0)

    if state_only_steps:
        pl.when(pl.program_id(1) < state_only_steps)(lambda: all_heads(False))
        pl.when(pl.program_id(1) >= state_only_steps)(lambda: all_heads(True))
    else:
        all_heads(True)


def _gla(p_hg, lb_row, norm_g, o_fwd, n_lat, n_ctx, width, dk, rev, ctx_out):
    bsz, total, _ = p_hg.shape
    c = GLA_C
    nl, nc = n_lat // c, n_ctx // c
    nheads = width // dk
    readout = rev

    if rev:
        def chunk(i):
            return jnp.where(i < nc, nl + nc - 1 - i, nl + nc - 1 - i)
    else:
        def chunk(i):
            return jnp.where(i < nc, nl + i, i - nc)

    def col_spec(colblk):
        return pl.BlockSpec((1, c, width), lambda b, i: (b, chunk(i), colblk))

    lev, tri = _gla_tables(c, rev)
    in_specs = [col_spec(0), col_spec(2 if rev else 1), col_spec(3),
                pl.BlockSpec((1, width), lambda b, i: (0, 0)),
                pl.BlockSpec((c, c), lambda b, i: (0, 0)),
                pl.BlockSpec((c, c), lambda b, i: (0, 0))]
    args = [p_hg, p_hg, p_hg, lb_row.reshape(1, width), lev, tri]
    if readout:
        in_specs += [col_spec(4),
                     pl.BlockSpec((1, c, width), lambda b, i: (b, chunk(i), 0)),
                     pl.BlockSpec((1, dk), lambda b, i: (0, 0))]
        args += [p_hg, o_fwd, norm_g.reshape(1, dk)]
    return pl.pallas_call(
        functools.partial(_gla_kernel, rev=rev, readout=readout, c=c, nheads=nheads, dk=dk,
                          state_only_steps=0 if ctx_out else nc),
        grid=(bsz, nl + nc),
        in_specs=in_specs,
        out_specs=pl.BlockSpec((1, c, width), lambda b, i: (b, chunk(i), 0)),
        out_shape=jax.ShapeDtypeStruct((bsz, total, width), BF16),
        scratch_shapes=[pltpu.VMEM((nheads, dk, dk), F32), pltpu.VMEM((math.gcd(HEADS_PER_ITER, nheads), c, dk), F32)],
        compiler_params=pltpu.CompilerParams(
            dimension_semantics=("parallel", "arbitrary"), vmem_limit_bytes=VMEM_LIMIT),
        name="gla_bwd" if rev else "gla_fwd",
    )(*args)


def _s5_tables(a_re, a_im, log_dt, b_re, b_im, c_re, c_im):
    t = S5_T
    ngrp, nst, cg = b_re.shape
    a_re = jnp.minimum(a_re.astype(F32), S5_MAX_RE)
    a_im = a_im.astype(F32)
    dt = jnp.exp(log_dt.astype(F32))[..., None]
    mag = jnp.exp(dt * a_re)
    abr, abi = mag * jnp.cos(dt * a_im), mag * jnp.sin(dt * a_im)
    den = a_re * a_re + a_im * a_im
    nr, ni = abr - 1.0, abi
    f_re = ((nr * a_re + ni * a_im) / den)[..., None]
    f_im = ((ni * a_re - nr * a_im) / den)[..., None]
    b_re, b_im = b_re.astype(F32)[None], b_im.astype(F32)[None]
    bb_re = f_re * b_re - f_im * b_im
    bb_im = f_re * b_im + f_im * b_re
    pr, pi = [jnp.ones_like(abr)], [jnp.zeros_like(abr)]
    for _ in range(t):
        pr.append(pr[-1] * abr - pi[-1] * abi)
        pi.append(pr[-2] * abi + pi[-1] * abr)
    pr, pi = jnp.stack(pr), jnp.stack(pi)
    c_re, c_im = c_re.astype(F32), c_im.astype(F32)
    l_re = c_re[None, None] * pr[:, :, :, None, :] - c_im[None, None] * pi[:, :, :, None, :]
    l_im = c_re[None, None] * pi[:, :, :, None, :] + c_im[None, None] * pr[:, :, :, None, :]
    kern = (jnp.einsum("drgcn,rgnk->drgck", l_re[:t], bb_re, precision=_HI)
            - jnp.einsum("drgcn,rgnk->drgck", l_im[:t], bb_im, precision=_HI))
    lag = np.arange(t)[:, None] - np.arange(t)[None, :]
    sel = np.stack([lag[:, :, None] == np.arange(t), -lag[:, :, None] == np.arange(t)])
    mt = jnp.einsum("rtsd,drgck->gsktc", jnp.asarray(sel, BF16), kern.astype(BF16),
                    preferred_element_type=F32).astype(BF16).reshape(ngrp, t * cg, t * cg)
    pf_r, pf_i = pr[t - 1 - np.arange(t), 0], pi[t - 1 - np.arange(t), 0]
    pb_r, pb_i = pr[np.arange(t), 1], pi[np.arange(t), 1]

    def drive(p_r, p_i, r):
        g_r = p_r[..., None] * bb_re[r][None] - p_i[..., None] * bb_im[r][None]
        g_i = p_r[..., None] * bb_im[r][None] + p_i[..., None] * bb_re[r][None]
        to = lambda x: x.transpose(1, 0, 3, 2).reshape(ngrp, t * cg, nst)
        return to(g_r), to(g_i)

    gf_r, gf_i = drive(pf_r, pf_i, 0)
    gb_r, gb_i = drive(pb_r, pb_i, 1)
    def read(idx, r):
        rr = l_re[idx, r].transpose(1, 3, 0, 2).reshape(ngrp, nst, t * cg)
        ri = -l_im[idx, r].transpose(1, 3, 0, 2).reshape(ngrp, nst, t * cg)
        return rr, ri

    of_r, of_i = read(np.arange(t) + 1, 0)
    ob_r, ob_i = read(t - np.arange(t), 1)

    eye = jnp.eye(2, dtype=BF16)
    g4 = jnp.stack([gf_r, gf_i, gb_r, gb_i], axis=2).astype(BF16).reshape(ngrp // 2, 2, t * cg, 4, 1, nst)
    gin = (g4 * eye[None, :, None, None, :, None]).reshape(ngrp // 2, 2 * t * cg, 8 * nst)
    o4 = jnp.stack([of_r, of_i, ob_r, ob_i], axis=1).astype(BF16).reshape(ngrp // 2, 2, 4, nst, 1, t * cg)
    o4 = o4.transpose(0, 2, 1, 3, 4, 5)
    gout = (o4 * eye[None, None, :, None, :, None]).reshape(ngrp // 2, 8 * nst, 2 * t * cg)
    a_chunk = jnp.stack([pr[t, 0], pi[t, 0], pr[t, 1], pi[t, 1]]).reshape(4, ngrp * nst)
    return mt, gin, gout, a_chunk


def _s5_kernel(z_ref, mt_ref, gin_ref, gout_ref, ac_ref, skip_ref, y_ref, d_scr, x_scr,
               *, n_lat, n_ctx, npairs, pw):
    nrows = n_lat + n_ctx
    gw = pw // 2
    tw = pw // 4
    for p in range(npairs):
        d = _dot(z_ref[0, :, p * pw:(p + 1) * pw], gin_ref[p])
        for j in range(4):
            d_scr[j, :, p * tw:(p + 1) * tw] = d[:, j * tw:(j + 1) * tw]

    def scan(plane, forward):
        ar = ac_ref[plane:plane + 1, :]
        ai = ac_ref[plane + 1:plane + 2, :]

        def body(i, carry):
            xr, xi = carry
            if forward:
                n = jnp.where(i < n_ctx, n_lat + i, i - n_ctx)
            else:
                n = nrows - 1 - i
            x_scr[plane, pl.ds(n, 1), :] = xr
            x_scr[plane + 1, pl.ds(n, 1), :] = xi
            dr = d_scr[plane, pl.ds(n, 1), :]
            di = d_scr[plane + 1, pl.ds(n, 1), :]
            return ar * xr - ai * xi + dr, ar * xi + ai * xr + di

        zero = jnp.zeros((1, d_scr.shape[2]), F32)
        lax.fori_loop(0, nrows, body, (zero, zero))

    scan(0, True)
    scan(2, False)

    for p in range(npairs):
        xp = jnp.concatenate([x_scr[j, :, p * tw:(p + 1) * tw] for j in range(4)], axis=1).astype(BF16)
        y = _dot(xp, gout_ref[p])
        for g in range(2):
            lo = p * pw + g * gw
            zg = z_ref[0, :, lo:lo + gw]
            yg = y[:, g * gw:(g + 1) * gw] + _dot(zg, mt_ref[2 * p + g]) + skip_ref[:, lo:lo + gw] * zg.astype(F32)
            y_ref[0, :, lo:lo + gw] = yg.astype(y_ref.dtype)


def _s5(zf, tables, layer, skip_flat, n_lat_chunks, n_ctx_chunks):
    bsz, nrows, wide = zf.shape
    mt, gin, gout, a_chunk = tables
    npairs, pw = gin.shape[1:3]
    planes = a_chunk.shape[2]
    return pl.pallas_call(
        functools.partial(_s5_kernel, n_lat=n_lat_chunks, n_ctx=n_ctx_chunks, npairs=npairs, pw=pw),
        grid=(bsz,),
        in_specs=[pl.BlockSpec((1, nrows, wide), lambda b: (b, 0, 0)),
                  _layer_spec(mt, layer), _layer_spec(gin, layer), _layer_spec(gout, layer),
                  _layer_spec(a_chunk, layer), _const_spec(skip_flat.shape)],
        out_specs=pl.BlockSpec((1, nrows, wide), lambda b: (b, 0, 0)),
        out_shape=jax.ShapeDtypeStruct((bsz, nrows, wide), BF16),
        scratch_shapes=[pltpu.VMEM((4, nrows, planes), F32), pltpu.VMEM((4, nrows, planes), F32)],
        compiler_params=pltpu.CompilerParams(
            dimension_semantics=("parallel",), vmem_limit_bytes=VMEM_LIMIT),
        name="s5",
    )(zf, mt, gin, gout, a_chunk, skip_flat)


def _gelu_tanh(x):
    return 0.5 * x * (1.0 + jnp.tanh(math.sqrt(2.0 / math.pi) * (x + 0.044715 * (x * x * x))))


def _merge_kernel(h_ref, yh_ref, yz_ref, gate_ref, mod_ref, whg_ref, wglu_ref, wbr_ref, wout_ref, o_ref, y_scr,
                  *, ngrp, cg):
    d = h_ref.shape[2]
    y_hg = _dot(yh_ref[0], whg_ref[...])
    ys = _gelu_tanh(_chunk_unflatten(yz_ref, y_scr, yz_ref.shape[1], ngrp, cg))
    glu = _sigmoid(_dot(ys.astype(BF16), wglu_ref[...]))
    y_s5 = _dot((ys * glu).astype(BF16), wbr_ref[...])
    gate = gate_ref[0].astype(F32)
    merged = _sigmoid(gate[:, :d]) * y_hg + _sigmoid(gate[:, d:]) * y_s5
    o_ref[0] = h_ref[0] + mod_ref[0, 2:3, :] * _dot(merged.astype(BF16), wout_ref[...])


def _merge(h, yh, yz, p_gate, mods, mod_row_fn, whg, wglu, wbr, wout, layer, ngrp, cg, tok_off_blocks):
    bsz, length, d = h.shape
    nt = length // TOK_TILE
    ct = TOK_TILE // S5_T
    seq = lambda rows, w: pl.BlockSpec((1, rows, w), lambda b, t: (b, t + tok_off_blocks, 0))
    return pl.pallas_call(
        functools.partial(_merge_kernel, ngrp=ngrp, cg=cg),
        grid=(bsz, nt),
        in_specs=[pl.BlockSpec((1, TOK_TILE, d), lambda b, t: (b, t, 0)),
                  seq(TOK_TILE, yh.shape[2]), seq(ct, yz.shape[2]), seq(TOK_TILE, p_gate.shape[2]),
                  pl.BlockSpec((1, 6, d), lambda b, t: (mod_row_fn(b), 0, 0)),
                  _layer_spec(whg, layer), _layer_spec(wglu, layer), _layer_spec(wbr, layer),
                  _layer_spec(wout, layer)],
        out_specs=pl.BlockSpec((1, TOK_TILE, d), lambda b, t: (b, t, 0)),
        out_shape=jax.ShapeDtypeStruct(h.shape, F32),
        scratch_shapes=[pltpu.VMEM((ngrp * cg // 128, TOK_TILE, 128), F32)],
        input_output_aliases={0: 0},
        compiler_params=pltpu.CompilerParams(
            dimension_semantics=("parallel", "parallel"), vmem_limit_bytes=VMEM_LIMIT),
        name="merge",
    )(h, yh, yz, p_gate, mods, whg, wglu, wbr, wout)


def _ffn_kernel(h_ref, mod_ref, g_ref, wab_ref, cw_ref, cb_ref, wd_ref, fg_ref,
                o_ref, xn_scr, up_scr, act_scr, *, n_tok, width, vertical, cw, mrows, final):
    j = pl.program_id(1)
    nj = pl.num_programs(1)
    nblk = n_tok // mrows
    rpb = mrows // width
    pad = width

    @pl.when(j == 0)
    def _():
        def norm_body(i, carry):
            r0 = pl.multiple_of(i * mrows, mrows)
            xn_scr[pl.ds(r0, mrows), :] = _modulated_norm(
                h_ref[0, pl.ds(r0, mrows), :], g_ref[...], mod_ref[0, 3:4, :], mod_ref[0, 4:5, :]).astype(BF16)
            o_ref[0, pl.ds(r0, mrows), :] = jnp.zeros((mrows, o_ref.shape[2]), F32)
            return carry

        lax.fori_loop(0, nblk, norm_body, 0)
        xn_scr[n_tok:n_tok + width, :] = jnp.zeros((width, xn_scr.shape[1]), BF16)
        for copy in range(3):
            up_scr[copy, 0:pad, :] = jnp.zeros((pad, 2 * cw), F32)
            up_scr[copy, pad + n_tok:pad + n_tok + width, :] = jnp.zeros((width, 2 * cw), F32)

    def up_rows(row0, m):
        up = _dot(xn_scr[pl.ds(row0, m), :], wab_ref[...])
        col = lax.broadcasted_iota(jnp.int32, (m, 1), 0) % width
        up_scr[0, pl.ds(pad + row0, m), :] = jnp.where(col == 0, 0.0, pltpu.roll(up, 1, axis=0))
        up_scr[1, pl.ds(pad + row0, m), :] = up
        up_scr[2, pl.ds(pad + row0, m), :] = jnp.where(col == width - 1, 0.0, pltpu.roll(up, m - 1, axis=0))

    def conv_row(row0, lane0):
        acc = cb_ref[:, lane0:lane0 + 128]
        for dr in ((-1, 0, 1) if vertical else (0,)):
            for dc in (-1, 0, 1):
                tap = (dr + 1) * 3 + (dc + 1)
                src = up_scr[dc + 1, pl.ds(pl.multiple_of(pad + row0 + dr * width, 8), width), lane0:lane0 + 128]
                acc = acc + cw_ref[tap:tap + 1, lane0:lane0 + 128] * src
        return acc

    def conv_block(blk):
        r0 = blk * mrows
        for lt in range(cw // 128):
            for r in range(rpb):
                row0 = r0 + r * width
                ca = conv_row(row0, lt * 128)
                cb = conv_row(row0, cw + lt * 128)
                act_scr[pl.ds(pl.multiple_of(row0, 16), width), lt * 128:(lt + 1) * 128] = (
                    ca * _sigmoid(ca) * cb).astype(BF16)

    def down_rows(row0):
        o_ref[0, pl.ds(row0, mrows), :] += _dot(act_scr[pl.ds(row0, mrows), :], wd_ref[...])

    if nblk == 1:
        up_rows(0, mrows)
        conv_block(0)
        down_rows(0)
    else:
        streams = math.gcd(FFN_STREAMS, nblk)
        per = nblk // streams
        ahead = width if vertical else 0
        if vertical:
            up_rows(0, width)
            for s in range(1, streams):
                up_rows(s * per * mrows - width, 2 * width)

        def body(k, carry):
            for s in range(streams):
                r0 = pl.multiple_of((k + s * per) * mrows, mrows)
                up_rows(r0 + ahead, mrows)
            for s in range(streams):
                conv_block(k + s * per)
            for s in range(streams):
                down_rows(pl.multiple_of((k + s * per) * mrows, mrows))
            return carry

        lax.fori_loop(0, per, body, 0)

    @pl.when(j == nj - 1)
    def _():
        def fin_body(i, carry):
            r0 = pl.multiple_of(i * mrows, mrows)
            out = h_ref[0, pl.ds(r0, mrows), :] + mod_ref[0, 5:6, :] * o_ref[0, pl.ds(r0, mrows), :]
            if final:
                out = out * lax.rsqrt(jnp.mean(out * out, axis=-1, keepdims=True) + RMS_EPS) * fg_ref[...]
            o_ref[0, pl.ds(r0, mrows), :] = out
            return carry

        lax.fori_loop(0, nblk, fin_body, 0)


def _ffn(h, mods, mod_row_fn, norm_g, ffn_w, layer, final_g, width, vertical, final):
    bsz, n_tok, d = h.shape
    wab, cwf, cbf, wd = ffn_w
    cw = FFN_TILE
    nj = wd.shape[1] // cw
    return pl.pallas_call(
        functools.partial(_ffn_kernel, n_tok=n_tok, width=width, vertical=vertical, cw=cw, mrows=TOK_TILE,
                          final=final),
        grid=(bsz, nj),
        in_specs=[pl.BlockSpec((1, n_tok, d), lambda b, j: (b, 0, 0), pipeline_mode=pl.Buffered(1)),
                  pl.BlockSpec((1, 6, d), lambda b, j: (mod_row_fn(b), 0, 0)),
                  pl.BlockSpec((1, d), lambda b, j: (0, 0)),
                  pl.BlockSpec((None, d, 2 * cw), lambda b, j: (layer, 0, j)),
                  pl.BlockSpec((None, 9, 2 * cw), lambda b, j: (layer, 0, j)),
                  pl.BlockSpec((None, 1, 2 * cw), lambda b, j: (layer, 0, j)),
                  pl.BlockSpec((None, cw, d), lambda b, j: (layer, j, 0)),
                  pl.BlockSpec((1, d), lambda b, j: (0, 0))],
        out_specs=pl.BlockSpec((1, n_tok, d), lambda b, j: (b, 0, 0)),
        out_shape=jax.ShapeDtypeStruct(h.shape, F32),
        scratch_shapes=[pltpu.VMEM((n_tok + width, d), BF16),
                        pltpu.VMEM((3, n_tok + 2 * width, 2 * cw), F32),
                        pltpu.VMEM((n_tok, cw), BF16)],
        input_output_aliases={0: 0},
        compiler_params=pltpu.CompilerParams(
            dimension_semantics=("parallel", "arbitrary"), vmem_limit_bytes=VMEM_LIMIT),
        name="ffn",
    )(h, mods, norm_g.reshape(1, d), wab, cwf, cbf, wd, final_g.reshape(1, d))


def _ffn_weights(w_up, conv_w, conv_b, w_down):
    depth, ffn, _ = w_down.shape
    cw = FFN_TILE
    fp = ((ffn + cw - 1) // cw) * cw
    nj = fp // cw

    def tiles(x):
        lead = x.shape[:-1]
        halves = []
        for part in (x[..., :ffn], x[..., ffn:]):
            part = jnp.pad(part, [(0, 0)] * len(lead) + [(0, fp - ffn)])
            halves.append(part.reshape(lead + (nj, cw)))
        return jnp.concatenate(halves, axis=-1).reshape(lead + (nj * 2 * cw,))

    wd = jnp.pad(w_down.astype(BF16), [(0, 0), (0, fp - ffn), (0, 0)])
    return (tiles(w_up.astype(BF16)), tiles(conv_w.reshape(depth, 9, 2 * ffn)),
            tiles(conv_b.reshape(depth, 1, 2 * ffn)), wd)


def _lower_bounds(lb_raw):
    p = jax.nn.softmax(lb_raw.astype(F32), axis=0)
    cs = jnp.cumsum(p, axis=0)
    return cs - cs[0:1]


def kernel(x, c, ctx, c_ctx, ada_w, ada_b, norm1_g, w_in, hg_lb_raw, hg_norm_g, w_hg_br, s5_a_re, s5_a_im, s5_log_dt, s5_b_re, s5_b_im, s5_c_re, s5_c_im, s5_d, w_s5_glu, w_s5_br, w_out, norm2_g, w_up, ffn_conv_w, ffn_conv_b, w_down, final_g):
    bsz, n_lat, d = x.shape
    n_ctx = ctx.shape[1]
    depth = ada_w.shape[0]
    dk = hg_norm_g.shape[1]
    hg_w = hg_lb_raw.shape[2]
    ngrp, nst, cg = s5_b_re.shape[1:]
    assert n_lat % TOK_TILE == 0 and n_ctx % TOK_TILE == 0 and n_lat % GLA_C == 0 and n_ctx % GLA_C == 0
    assert 5 * hg_w + ngrp * cg + 2 * d == w_in.shape[2] and S5_T * cg == 256 and 128 % cg == 0
    assert ngrp % (128 // cg) == 0

    rows = ((bsz + 1 + 7) // 8) * 8
    cond = jnp.zeros((rows, d), F32).at[:bsz].set(c).at[bsz].set(c_ctx)
    mods = _ada_mod(cond, ada_w, ada_b).reshape(depth, rows, 6, d)
    lat_row = lambda b: b
    ctx_row = lambda b: bsz
    lbs = _lower_bounds(hg_lb_raw)
    lat_blk = n_lat // TOK_TILE

    w_in_b, whg_b, wglu_b, wbr_b, wout_b = (w.astype(BF16) for w in (w_in, w_hg_br, w_s5_glu, w_s5_br, w_out))
    ffn_w = _ffn_weights(w_up, ffn_conv_w, ffn_conv_b, w_down)
    s5_tabs = jax.vmap(_s5_tables)(s5_a_re, s5_a_im, s5_log_dt, s5_b_re, s5_b_im, s5_c_re, s5_c_im)
    skip = jnp.broadcast_to(s5_d.astype(F32).reshape(depth, ngrp, 1, cg), (depth, ngrp, S5_T, cg))
    skip = skip.reshape(depth, 1, ngrp * S5_T * cg)

    h_lat, h_ctx = x, ctx
    for l in range(depth):
        last = l == depth - 1
        p_hg, z, p_gate = _proj_in(h_lat, h_ctx, mods[l], norm1_g[l], w_in_b, l, 5 * hg_w, ngrp, cg)

        o_fwd = _gla(p_hg, lbs[l, 0], hg_norm_g[l], None, n_lat, n_ctx, hg_w, dk, rev=False, ctx_out=not last)
        yh = _gla(p_hg, lbs[l, 1], hg_norm_g[l], o_fwd, n_lat, n_ctx, hg_w, dk, rev=True, ctx_out=not last)
        yz = _s5(z, s5_tabs, l, skip[l], n_lat // S5_T, n_ctx // S5_T)

        wts = (whg_b, wglu_b, wbr_b, wout_b, l, ngrp, cg)
        h_lat = _merge(h_lat, yh, yz, p_gate, mods[l], lat_row, *wts, 0)
        h_lat = _ffn(h_lat, mods[l], lat_row, norm2_g[l], ffn_w, l, final_g, GRID_W, True, last)
        if not last:
            h_ctx = _merge(h_ctx, yh, yz, p_gate, mods[l], ctx_row, *wts, lat_blk)
            stack = math.gcd(bsz, max(1, n_lat // n_ctx))
            h_ctx = _ffn(h_ctx.reshape(bsz // stack, stack * n_ctx, d), mods[l], ctx_row, norm2_g[l], ffn_w, l,
                         final_g, n_ctx, False, False).reshape(bsz, n_ctx, d)
    return h_lat
```

```python
import functools
import math

import numpy as np
import jax
import jax.numpy as jnp
from jax import lax
from jax.experimental import pallas as pl
from jax.experimental.pallas import tpu as pltpu

GRID_W = 64
RMS_EPS = 1e-6
S5_MAX_RE = -1e-4
LOG2_E = 1.4426950408889634
S5_T = 16
GLA_C = 128
HEADS_PER_ITER = 8
TOK_TILE = 256
FFN_TILE = 256
FFN_STREAMS = 4
VMEM_LIMIT = 56 * 1024 * 1024

F32 = jnp.float32
BF16 = jnp.bfloat16
_HI = lax.Precision.HIGHEST


def _nt_dot(a, b):
    return lax.dot_general(a, b, (((1,), (1,)), ((), ())), preferred_element_type=F32)


def _tn_dot(a, b):
    return lax.dot_general(a, b, (((0,), (0,)), ((), ())), preferred_element_type=F32)


def _dot(a, b):
    return jnp.dot(a, b, preferred_element_type=F32)


def _sigmoid(x):
    return 1.0 / (1.0 + jnp.exp(-x))


def _const_spec(shape):
    nd = len(shape)
    return pl.BlockSpec(shape, lambda *_: (0,) * nd, pipeline_mode=pl.Buffered(1))


def _layer_spec(stacked, layer):
    rest = stacked.shape[1:]
    return pl.BlockSpec((None,) + rest, lambda *_: (layer,) + (0,) * len(rest), pipeline_mode=pl.Buffered(1))


def _ada_kernel(cond_ref, w_ref, b_ref, o_ref):
    cnd = cond_ref[...]
    s = cnd * _sigmoid(cnd)
    o_ref[0] = jnp.dot(s, w_ref[0], preferred_element_type=F32, precision=_HI) + b_ref[0]


def _ada_mod(cond, ada_w, ada_b):
    depth, d, n6 = ada_w.shape
    rows = cond.shape[0]
    tn = 512
    return pl.pallas_call(
        _ada_kernel,
        grid=(depth, n6 // tn),
        in_specs=[
            pl.BlockSpec((rows, d), lambda l, j: (0, 0)),
            pl.BlockSpec((1, d, tn), lambda l, j: (l, 0, j)),
            pl.BlockSpec((1, 1, tn), lambda l, j: (l, 0, j)),
        ],
        out_specs=pl.BlockSpec((1, rows, tn), lambda l, j: (l, 0, j)),
        out_shape=jax.ShapeDtypeStruct((depth, rows, n6), F32),
        name="ada_mod",
    )(cond, ada_w, ada_b.reshape(depth, 1, n6))


def _modulated_norm(x, g, shift, scale):
    ms = jnp.mean(x * x, axis=-1, keepdims=True)
    y = x * lax.rsqrt(ms + RMS_EPS) * g
    return y * (1.0 + scale) + shift


def _strip_masks(n_rows, cg):
    lane = lax.broadcasted_iota(jnp.int32, (n_rows, 128), 1)
    return [(lane >= j * cg) & (lane < (j + 1) * cg) for j in range(128 // cg)]


def _chunk_flatten(u_scr, z_ref, n_chunks, ngrp, cg):
    gpt = 128 // cg
    masks = _strip_masks(n_chunks, cg)
    for g in range(ngrp):
        k, i = divmod(g, gpt)
        for hf in range(S5_T // gpt):
            acc = None
            for j in range(gpt):
                src = u_scr[k, pl.ds(hf * gpt + j, n_chunks, stride=S5_T), :]
                shift = ((j - i) * cg) % 128
                if shift:
                    src = pltpu.roll(src, shift, axis=1)
                acc = src if acc is None else jnp.where(masks[j], src, acc)
            tile = g * (S5_T // gpt) + hf
            z_ref[0, :, tile * 128:(tile + 1) * 128] = acc.astype(z_ref.dtype)


def _chunk_unflatten(yz_ref, scr, n_chunks, ngrp, cg):
    gpt = 128 // cg
    masks = _strip_masks(n_chunks, cg)
    for k in range(ngrp // gpt):
        for hf in range(S5_T // gpt):
            srcs = []
            for i in range(gpt):
                tile = (k * gpt + i) * (S5_T // gpt) + hf
                srcs.append(yz_ref[0, :, tile * 128:(tile + 1) * 128].astype(F32))
            for j in range(gpt):
                acc = None
                for i in range(gpt):
                    shift = ((i - j) * cg) % 128
                    src = pltpu.roll(srcs[i], shift, axis=1) if shift else srcs[i]
                    acc = src if acc is None else jnp.where(masks[i], src, acc)
                t = hf * gpt + j
                scr[k, t * n_chunks:(t + 1) * n_chunks, :] = acc
    return jnp.concatenate(
        [jnp.concatenate([scr[k, pl.ds(ch, S5_T, stride=n_chunks), :] for k in range(scr.shape[0])], axis=1)
         for ch in range(n_chunks)], axis=0)


def _proj_in_kernel(hl_ref, hc_ref, mod_ref, g_ref, w_ref, hg_ref, z_ref, gate_ref, xn_scr, u_scr,
                    *, nt_lat, ngrp, cg):
    def norm(h_ref):
        xn_scr[...] = _modulated_norm(h_ref[0], g_ref[...], mod_ref[0, 0:1, :], mod_ref[0, 1:2, :]).astype(BF16)

    pl.when(pl.program_id(1) < nt_lat)(lambda: norm(hl_ref))
    pl.when(pl.program_id(1) >= nt_lat)(lambda: norm(hc_ref))
    xn = xn_scr[...]
    hg_w, s5_w = hg_ref.shape[2], ngrp * cg

    def project(o_ref, col0):
        width = o_ref.shape[2]
        step = math.gcd(512, width)
        for c in range(0, width, step):
            o_ref[0, :, c:c + step] = _dot(xn, w_ref[:, col0 + c:col0 + c + step]).astype(o_ref.dtype)

    u = _dot(xn, w_ref[:, hg_w:hg_w + s5_w])
    for k in range(s5_w // 128):
        u_scr[k] = u[:, k * 128:(k + 1) * 128]
    _chunk_flatten(u_scr, z_ref, xn_scr.shape[0] // S5_T, ngrp, cg)
    project(hg_ref, 0)
    project(gate_ref, hg_w + s5_w)


def _proj_in(h_lat, h_ctx, mods, norm_g, w_stack, layer, hg_cols, ngrp, cg):
    bsz, n_lat, d = h_lat.shape
    n_ctx = h_ctx.shape[1]
    total = n_lat + n_ctx
    nt_lat, nt_ctx = n_lat // TOK_TILE, n_ctx // TOK_TILE
    s5_w = ngrp * cg
    gate_cols = w_stack.shape[2] - hg_cols - s5_w
    ct = TOK_TILE // S5_T
    return pl.pallas_call(
        functools.partial(_proj_in_kernel, nt_lat=nt_lat, ngrp=ngrp, cg=cg),
        grid=(bsz, nt_lat + nt_ctx),
        in_specs=[
            pl.BlockSpec((1, TOK_TILE, d), lambda b, t: (b, jnp.minimum(t, nt_lat - 1), 0)),
            pl.BlockSpec((1, TOK_TILE, d), lambda b, t: (b, jnp.maximum(t - nt_lat, 0), 0)),
            pl.BlockSpec((1, 6, d), lambda b, t: (jnp.where(t < nt_lat, b, bsz), 0, 0)),
            pl.BlockSpec((1, d), lambda b, t: (0, 0)),
            _layer_spec(w_stack, layer),
        ],
        out_specs=[pl.BlockSpec((1, TOK_TILE, hg_cols), lambda b, t: (b, t, 0)),
                   pl.BlockSpec((1, ct, s5_w * S5_T), lambda b, t: (b, t, 0)),
                   pl.BlockSpec((1, TOK_TILE, gate_cols), lambda b, t: (b, t, 0))],
        out_shape=[jax.ShapeDtypeStruct((bsz, total, hg_cols), BF16),
                   jax.ShapeDtypeStruct((bsz, total // S5_T, s5_w * S5_T), BF16),
                   jax.ShapeDtypeStruct((bsz, total, gate_cols), BF16)],
        scratch_shapes=[pltpu.VMEM((TOK_TILE, d), BF16), pltpu.VMEM((s5_w // 128, TOK_TILE, 128), F32)],
        compiler_params=pltpu.CompilerParams(
            dimension_semantics=("parallel", "arbitrary"), vmem_limit_bytes=VMEM_LIMIT),
        name="proj_in",
    )(h_lat, h_ctx, mods, norm_g.reshape(1, d), w_stack)


def _gla_tables(c, rev):
    t = np.arange(c)[:, None]
    s = np.arange(c)[None, :]
    x = t ^ s
    lev = np.full((c, c), -2, np.int32)
    causal = (t < s) if rev else (t > s)
    with np.errstate(divide="ignore"):
        hb = np.floor(np.log2(np.maximum(x, 1))).astype(np.int32)
    lev = np.where(causal, hb, lev)
    lev = np.where(t == s, -1, lev).astype(np.int32)
    tri = ((t <= s) if rev else (t >= s)).astype(np.float32)
    return jnp.asarray(lev), jnp.asarray(tri, dtype=BF16)


def _gla_kernel(*refs, rev, readout, c, nheads, dk, state_only_steps):
    if readout:
        (q_ref, fz_ref, i_ref, lb_ref, lev_ref, tri_ref, g_ref, of_ref, ng_ref,
         out_ref, st_ref, b_scr_all) = refs
    else:
        q_ref, fz_ref, i_ref, lb_ref, lev_ref, tri_ref, out_ref, st_ref, b_scr_all = refs
    nlev = int(math.log2(c))

    @pl.when(pl.program_id(1) == 0)
    def _():
        st_ref[...] = jnp.zeros_like(st_ref)

    lev = lev_ref[...]
    tri = tri_ref[...]
    row = lax.broadcasted_iota(jnp.int32, (c, 1), 0)

    def head(hd, b_scr, full):
        sl = pl.ds(pl.multiple_of(hd * dk, dk), dk)
        z = fz_ref[0, :, sl].astype(F32)
        v = i_ref[0, :, sl]
        lb = lb_ref[0:1, sl]

        e = jnp.exp(-jnp.abs(z))
        den = 1.0 + e
        inv = 1.0 / den
        pos = z >= 0
        sig_pos = jnp.where(pos, 1.0, e) * inv
        sig_neg = jnp.where(pos, e, 1.0) * inv
        f = lb + (1.0 - lb) * sig_pos
        a = jnp.where(lb > 0, jnp.log(f), jnp.minimum(z, 0.0) - jnp.log(den)) * LOG2_E
        k = (1.0 - lb) * sig_neg

        a_hi = a.astype(BF16)
        a_lo = (a - a_hi.astype(F32)).astype(BF16)
        bb = _dot(tri, jnp.concatenate([a_hi, a_lo], axis=1))
        b = bb[:, :dk] + bb[:, dk:]
        b_scr[...] = b

        st = st_ref[hd]
        b_end = b_scr[pl.ds(0 if rev else c - 1, 1), :]
        k_out = (k * jnp.exp2(b_end - b)).astype(BF16)
        st_ref[hd] = jnp.exp2(b_end) * st + _tn_dot(v, k_out)
        if not full:
            out_ref[0, :, sl] = jnp.zeros((c, dk), out_ref.dtype)
            return
        qz = q_ref[0, :, sl].astype(F32)
        q = qz * _sigmoid(qz)

        odd = (row & 1) == 1
        f_prev = pltpu.roll(f, 1, axis=0)
        f_next = pltpu.roll(f, c - 1, axis=0)
        qf = q * f
        scores = jnp.zeros((c, c), F32)
        for level in range(nlev):
            half = 1 << level
            if level == 0:
                xq, xk = qf, k
            elif level == 1:
                if rev:
                    xq, xk = qf * jnp.where(odd, 1.0, f_next), k * jnp.where(odd, f_prev, 1.0)
                else:
                    xq, xk = qf * jnp.where(odd, f_prev, 1.0), k * jnp.where(odd, 1.0, f_next)
            else:
                blk = 2 * half
                pieces = []
                for j in range(c // blk):
                    brow = b_scr[pl.ds(j * blk + (half if rev else half - 1), 1), :]
                    pieces.append(jnp.broadcast_to(brow, (blk, dk)))
                ref_b = jnp.concatenate(pieces, axis=0) if len(pieces) > 1 else pieces[0]
                decay = jnp.exp2(-jnp.abs(b - ref_b))
                xq, xk = q * decay, k * decay
            scores = jnp.where(lev == level, _nt_dot(xq.astype(BF16), xk.astype(BF16)), scores)
        scores = jnp.where(lev == -1, jnp.sum(q * k, axis=-1, keepdims=True), scores)
        o = _dot(scores.astype(BF16), v) + _nt_dot((q * jnp.exp2(b)).astype(BF16), st.astype(BF16))

        if readout:
            o = o + of_ref[0, :, sl].astype(F32)
            o = o * lax.rsqrt(jnp.mean(o * o, axis=-1, keepdims=True) + RMS_EPS) * ng_ref[...]
            gz = g_ref[0, :, sl].astype(F32)
            o = o * (gz * _sigmoid(gz))
        out_ref[0, :, sl] = o.astype(out_ref.dtype)

    per_iter = b_scr_all.shape[0]

    def all_heads(full):
        def head_group(i, carry):
            for slot in range(per_iter):
                head(i * per_iter + slot, b_scr_all.at[slot], full)
            return carry

        lax.fori_loop(0, nheads // per_iter, head_group, 0)

    if state_only_steps:
        pl.when(pl.program_id(1) < state_only_steps)(lambda: all_heads(False))
        pl.when(pl.program_id(1) >= state_only_steps)(lambda: all_heads(True))
    else:
        all_heads(True)


def _gla(p_hg, lb_row, norm_g, o_fwd, n_lat, n_ctx, width, dk, rev, ctx_out):
    bsz, total, _ = p_hg.shape
    c = GLA_C
    nl, nc = n_lat // c, n_ctx // c
    nheads = width // dk
    readout = rev

    if rev:
        def chunk(i):
            return jnp.where(i < nc, nl + nc - 1 - i, nl + nc - 1 - i)
    else:
        def chunk(i):
            return jnp.where(i < nc, nl + i, i - nc)

    def col_spec(colblk):
        return pl.BlockSpec((1, c, width), lambda b, i: (b, chunk(i), colblk))

    lev, tri = _gla_tables(c, rev)
    in_specs = [col_spec(0), col_spec(2 if rev else 1), col_spec(3),
                pl.BlockSpec((1, width), lambda b, i: (0, 0)),
                pl.BlockSpec((c, c), lambda b, i: (0, 0)),
                pl.BlockSpec((c, c), lambda b, i: (0, 0))]
    args = [p_hg, p_hg, p_hg, lb_row.reshape(1, width), lev, tri]
    if readout:
        in_specs += [col_spec(4),
                     pl.BlockSpec((1, c, width), lambda b, i: (b, chunk(i), 0)),
                     pl.BlockSpec((1, dk), lambda b, i: (0, 0))]
        args += [p_hg, o_fwd, norm_g.reshape(1, dk)]
    return pl.pallas_call(
        functools.partial(_gla_kernel, rev=rev, readout=readout, c=c, nheads=nheads, dk=dk,
                          state_only_steps=0 if ctx_out else nc),
        grid=(bsz, nl + nc),
        in_specs=in_specs,
        out_specs=pl.BlockSpec((1, c, width), lambda b, i: (b, chunk(i), 0)),
        out_shape=jax.ShapeDtypeStruct((bsz, total, width), BF16),
        scratch_shapes=[pltpu.VMEM((nheads, dk, dk), F32), pltpu.VMEM((math.gcd(HEADS_PER_ITER, nheads), c, dk), F32)],
        compiler_params=pltpu.CompilerParams(
            dimension_semantics=("parallel", "arbitrary"), vmem_limit_bytes=VMEM_LIMIT),
        name="gla_bwd" if rev else "gla_fwd",
    )(*args)


def _s5_tables(a_re, a_im, log_dt, b_re, b_im, c_re, c_im):
    t = S5_T
    ngrp, nst, cg = b_re.shape
    a_re = jnp.minimum(a_re.astype(F32), S5_MAX_RE)
    a_im = a_im.astype(F32)
    dt = jnp.exp(log_dt.astype(F32))[..., None]
    mag = jnp.exp(dt * a_re)
    abr, abi = mag * jnp.cos(dt * a_im), mag * jnp.sin(dt * a_im)
    den = a_re * a_re + a_im * a_im
    nr, ni = abr - 1.0, abi
    f_re = ((nr * a_re + ni * a_im) / den)[..., None]
    f_im = ((ni * a_re - nr * a_im) / den)[..., None]
    b_re, b_im = b_re.astype(F32)[None], b_im.astype(F32)[None]
    bb_re = f_re * b_re - f_im * b_im
    bb_im = f_re * b_im + f_im * b_re
    pr, pi = [jnp.ones_like(abr)], [jnp.zeros_like(abr)]
    for _ in range(t):
        pr.append(pr[-1] * abr - pi[-1] * abi)
        pi.append(pr[-2] * abi + pi[-1] * abr)
    pr, pi = jnp.stack(pr), jnp.stack(pi)
    c_re, c_im = c_re.astype(F32), c_im.astype(F32)
    l_re = c_re[None, None] * pr[:, :, :, None, :] - c_im[None, None] * pi[:, :, :, None, :]
    l_im = c_re[None, None] * pi[:, :, :, None, :] + c_im[None, None] * pr[:, :, :, None, :]
    kern = (jnp.einsum("drgcn,rgnk->drgck", l_re[:t], bb_re, precision=_HI)
            - jnp.einsum("drgcn,rgnk->drgck", l_im[:t], bb_im, precision=_HI))
    lag = np.arange(t)[:, None] - np.arange(t)[None, :]
    sel = np.stack([lag[:, :, None] == np.arange(t), -lag[:, :, None] == np.arange(t)])
    mt = jnp.einsum("rtsd,drgck->gsktc", jnp.asarray(sel, BF16), kern.astype(BF16),
                    preferred_element_type=F32).astype(BF16).reshape(ngrp, t * cg, t * cg)
    pf_r, pf_i = pr[t - 1 - np.arange(t), 0], pi[t - 1 - np.arange(t), 0]
    pb_r, pb_i = pr[np.arange(t), 1], pi[np.arange(t), 1]

    def drive(p_r, p_i, r):
        g_r = p_r[..., None] * bb_re[r][None] - p_i[..., None] * bb_im[r][None]
        g_i = p_r[..., None] * bb_im[r][None] + p_i[..., None] * bb_re[r][None]
        to = lambda x: x.transpose(1, 0, 3, 2).reshape(ngrp, t * cg, nst)
        return to(g_r), to(g_i)

    gf_r, gf_i = drive(pf_r, pf_i, 0)
    gb_r, gb_i = drive(pb_r, pb_i, 1)
    def read(idx, r):
        rr = l_re[idx, r].transpose(1, 3, 0, 2).reshape(ngrp, nst, t * cg)
        ri = -l_im[idx, r].transpose(1, 3, 0, 2).reshape(ngrp, nst, t * cg)
        return rr, ri

    of_r, of_i = read(np.arange(t) + 1, 0)
    ob_r, ob_i = read(t - np.arange(t), 1)

    eye = jnp.eye(2, dtype=BF16)
    g4 = jnp.stack([gf_r, gf_i, gb_r, gb_i], axis=2).astype(BF16).reshape(ngrp // 2, 2, t * cg, 4, 1, nst)
    gin = (g4 * eye[None, :, None, None, :, None]).reshape(ngrp // 2, 2 * t * cg, 8 * nst)
    o4 = jnp.stack([of_r, of_i, ob_r, ob_i], axis=1).astype(BF16).reshape(ngrp // 2, 2, 4, nst, 1, t * cg)
    o4 = o4.transpose(0, 2, 1, 3, 4, 5)
    gout = (o4 * eye[None, None, :, None, :, None]).reshape(ngrp // 2, 8 * nst, 2 * t * cg)
    a_chunk = jnp.stack([pr[t, 0], pi[t, 0], pr[t, 1], pi[t, 1]]).reshape(4, ngrp * nst)
    return mt, gin, gout, a_chunk


def _s5_kernel(z_ref, mt_ref, gin_ref, gout_ref, ac_ref, skip_ref, y_ref, d_scr, x_scr,
               *, n_lat, n_ctx, npairs, pw):
    nrows = n_lat + n_ctx
    gw = pw // 2
    tw = pw // 4
    for p in range(npairs):
        d = _dot(z_ref[0, :, p * pw:(p + 1) * pw], gin_ref[p])
        for j in range(4):
            d_scr[j, :, p * tw:(p + 1) * tw] = d[:, j * tw:(j + 1) * tw]

    def scan(plane, forward):
        ar = ac_ref[plane:plane + 1, :]
        ai = ac_ref[plane + 1:plane + 2, :]

        def body(i, carry):
            xr, xi = carry
            if forward:
                n = jnp.where(i < n_ctx, n_lat + i, i - n_ctx)
            else:
                n = nrows - 1 - i
            x_scr[plane, pl.ds(n, 1), :] = xr
            x_scr[plane + 1, pl.ds(n, 1), :] = xi
            dr = d_scr[plane, pl.ds(n, 1), :]
            di = d_scr[plane + 1, pl.ds(n, 1), :]
            return ar * xr - ai * xi + dr, ar * xi + ai * xr + di

        zero = jnp.zeros((1, d_scr.shape[2]), F32)
        lax.fori_loop(0, nrows, body, (zero, zero))

    scan(0, True)
    scan(2, False)

    for p in range(npairs):
        xp = jnp.concatenate([x_scr[j, :, p * tw:(p + 1) * tw] for j in range(4)], axis=1).astype(BF16)
        y = _dot(xp, gout_ref[p])
        for g in range(2):
            lo = p * pw + g * gw
            zg = z_ref[0, :, lo:lo + gw]
            yg = y[:, g * gw:(g + 1) * gw] + _dot(zg, mt_ref[2 * p + g]) + skip_ref[:, lo:lo + gw] * zg.astype(F32)
            y_ref[0, :, lo:lo + gw] = yg.astype(y_ref.dtype)


def _s5(zf, tables, layer, skip_flat, n_lat_chunks, n_ctx_chunks):
    bsz, nrows, wide = zf.shape
    mt, gin, gout, a_chunk = tables
    npairs, pw = gin.shape[1:3]
    planes = a_chunk.shape[2]
    return pl.pallas_call(
        functools.partial(_s5_kernel, n_lat=n_lat_chunks, n_ctx=n_ctx_chunks, npairs=npairs, pw=pw),
        grid=(bsz,),
        in_specs=[pl.BlockSpec((1, nrows, wide), lambda b: (b, 0, 0)),
                  _layer_spec(mt, layer), _layer_spec(gin, layer), _layer_spec(gout, layer),
                  _layer_spec(a_chunk, layer), _const_spec(skip_flat.shape)],
        out_specs=pl.BlockSpec((1, nrows, wide), lambda b: (b, 0, 0)),
        out_shape=jax.ShapeDtypeStruct((bsz, nrows, wide), BF16),
        scratch_shapes=[pltpu.VMEM((4, nrows, planes), F32), pltpu.VMEM((4, nrows, planes), F32)],
        compiler_params=pltpu.CompilerParams(
            dimension_semantics=("parallel",), vmem_limit_bytes=VMEM_LIMIT),
        name="s5",
    )(zf, mt, gin, gout, a_chunk, skip_flat)


def _gelu_tanh(x):
    return 0.5 * x * (1.0 + jnp.tanh(math.sqrt(2.0 / math.pi) * (x + 0.044715 * (x * x * x))))


def _merge_kernel(h_ref, yh_ref, yz_ref, gate_ref, mod_ref, whg_ref, wglu_ref, wbr_ref, wout_ref, o_ref, y_scr,
                  *, ngrp, cg):
    d = h_ref.shape[2]
    y_hg = _dot(yh_ref[0], whg_ref[...])
    ys = _gelu_tanh(_chunk_unflatten(yz_ref, y_scr, yz_ref.shape[1], ngrp, cg))
    glu = _sigmoid(_dot(ys.astype(BF16), wglu_ref[...]))
    y_s5 = _dot((ys * glu).astype(BF16), wbr_ref[...])
    gate = gate_ref[0].astype(F32)
    merged = _sigmoid(gate[:, :d]) * y_hg + _sigmoid(gate[:, d:]) * y_s5
    o_ref[0] = h_ref[0] + mod_ref[0, 2:3, :] * _dot(merged.astype(BF16), wout_ref[...])


def _merge(h, yh, yz, p_gate, mods, mod_row_fn, whg, wglu, wbr, wout, layer, ngrp, cg, tok_off_blocks):
    bsz, length, d = h.shape
    nt = length // TOK_TILE
    ct = TOK_TILE // S5_T
    seq = lambda rows, w: pl.BlockSpec((1, rows, w), lambda b, t: (b, t + tok_off_blocks, 0))
    return pl.pallas_call(
        functools.partial(_merge_kernel, ngrp=ngrp, cg=cg),
        grid=(bsz, nt),
        in_specs=[pl.BlockSpec((1, TOK_TILE, d), lambda b, t: (b, t, 0)),
                  seq(TOK_TILE, yh.shape[2]), seq(ct, yz.shape[2]), seq(TOK_TILE, p_gate.shape[2]),
                  pl.BlockSpec((1, 6, d), lambda b, t: (mod_row_fn(b), 0, 0)),
                  _layer_spec(whg, layer), _layer_spec(wglu, layer), _layer_spec(wbr, layer),
                  _layer_spec(wout, layer)],
        out_specs=pl.BlockSpec((1, TOK_TILE, d), lambda b, t: (b, t, 0)),
        out_shape=jax.ShapeDtypeStruct(h.shape, F32),
        scratch_shapes=[pltpu.VMEM((ngrp * cg // 128, TOK_TILE, 128), F32)],
        input_output_aliases={0: 0} if layer > 0 else {},
        compiler_params=pltpu.CompilerParams(
            dimension_semantics=("parallel", "parallel"), vmem_limit_bytes=VMEM_LIMIT),
        name="merge",
    )(h, yh, yz, p_gate, mods, whg, wglu, wbr, wout)


def _ffn_kernel(h_ref, mod_ref, g_ref, wa_ref, wb_ref, cw_ref, cb_ref, wd_ref, fg_ref,
                o_ref, xn_scr, up_scr, act_scr, *, n_tok, width, vertical, cw, mrows, final):
    j = pl.program_id(1)
    nj = pl.num_programs(1)
    nblk = n_tok // mrows
    rpb = mrows // width
    pad = width

    @pl.when(j == 0)
    def _():
        def norm_body(i, carry):
            r0 = pl.multiple_of(i * mrows, mrows)
            xn_scr[pl.ds(r0, mrows), :] = _modulated_norm(
                h_ref[0, pl.ds(r0, mrows), :], g_ref[...], mod_ref[0, 3:4, :], mod_ref[0, 4:5, :]).astype(BF16)
            o_ref[0, pl.ds(r0, mrows), :] = jnp.zeros((mrows, o_ref.shape[2]), F32)
            return carry

        lax.fori_loop(0, nblk, norm_body, 0)
        xn_scr[n_tok:n_tok + width, :] = jnp.zeros((width, xn_scr.shape[1]), BF16)
        for copy in range(3):
            up_scr[copy, 0:pad, :] = jnp.zeros((pad, 2 * cw), F32)
            up_scr[copy, pad + n_tok:pad + n_tok + width, :] = jnp.zeros((width, 2 * cw), F32)

    def up_rows(row0, m):
        xn = xn_scr[pl.ds(row0, m), :]
        up = jnp.concatenate([_dot(xn, wa_ref[...]), _dot(xn, wb_ref[...])], axis=1)
        col = lax.broadcasted_iota(jnp.int32, (m, 1), 0) % width
        up_scr[0, pl.ds(pad + row0, m), :] = jnp.where(col == 0, 0.0, pltpu.roll(up, 1, axis=0))
        up_scr[1, pl.ds(pad + row0, m), :] = up
        up_scr[2, pl.ds(pad + row0, m), :] = jnp.where(col == width - 1, 0.0, pltpu.roll(up, m - 1, axis=0))

    def conv_row(row0, lane0):
        acc = cb_ref[:, lane0:lane0 + 128]
        for dr in ((-1, 0, 1) if vertical else (0,)):
            for dc in (-1, 0, 1):
                tap = (dr + 1) * 3 + (dc + 1)
                src = up_scr[dc + 1, pl.ds(pl.multiple_of(pad + row0 + dr * width, 8), width), lane0:lane0 + 128]
                acc = acc + cw_ref[tap:tap + 1, lane0:lane0 + 128] * src
        return acc

    def conv_block(blk):
        r0 = blk * mrows
        for lt in range(cw // 128):
            for r in range(rpb):
                row0 = r0 + r * width
                ca = conv_row(row0, lt * 128)
                cb = conv_row(row0, cw + lt * 128)
                act_scr[pl.ds(pl.multiple_of(row0, 16), width), lt * 128:(lt + 1) * 128] = (
                    ca * _sigmoid(ca) * cb).astype(BF16)

    def down_rows(row0):
        o_ref[0, pl.ds(row0, mrows), :] += _dot(act_scr[pl.ds(row0, mrows), :], wd_ref[...])

    if nblk == 1:
        up_rows(0, mrows)
        conv_block(0)
        down_rows(0)
    else:
        streams = math.gcd(FFN_STREAMS, nblk)
        per = nblk // streams
        ahead = width if vertical else 0
        if vertical:
            up_rows(0, width)
            for s in range(1, streams):
                up_rows(s * per * mrows - width, 2 * width)

        def body(k, carry):
            for s in range(streams):
                r0 = pl.multiple_of((k + s * per) * mrows, mrows)
                up_rows(r0 + ahead, mrows)
            for s in range(streams):
                conv_block(k + s * per)
            for s in range(streams):
                down_rows(pl.multiple_of((k + s * per) * mrows, mrows))
            return carry

        lax.fori_loop(0, per, body, 0)

    @pl.when(j == nj - 1)
    def _():
        def fin_body(i, carry):
            r0 = pl.multiple_of(i * mrows, mrows)
            out = h_ref[0, pl.ds(r0, mrows), :] + mod_ref[0, 5:6, :] * o_ref[0, pl.ds(r0, mrows), :]
            if final:
                out = out * lax.rsqrt(jnp.mean(out * out, axis=-1, keepdims=True) + RMS_EPS) * fg_ref[...]
            o_ref[0, pl.ds(r0, mrows), :] = out
            return carry

        lax.fori_loop(0, nblk, fin_body, 0)


def _ffn(h, mods, mod_row_fn, norm_g, ffn_w, layer, final_g, width, vertical, final):
    bsz, n_tok, d = h.shape
    w_pad, cwf, cbf, wd = ffn_w
    cw = FFN_TILE
    nj = wd.shape[1] // cw
    return pl.pallas_call(
        functools.partial(_ffn_kernel, n_tok=n_tok, width=width, vertical=vertical, cw=cw, mrows=TOK_TILE,
                          final=final),
        grid=(bsz, nj),
        in_specs=[pl.BlockSpec((1, n_tok, d), lambda b, j: (b, 0, 0), pipeline_mode=pl.Buffered(1)),
                  pl.BlockSpec((1, 6, d), lambda b, j: (mod_row_fn(b), 0, 0)),
                  pl.BlockSpec((1, d), lambda b, j: (0, 0)),
                  pl.BlockSpec((None, d, cw), lambda b, j: (layer, 0, j)),
                  pl.BlockSpec((None, d, cw), lambda b, j: (layer, 0, nj + j)),
                  pl.BlockSpec((None, 9, 2 * cw), lambda b, j: (layer, 0, j)),
                  pl.BlockSpec((None, 1, 2 * cw), lambda b, j: (layer, 0, j)),
                  pl.BlockSpec((None, cw, d), lambda b, j: (layer, j, 0)),
                  pl.BlockSpec((1, d), lambda b, j: (0, 0))],
        out_specs=pl.BlockSpec((1, n_tok, d), lambda b, j: (b, 0, 0)),
        out_shape=jax.ShapeDtypeStruct(h.shape, F32),
        scratch_shapes=[pltpu.VMEM((n_tok + width, d), BF16),
                        pltpu.VMEM((3, n_tok + 2 * width, 2 * cw), F32),
                        pltpu.VMEM((n_tok, cw), BF16)],
        input_output_aliases={0: 0},
        compiler_params=pltpu.CompilerParams(
            dimension_semantics=("parallel", "arbitrary"), vmem_limit_bytes=VMEM_LIMIT),
        name="ffn",
    )(h, mods, norm_g.reshape(1, d), w_pad, w_pad, cwf, cbf, wd, final_g.reshape(1, d))


def _ffn_weights(w_up, conv_w, conv_b, w_down):
    depth, ffn, _ = w_down.shape
    cw = FFN_TILE
    fp = ((ffn + cw - 1) // cw) * cw
    nj = fp // cw

    def halves(x):
        padw = [(0, 0)] * (x.ndim - 1) + [(0, fp - ffn)]
        return jnp.pad(x[..., :ffn], padw), jnp.pad(x[..., ffn:], padw)

    def tiles(x):
        lead = x.shape[:-1]
        parts = [part.reshape(lead + (nj, cw)) for part in halves(x)]
        return jnp.concatenate(parts, axis=-1).reshape(lead + (nj * 2 * cw,))

    wd = jnp.pad(w_down.astype(BF16), [(0, 0), (0, fp - ffn), (0, 0)])
    w_pad = jnp.concatenate(halves(w_up.astype(BF16)), axis=-1)
    return w_pad, tiles(conv_w.reshape(depth, 9, 2 * ffn)), tiles(conv_b.reshape(depth, 1, 2 * ffn)), wd


def _lower_bounds(lb_raw):
    p = jax.nn.softmax(lb_raw.astype(F32), axis=0)
    cs = jnp.cumsum(p, axis=0)
    return cs - cs[0:1]


def kernel(x, c, ctx, c_ctx, ada_w, ada_b, norm1_g, w_in, hg_lb_raw, hg_norm_g, w_hg_br, s5_a_re, s5_a_im, s5_log_dt, s5_b_re, s5_b_im, s5_c_re, s5_c_im, s5_d, w_s5_glu, w_s5_br, w_out, norm2_g, w_up, ffn_conv_w, ffn_conv_b, w_down, final_g):
    bsz, n_lat, d = x.shape
    n_ctx = ctx.shape[1]
    depth = ada_w.shape[0]
    dk = hg_norm_g.shape[1]
    hg_w = hg_lb_raw.shape[2]
    ngrp, nst, cg = s5_b_re.shape[1:]
    assert n_lat % TOK_TILE == 0 and n_ctx % TOK_TILE == 0 and n_lat % GLA_C == 0 and n_ctx % GLA_C == 0
    assert 5 * hg_w + ngrp * cg + 2 * d == w_in.shape[2] and S5_T * cg == 256 and 128 % cg == 0
    assert ngrp % (128 // cg) == 0

    rows = ((bsz + 1 + 7) // 8) * 8
    cond = jnp.zeros((rows, d), F32).at[:bsz].set(c).at[bsz].set(c_ctx)
    mods = _ada_mod(cond, ada_w, ada_b).reshape(depth, rows, 6, d)
    lat_row = lambda b: b
    ctx_row = lambda b: bsz
    lbs = _lower_bounds(hg_lb_raw)
    lat_blk = n_lat // TOK_TILE

    w_in_b, whg_b, wglu_b, wbr_b, wout_b = (w.astype(BF16) for w in (w_in, w_hg_br, w_s5_glu, w_s5_br, w_out))
    ffn_w = _ffn_weights(w_up, ffn_conv_w, ffn_conv_b, w_down)
    s5_tabs = jax.vmap(_s5_tables)(s5_a_re, s5_a_im, s5_log_dt, s5_b_re, s5_b_im, s5_c_re, s5_c_im)
    skip = jnp.broadcast_to(s5_d.astype(F32).reshape(depth, ngrp, 1, cg), (depth, ngrp, S5_T, cg))
    skip = skip.reshape(depth, 1, ngrp * S5_T * cg)

    h_lat, h_ctx = x, ctx
    for l in range(depth):
        last = l == depth - 1
        p_hg, z, p_gate = _proj_in(h_lat, h_ctx, mods[l], norm1_g[l], w_in_b, l, 5 * hg_w, ngrp, cg)

        o_fwd = _gla(p_hg, lbs[l, 0], hg_norm_g[l], None, n_lat, n_ctx, hg_w, dk, rev=False, ctx_out=not last)
        yh = _gla(p_hg, lbs[l, 1], hg_norm_g[l], o_fwd, n_lat, n_ctx, hg_w, dk, rev=True, ctx_out=not last)
        yz = _s5(z, s5_tabs, l, skip[l], n_lat // S5_T, n_ctx // S5_T)

        wts = (whg_b, wglu_b, wbr_b, wout_b, l, ngrp, cg)
        h_lat = _merge(h_lat, yh, yz, p_gate, mods[l], lat_row, *wts, 0)
        h_lat = _ffn(h_lat, mods[l], lat_row, norm2_g[l], ffn_w, l, final_g, GRID_W, True, last)
        if not last:
            h_ctx = _merge(h_ctx, yh, yz, p_gate, mods[l], ctx_row, *wts, lat_blk)
            stack = math.gcd(bsz, max(1, n_lat // n_ctx))
            h_ctx = _ffn(h_ctx.reshape(bsz // stack, stack * n_ctx, d), mods[l], ctx_row, norm2_g[l], ffn_w, l,
                         final_g, n_ctx, False, False).reshape(bsz, n_ctx, d)
    return h_lat
```

```python
import functools
import math

import numpy as np
import jax
import jax.numpy as jnp
from jax import lax
from jax.experimental import pallas as pl
from jax.experimental.pallas import tpu as pltpu

GRID_W = 64
RMS_EPS = 1e-6
S5_MAX_RE = -1e-4
LOG2_E = 1.4426950408889634
S5_T = 16
GLA_C = 128
HEADS_PER_ITER = 8
TOK_TILE = 256
FFN_TILE = 256
FFN_STREAMS = 4
VMEM_LIMIT = 56 * 1024 * 1024

F32 = jnp.float32
BF16 = jnp.bfloat16
_HI = lax.Precision.HIGHEST


def _nt_dot(a, b):
    return lax.dot_general(a, b, (((1,), (1,)), ((), ())), preferred_element_type=F32)


def _tn_dot(a, b):
    return lax.dot_general(a, b, (((0,), (0,)), ((), ())), preferred_element_type=F32)


def _dot(a, b):
    return jnp.dot(a, b, preferred_element_type=F32)


def _sigmoid(x):
    return 1.0 / (1.0 + jnp.exp(-x))


def _const_spec(shape):
    nd = len(shape)
    return pl.BlockSpec(shape, lambda *_: (0,) * nd, pipeline_mode=pl.Buffered(1))


def _layer_spec(stacked, layer):
    rest = stacked.shape[1:]
    return pl.BlockSpec((None,) + rest, lambda *_: (layer,) + (0,) * len(rest), pipeline_mode=pl.Buffered(1))


def _ada_kernel(cond_ref, w_ref, b_ref, o_ref):
    cnd = cond_ref[...]
    s = cnd * _sigmoid(cnd)
    o_ref[0] = jnp.dot(s, w_ref[0], preferred_element_type=F32, precision=_HI) + b_ref[0]


def _ada_mod(cond, ada_w, ada_b):
    depth, d, n6 = ada_w.shape
    rows = cond.shape[0]
    tn = 512
    return pl.pallas_call(
        _ada_kernel,
        grid=(depth, n6 // tn),
        in_specs=[
            pl.BlockSpec((rows, d), lambda l, j: (0, 0)),
            pl.BlockSpec((1, d, tn), lambda l, j: (l, 0, j)),
            pl.BlockSpec((1, 1, tn), lambda l, j: (l, 0, j)),
        ],
        out_specs=pl.BlockSpec((1, rows, tn), lambda l, j: (l, 0, j)),
        out_shape=jax.ShapeDtypeStruct((depth, rows, n6), F32),
        name="ada_mod",
    )(cond, ada_w, ada_b.reshape(depth, 1, n6))


def _modulated_norm(x, g, shift, scale):
    ms = jnp.mean(x * x, axis=-1, keepdims=True)
    y = x * lax.rsqrt(ms + RMS_EPS) * g
    return y * (1.0 + scale) + shift


def _strip_masks(n_rows, cg):
    lane = lax.broadcasted_iota(jnp.int32, (n_rows, 128), 1)
    return [(lane >= j * cg) & (lane < (j + 1) * cg) for j in range(128 // cg)]


def _chunk_flatten(u_scr, z_ref, n_chunks, ngrp, cg):
    gpt = 128 // cg
    masks = _strip_masks(n_chunks, cg)
    for g in range(ngrp):
        k, i = divmod(g, gpt)
        for hf in range(S5_T // gpt):
            acc = None
            for j in range(gpt):
                src = u_scr[k, pl.ds(hf * gpt + j, n_chunks, stride=S5_T), :]
                shift = ((j - i) * cg) % 128
                if shift:
                    src = pltpu.roll(src, shift, axis=1)
                acc = src if acc is None else jnp.where(masks[j], src, acc)
            tile = g * (S5_T // gpt) + hf
            z_ref[0, :, tile * 128:(tile + 1) * 128] = acc.astype(z_ref.dtype)


def _chunk_unflatten(yz_ref, scr, n_chunks, ngrp, cg):
    gpt = 128 // cg
    masks = _strip_masks(n_chunks, cg)
    for k in range(ngrp // gpt):
        for hf in range(S5_T // gpt):
            srcs = []
            for i in range(gpt):
                tile = (k * gpt + i) * (S5_T // gpt) + hf
                srcs.append(yz_ref[0, :, tile * 128:(tile + 1) * 128].astype(F32))
            for j in range(gpt):
                acc = None
                for i in range(gpt):
                    shift = ((i - j) * cg) % 128
                    src = pltpu.roll(srcs[i], shift, axis=1) if shift else srcs[i]
                    acc = src if acc is None else jnp.where(masks[i], src, acc)
                t = hf * gpt + j
                scr[k, t * n_chunks:(t + 1) * n_chunks, :] = acc
    return jnp.concatenate(
        [jnp.concatenate([scr[k, pl.ds(ch, S5_T, stride=n_chunks), :] for k in range(scr.shape[0])], axis=1)
         for ch in range(n_chunks)], axis=0)


def _proj_in_kernel(hl_ref, hc_ref, mod_ref, g_ref, w_ref, hg_ref, z_ref, gate_ref, xn_scr, u_scr,
                    *, nt_lat, ngrp, cg):
    def norm(h_ref):
        xn_scr[...] = _modulated_norm(h_ref[0], g_ref[...], mod_ref[0, 0:1, :], mod_ref[0, 1:2, :]).astype(BF16)

    pl.when(pl.program_id(1) < nt_lat)(lambda: norm(hl_ref))
    pl.when(pl.program_id(1) >= nt_lat)(lambda: norm(hc_ref))
    xn = xn_scr[...]
    hg_w, s5_w = hg_ref.shape[2], ngrp * cg

    def project(o_ref, col0):
        width = o_ref.shape[2]
        step = math.gcd(512, width)
        for c in range(0, width, step):
            o_ref[0, :, c:c + step] = _dot(xn, w_ref[:, col0 + c:col0 + c + step]).astype(o_ref.dtype)

    u = _dot(xn, w_ref[:, hg_w:hg_w + s5_w])
    for k in range(s5_w // 128):
        u_scr[k] = u[:, k * 128:(k + 1) * 128]
    _chunk_flatten(u_scr, z_ref, xn_scr.shape[0] // S5_T, ngrp, cg)
    project(hg_ref, 0)
    project(gate_ref, hg_w + s5_w)


def _proj_in(h_lat, h_ctx, mods, norm_g, w_stack, layer, hg_cols, ngrp, cg):
    bsz, n_lat, d = h_lat.shape
    n_ctx = h_ctx.shape[1]
    total = n_lat + n_ctx
    nt_lat, nt_ctx = n_lat // TOK_TILE, n_ctx // TOK_TILE
    s5_w = ngrp * cg
    gate_cols = w_stack.shape[2] - hg_cols - s5_w
    ct = TOK_TILE // S5_T
    return pl.pallas_call(
        functools.partial(_proj_in_kernel, nt_lat=nt_lat, ngrp=ngrp, cg=cg),
        grid=(bsz, nt_lat + nt_ctx),
        in_specs=[
            pl.BlockSpec((1, TOK_TILE, d), lambda b, t: (b, jnp.minimum(t, nt_lat - 1), 0)),
            pl.BlockSpec((1, TOK_TILE, d), lambda b, t: (b, jnp.maximum(t - nt_lat, 0), 0)),
            pl.BlockSpec((1, 6, d), lambda b, t: (jnp.where(t < nt_lat, b, bsz), 0, 0)),
            pl.BlockSpec((1, d), lambda b, t: (0, 0)),
            _layer_spec(w_stack, layer),
        ],
        out_specs=[pl.BlockSpec((1, TOK_TILE, hg_cols), lambda b, t: (b, t, 0)),
                   pl.BlockSpec((1, ct, s5_w * S5_T), lambda b, t: (b, t, 0)),
                   pl.BlockSpec((1, TOK_TILE, gate_cols), lambda b, t: (b, t, 0))],
        out_shape=[jax.ShapeDtypeStruct((bsz, total, hg_cols), BF16),
                   jax.ShapeDtypeStruct((bsz, total // S5_T, s5_w * S5_T), BF16),
                   jax.ShapeDtypeStruct((bsz, total, gate_cols), BF16)],
        scratch_shapes=[pltpu.VMEM((TOK_TILE, d), BF16), pltpu.VMEM((s5_w // 128, TOK_TILE, 128), F32)],
        compiler_params=pltpu.CompilerParams(
            dimension_semantics=("parallel", "arbitrary"), vmem_limit_bytes=VMEM_LIMIT),
        name="proj_in",
    )(h_lat, h_ctx, mods, norm_g.reshape(1, d), w_stack)


def _gla_tables(c, rev):
    t = np.arange(c)[:, None]
    s = np.arange(c)[None, :]
    x = t ^ s
    lev = np.full((c, c), -2, np.int32)
    causal = (t < s) if rev else (t > s)
    with np.errstate(divide="ignore"):
        hb = np.floor(np.log2(np.maximum(x, 1))).astype(np.int32)
    lev = np.where(causal, hb, lev)
    lev = np.where(t == s, -1, lev).astype(np.int32)
    tri = ((t <= s) if rev else (t >= s)).astype(np.float32)
    return jnp.asarray(lev), jnp.asarray(tri, dtype=BF16)


def _gla_kernel(*refs, rev, readout, c, nheads, dk, state_only_steps):
    if readout:
        (q_ref, fz_ref, i_ref, lb_ref, lev_ref, tri_ref, g_ref, of_ref, ng_ref,
         out_ref, st_ref, b_scr_all) = refs
    else:
        q_ref, fz_ref, i_ref, lb_ref, lev_ref, tri_ref, out_ref, st_ref, b_scr_all = refs
    nlev = int(math.log2(c))

    @pl.when(pl.program_id(1) == 0)
    def _():
        st_ref[...] = jnp.zeros_like(st_ref)

    lev = lev_ref[...]
    tri = tri_ref[...]
    row = lax.broadcasted_iota(jnp.int32, (c, 1), 0)

    def head(hd, b_scr, full):
        sl = pl.ds(pl.multiple_of(hd * dk, dk), dk)
        z = fz_ref[0, :, sl].astype(F32)
        v = i_ref[0, :, sl]
        lb = lb_ref[0:1, sl]

        e = jnp.exp(-jnp.abs(z))
        den = 1.0 + e
        inv = 1.0 / den
        pos = z >= 0
        sig_pos = jnp.where(pos, 1.0, e) * inv
        sig_neg = jnp.where(pos, e, 1.0) * inv
        f = lb + (1.0 - lb) * sig_pos
        a = jnp.where(lb > 0, jnp.log(f), jnp.minimum(z, 0.0) - jnp.log(den)) * LOG2_E
        k = (1.0 - lb) * sig_neg

        a_hi = a.astype(BF16)
        a_lo = (a - a_hi.astype(F32)).astype(BF16)
        bb = _dot(tri, jnp.concatenate([a_hi, a_lo], axis=1))
        b = bb[:, :dk] + bb[:, dk:]
        b_scr[...] = b

        st = st_ref[hd]
        b_end = b_scr[pl.ds(0 if rev else c - 1, 1), :]
        k_out = (k * jnp.exp2(b_end - b)).astype(BF16)
        st_ref[hd] = jnp.exp2(b_end) * st + _tn_dot(v, k_out)
        if not full:
            out_ref[0, :, sl] = jnp.zeros((c, dk), out_ref.dtype)
            return
        qz = q_ref[0, :, sl].astype(F32)
        q = qz * _sigmoid(qz)

        odd = (row & 1) == 1
        f_prev = pltpu.roll(f, 1, axis=0)
        f_next = pltpu.roll(f, c - 1, axis=0)
        qf = q * f
        scores = jnp.zeros((c, c), F32)
        for level in range(nlev):
            half = 1 << level
            if level == 0:
                xq, xk = qf, k
            elif level == 1:
                if rev:
                    xq, xk = qf * jnp.where(odd, 1.0, f_next), k * jnp.where(odd, f_prev, 1.0)
                else:
                    xq, xk = qf * jnp.where(odd, f_prev, 1.0), k * jnp.where(odd, 1.0, f_next)
            else:
                blk = 2 * half
                pieces = []
                for j in range(c // blk):
                    brow = b_scr[pl.ds(j * blk + (half if rev else half - 1), 1), :]
                    pieces.append(jnp.broadcast_to(brow, (blk, dk)))
                ref_b = jnp.concatenate(pieces, axis=0) if len(pieces) > 1 else pieces[0]
                decay = jnp.exp2(-jnp.abs(b - ref_b))
                xq, xk = q * decay, k * decay
            scores = jnp.where(lev == level, _nt_dot(xq.astype(BF16), xk.astype(BF16)), scores)
        scores = jnp.where(lev == -1, jnp.sum(q * k, axis=-1, keepdims=True), scores)
        o = _dot(scores.astype(BF16), v) + _nt_dot((q * jnp.exp2(b)).astype(BF16), st.astype(BF16))

        if readout:
            o = o + of_ref[0, :, sl].astype(F32)
            o = o * lax.rsqrt(jnp.mean(o * o, axis=-1, keepdims=True) + RMS_EPS) * ng_ref[...]
            gz = g_ref[0, :, sl].astype(F32)
            o = o * (gz * _sigmoid(gz))
        out_ref[0, :, sl] = o.astype(out_ref.dtype)

    per_iter = b_scr_all.shape[0]

    def all_heads(full):
        def head_group(i, carry):
            for slot in range(per_iter):
                head(i * per_iter + slot, b_scr_all.at[slot], full)
            return carry

        lax.fori_loop(0, nheads // per_iter, head_group, 0)

    if state_only_steps:
        pl.when(pl.program_id(1) < state_only_steps)(lambda: all_heads(False))
        pl.when(pl.program_id(1) >= state_only_steps)(lambda: all_heads(True))
    else:
        all_heads(True)


def _gla(p_hg, lb_row, norm_g, o_fwd, n_lat, n_ctx, width, dk, rev, ctx_out):
    bsz, total, _ = p_hg.shape
    c = GLA_C
    nl, nc = n_lat // c, n_ctx // c
    nheads = width // dk
    readout = rev

    if rev:
        def chunk(i):
            return jnp.where(i < nc, nl + nc - 1 - i, nl + nc - 1 - i)
    else:
        def chunk(i):
            return jnp.where(i < nc, nl + i, i - nc)

    def col_spec(colblk):
        return pl.BlockSpec((1, c, width), lambda b, i: (b, chunk(i), colblk))

    lev, tri = _gla_tables(c, rev)
    in_specs = [col_spec(0), col_spec(2 if rev else 1), col_spec(3),
                pl.BlockSpec((1, width), lambda b, i: (0, 0)),
                pl.BlockSpec((c, c), lambda b, i: (0, 0)),
                pl.BlockSpec((c, c), lambda b, i: (0, 0))]
    args = [p_hg, p_hg, p_hg, lb_row.reshape(1, width), lev, tri]
    if readout:
        in_specs += [col_spec(4),
                     pl.BlockSpec((1, c, width), lambda b, i: (b, chunk(i), 0)),
                     pl.BlockSpec((1, dk), lambda b, i: (0, 0))]
        args += [p_hg, o_fwd, norm_g.reshape(1, dk)]
    return pl.pallas_call(
        functools.partial(_gla_kernel, rev=rev, readout=readout, c=c, nheads=nheads, dk=dk,
                          state_only_steps=0 if ctx_out else nc),
        grid=(bsz, nl + nc),
        in_specs=in_specs,
        out_specs=pl.BlockSpec((1, c, width), lambda b, i: (b, chunk(i), 0)),
        out_shape=jax.ShapeDtypeStruct((bsz, total, width), BF16),
        scratch_shapes=[pltpu.VMEM((nheads, dk, dk), F32), pltpu.VMEM((math.gcd(HEADS_PER_ITER, nheads), c, dk), F32)],
        compiler_params=pltpu.CompilerParams(
            dimension_semantics=("parallel", "arbitrary"), vmem_limit_bytes=VMEM_LIMIT),
        name="gla_bwd" if rev else "gla_fwd",
    )(*args)


def _s5_tables(a_re, a_im, log_dt, b_re, b_im, c_re, c_im):
    t = S5_T
    ngrp, nst, cg = b_re.shape
    a_re = jnp.minimum(a_re.astype(F32), S5_MAX_RE)
    a_im = a_im.astype(F32)
    dt = jnp.exp(log_dt.astype(F32))[..., None]
    mag = jnp.exp(dt * a_re)
    abr, abi = mag * jnp.cos(dt * a_im), mag * jnp.sin(dt * a_im)
    den = a_re * a_re + a_im * a_im
    nr, ni = abr - 1.0, abi
    f_re = ((nr * a_re + ni * a_im) / den)[..., None]
    f_im = ((ni * a_re - nr * a_im) / den)[..., None]
    b_re, b_im = b_re.astype(F32)[None], b_im.astype(F32)[None]
    bb_re = f_re * b_re - f_im * b_im
    bb_im = f_re * b_im + f_im * b_re
    pr, pi = [jnp.ones_like(abr)], [jnp.zeros_like(abr)]
    for _ in range(t):
        pr.append(pr[-1] * abr - pi[-1] * abi)
        pi.append(pr[-2] * abi + pi[-1] * abr)
    pr, pi = jnp.stack(pr), jnp.stack(pi)
    c_re, c_im = c_re.astype(F32), c_im.astype(F32)
    l_re = c_re[None, None] * pr[:, :, :, None, :] - c_im[None, None] * pi[:, :, :, None, :]
    l_im = c_re[None, None] * pi[:, :, :, None, :] + c_im[None, None] * pr[:, :, :, None, :]
    bt_re, bt_im = bb_re.transpose(0, 1, 3, 2)[None, :, :, None], bb_im.transpose(0, 1, 3, 2)[None, :, :, None]
    kern = jnp.sum(l_re[:t, :, :, :, None, :] * bt_re - l_im[:t, :, :, :, None, :] * bt_im, axis=-1)
    lag = np.arange(t)[:, None] - np.arange(t)[None, :]
    sel = np.stack([lag[:, :, None] == np.arange(t), -lag[:, :, None] == np.arange(t)])
    mt = jnp.einsum("rtsd,drgck->gsktc", jnp.asarray(sel, BF16), kern.astype(BF16),
                    preferred_element_type=F32).astype(BF16).reshape(ngrp, t * cg, t * cg)
    pf_r, pf_i = pr[t - 1 - np.arange(t), 0], pi[t - 1 - np.arange(t), 0]
    pb_r, pb_i = pr[np.arange(t), 1], pi[np.arange(t), 1]

    def drive(p_r, p_i, r):
        g_r = p_r[..., None] * bb_re[r][None] - p_i[..., None] * bb_im[r][None]
        g_i = p_r[..., None] * bb_im[r][None] + p_i[..., None] * bb_re[r][None]
        to = lambda x: x.transpose(1, 0, 3, 2).reshape(ngrp, t * cg, nst)
        return to(g_r), to(g_i)

    gf_r, gf_i = drive(pf_r, pf_i, 0)
    gb_r, gb_i = drive(pb_r, pb_i, 1)
    def read(idx, r):
        rr = l_re[idx, r].transpose(1, 3, 0, 2).reshape(ngrp, nst, t * cg)
        ri = -l_im[idx, r].transpose(1, 3, 0, 2).reshape(ngrp, nst, t * cg)
        return rr, ri

    of_r, of_i = read(np.arange(t) + 1, 0)
    ob_r, ob_i = read(t - np.arange(t), 1)

    eye = jnp.eye(2, dtype=BF16)
    g4 = jnp.stack([gf_r, gf_i, gb_r, gb_i], axis=2).astype(BF16).reshape(ngrp // 2, 2, t * cg, 4, 1, nst)
    gin = (g4 * eye[None, :, None, None, :, None]).reshape(ngrp // 2, 2 * t * cg, 8 * nst)
    o4 = jnp.stack([of_r, of_i, ob_r, ob_i], axis=1).astype(BF16).reshape(ngrp // 2, 2, 4, nst, 1, t * cg)
    o4 = o4.transpose(0, 2, 1, 3, 4, 5)
    gout = (o4 * eye[None, None, :, None, :, None]).reshape(ngrp // 2, 8 * nst, 2 * t * cg)
    a_chunk = jnp.stack([pr[t, 0], pi[t, 0], pr[t, 1], pi[t, 1]]).reshape(4, ngrp * nst)
    return mt, gin, gout, a_chunk


def _s5_kernel(z_ref, mt_ref, gin_ref, gout_ref, ac_ref, skip_ref, y_ref, d_scr, x_scr,
               *, n_lat, n_ctx, npairs, pw):
    nrows = n_lat + n_ctx
    gw = pw // 2
    tw = pw // 4
    for p in range(npairs):
        d = _dot(z_ref[0, :, p * pw:(p + 1) * pw], gin_ref[p])
        for j in range(4):
            d_scr[j, :, p * tw:(p + 1) * tw] = d[:, j * tw:(j + 1) * tw]

    def scan(plane, forward):
        ar = ac_ref[plane:plane + 1, :]
        ai = ac_ref[plane + 1:plane + 2, :]

        def body(i, carry):
            xr, xi = carry
            if forward:
                n = jnp.where(i < n_ctx, n_lat + i, i - n_ctx)
            else:
                n = nrows - 1 - i
            x_scr[plane, pl.ds(n, 1), :] = xr
            x_scr[plane + 1, pl.ds(n, 1), :] = xi
            dr = d_scr[plane, pl.ds(n, 1), :]
            di = d_scr[plane + 1, pl.ds(n, 1), :]
            return ar * xr - ai * xi + dr, ar * xi + ai * xr + di

        zero = jnp.zeros((1, d_scr.shape[2]), F32)
        lax.fori_loop(0, nrows, body, (zero, zero))

    scan(0, True)
    scan(2, False)

    for p in range(npairs):
        xp = jnp.concatenate([x_scr[j, :, p * tw:(p + 1) * tw] for j in range(4)], axis=1).astype(BF16)
        y = _dot(xp, gout_ref[p])
        for g in range(2):
            lo = p * pw + g * gw
            zg = z_ref[0, :, lo:lo + gw]
            yg = y[:, g * gw:(g + 1) * gw] + _dot(zg, mt_ref[2 * p + g]) + skip_ref[:, lo:lo + gw] * zg.astype(F32)
            y_ref[0, :, lo:lo + gw] = yg.astype(y_ref.dtype)


def _s5(zf, tables, layer, skip_flat, n_lat_chunks, n_ctx_chunks):
    bsz, nrows, wide = zf.shape
    mt, gin, gout, a_chunk = tables
    npairs, pw = gin.shape[1:3]
    planes = a_chunk.shape[2]
    return pl.pallas_call(
        functools.partial(_s5_kernel, n_lat=n_lat_chunks, n_ctx=n_ctx_chunks, npairs=npairs, pw=pw),
        grid=(bsz,),
        in_specs=[pl.BlockSpec((1, nrows, wide), lambda b: (b, 0, 0)),
                  _layer_spec(mt, layer), _layer_spec(gin, layer), _layer_spec(gout, layer),
                  _layer_spec(a_chunk, layer), _const_spec(skip_flat.shape)],
        out_specs=pl.BlockSpec((1, nrows, wide), lambda b: (b, 0, 0)),
        out_shape=jax.ShapeDtypeStruct((bsz, nrows, wide), BF16),
        scratch_shapes=[pltpu.VMEM((4, nrows, planes), F32), pltpu.VMEM((4, nrows, planes), F32)],
        compiler_params=pltpu.CompilerParams(
            dimension_semantics=("parallel",), vmem_limit_bytes=VMEM_LIMIT),
        name="s5",
    )(zf, mt, gin, gout, a_chunk, skip_flat)


def _gelu_tanh(x):
    return 0.5 * x * (1.0 + jnp.tanh(math.sqrt(2.0 / math.pi) * (x + 0.044715 * (x * x * x))))


def _merge_kernel(h_ref, yh_ref, yz_ref, gate_ref, mod_ref, whg_ref, wglu_ref, wbr_ref, wout_ref, o_ref, y_scr,
                  *, ngrp, cg):
    d = h_ref.shape[2]
    y_hg = _dot(yh_ref[0], whg_ref[...])
    ys = _gelu_tanh(_chunk_unflatten(yz_ref, y_scr, yz_ref.shape[1], ngrp, cg))
    glu = _sigmoid(_dot(ys.astype(BF16), wglu_ref[...]))
    y_s5 = _dot((ys * glu).astype(BF16), wbr_ref[...])
    gate = gate_ref[0].astype(F32)
    merged = _sigmoid(gate[:, :d]) * y_hg + _sigmoid(gate[:, d:]) * y_s5
    o_ref[0] = h_ref[0] + mod_ref[0, 2:3, :] * _dot(merged.astype(BF16), wout_ref[...])


def _merge(h, yh, yz, p_gate, mods, mod_row_fn, whg, wglu, wbr, wout, layer, ngrp, cg, tok_off_blocks):
    bsz, length, d = h.shape
    nt = length // TOK_TILE
    ct = TOK_TILE // S5_T
    seq = lambda rows, w: pl.BlockSpec((1, rows, w), lambda b, t: (b, t + tok_off_blocks, 0))
    return pl.pallas_call(
        functools.partial(_merge_kernel, ngrp=ngrp, cg=cg),
        grid=(bsz, nt),
        in_specs=[pl.BlockSpec((1, TOK_TILE, d), lambda b, t: (b, t, 0)),
                  seq(TOK_TILE, yh.shape[2]), seq(ct, yz.shape[2]), seq(TOK_TILE, p_gate.shape[2]),
                  pl.BlockSpec((1, 6, d), lambda b, t: (mod_row_fn(b), 0, 0)),
                  _layer_spec(whg, layer), _layer_spec(wglu, layer), _layer_spec(wbr, layer),
                  _layer_spec(wout, layer)],
        out_specs=pl.BlockSpec((1, TOK_TILE, d), lambda b, t: (b, t, 0)),
        out_shape=jax.ShapeDtypeStruct(h.shape, F32),
        scratch_shapes=[pltpu.VMEM((ngrp * cg // 128, TOK_TILE, 128), F32)],
        input_output_aliases={0: 0} if layer > 0 else {},
        compiler_params=pltpu.CompilerParams(
            dimension_semantics=("parallel", "parallel"), vmem_limit_bytes=VMEM_LIMIT),
        name="merge",
    )(h, yh, yz, p_gate, mods, whg, wglu, wbr, wout)


def _ffn_kernel(h_ref, mod_ref, g_ref, wa_ref, wb_ref, cw_ref, cb_ref, wd_ref, fg_ref,
                o_ref, xn_scr, up_scr, act_scr, *, n_tok, width, vertical, cw, mrows, final):
    j = pl.program_id(1)
    nj = pl.num_programs(1)
    nblk = n_tok // mrows
    rpb = mrows // width
    pad = width

    @pl.when(j == 0)
    def _():
        def norm_body(i, carry):
            r0 = pl.multiple_of(i * mrows, mrows)
            xn_scr[pl.ds(r0, mrows), :] = _modulated_norm(
                h_ref[0, pl.ds(r0, mrows), :], g_ref[...], mod_ref[0, 3:4, :], mod_ref[0, 4:5, :]).astype(BF16)
            o_ref[0, pl.ds(r0, mrows), :] = jnp.zeros((mrows, o_ref.shape[2]), F32)
            return carry

        lax.fori_loop(0, nblk, norm_body, 0)
        xn_scr[n_tok:n_tok + width, :] = jnp.zeros((width, xn_scr.shape[1]), BF16)
        for copy in range(3):
            up_scr[copy, 0:pad, :] = jnp.zeros((pad, 2 * cw), F32)
            up_scr[copy, pad + n_tok:pad + n_tok + width, :] = jnp.zeros((width, 2 * cw), F32)

    def up_rows(row0, m):
        xn = xn_scr[pl.ds(row0, m), :]
        up = jnp.concatenate([_dot(xn, wa_ref[...]), _dot(xn, wb_ref[...])], axis=1)
        col = lax.broadcasted_iota(jnp.int32, (m, 1), 0) % width
        up_scr[0, pl.ds(pad + row0, m), :] = jnp.where(col == 0, 0.0, pltpu.roll(up, 1, axis=0))
        up_scr[1, pl.ds(pad + row0, m), :] = up
        up_scr[2, pl.ds(pad + row0, m), :] = jnp.where(col == width - 1, 0.0, pltpu.roll(up, m - 1, axis=0))

    def conv_row(row0, lane0):
        acc = cb_ref[:, lane0:lane0 + 128]
        for dr in ((-1, 0, 1) if vertical else (0,)):
            for dc in (-1, 0, 1):
                tap = (dr + 1) * 3 + (dc + 1)
                src = up_scr[dc + 1, pl.ds(pl.multiple_of(pad + row0 + dr * width, 8), width), lane0:lane0 + 128]
                acc = acc + cw_ref[tap:tap + 1, lane0:lane0 + 128] * src
        return acc

    def conv_block(blk):
        r0 = blk * mrows
        for lt in range(cw // 128):
            for r in range(rpb):
                row0 = r0 + r * width
                ca = conv_row(row0, lt * 128)
                cb = conv_row(row0, cw + lt * 128)
                act_scr[pl.ds(pl.multiple_of(row0, 16), width), lt * 128:(lt + 1) * 128] = (
                    ca * _sigmoid(ca) * cb).astype(BF16)

    def down_rows(row0):
        o_ref[0, pl.ds(row0, mrows), :] += _dot(act_scr[pl.ds(row0, mrows), :], wd_ref[...])

    if nblk == 1:
        up_rows(0, mrows)
        conv_block(0)
        down_rows(0)
    else:
        streams = math.gcd(FFN_STREAMS, nblk)
        per = nblk // streams
        ahead = width if vertical else 0
        if vertical:
            up_rows(0, width)
            for s in range(1, streams):
                up_rows(s * per * mrows - width, 2 * width)

        def body(k, carry):
            for s in range(streams):
                r0 = pl.multiple_of((k + s * per) * mrows, mrows)
                up_rows(r0 + ahead, mrows)
            for s in range(streams):
                conv_block(k + s * per)
            for s in range(streams):
                down_rows(pl.multiple_of((k + s * per) * mrows, mrows))
            return carry

        lax.fori_loop(0, per, body, 0)

    @pl.when(j == nj - 1)
    def _():
        def fin_body(i, carry):
            r0 = pl.multiple_of(i * mrows, mrows)
            out = h_ref[0, pl.ds(r0, mrows), :] + mod_ref[0, 5:6, :] * o_ref[0, pl.ds(r0, mrows), :]
            if final:
                out = out * lax.rsqrt(jnp.mean(out * out, axis=-1, keepdims=True) + RMS_EPS) * fg_ref[...]
            o_ref[0, pl.ds(r0, mrows), :] = out
            return carry

        lax.fori_loop(0, nblk, fin_body, 0)


def _ffn(h, mods, mod_row_fn, norm_g, ffn_w, layer, final_g, width, vertical, final):
    bsz, n_tok, d = h.shape
    w_pad, cwf, cbf, wd = ffn_w
    cw = FFN_TILE
    nj = wd.shape[1] // cw
    return pl.pallas_call(
        functools.partial(_ffn_kernel, n_tok=n_tok, width=width, vertical=vertical, cw=cw, mrows=TOK_TILE,
                          final=final),
        grid=(bsz, nj),
        in_specs=[pl.BlockSpec((1, n_tok, d), lambda b, j: (b, 0, 0), pipeline_mode=pl.Buffered(1)),
                  pl.BlockSpec((1, 6, d), lambda b, j: (mod_row_fn(b), 0, 0)),
                  pl.BlockSpec((1, d), lambda b, j: (0, 0)),
                  pl.BlockSpec((None, d, cw), lambda b, j: (layer, 0, j)),
                  pl.BlockSpec((None, d, cw), lambda b, j: (layer, 0, nj + j)),
                  pl.BlockSpec((None, 9, 2 * cw), lambda b, j: (layer, 0, j)),
                  pl.BlockSpec((None, 1, 2 * cw), lambda b, j: (layer, 0, j)),
                  pl.BlockSpec((None, cw, d), lambda b, j: (layer, j, 0)),
                  pl.BlockSpec((1, d), lambda b, j: (0, 0))],
        out_specs=pl.BlockSpec((1, n_tok, d), lambda b, j: (b, 0, 0)),
        out_shape=jax.ShapeDtypeStruct(h.shape, F32),
        scratch_shapes=[pltpu.VMEM((n_tok + width, d), BF16),
                        pltpu.VMEM((3, n_tok + 2 * width, 2 * cw), F32),
                        pltpu.VMEM((n_tok, cw), BF16)],
        input_output_aliases={0: 0},
        compiler_params=pltpu.CompilerParams(
            dimension_semantics=("parallel", "arbitrary"), vmem_limit_bytes=VMEM_LIMIT),
        name="ffn",
    )(h, mods, norm_g.reshape(1, d), w_pad, w_pad, cwf, cbf, wd, final_g.reshape(1, d))


def _ffn_weights(w_up, conv_w, conv_b, w_down):
    depth, ffn, _ = w_down.shape
    cw = FFN_TILE
    fp = ((ffn + cw - 1) // cw) * cw
    nj = fp // cw

    def halves(x):
        padw = [(0, 0)] * (x.ndim - 1) + [(0, fp - ffn)]
        return jnp.pad(x[..., :ffn], padw), jnp.pad(x[..., ffn:], padw)

    def tiles(x):
        lead = x.shape[:-1]
        parts = [part.reshape(lead + (nj, cw)) for part in halves(x)]
        return jnp.concatenate(parts, axis=-1).reshape(lead + (nj * 2 * cw,))

    wd = jnp.pad(w_down.astype(BF16), [(0, 0), (0, fp - ffn), (0, 0)])
    w_pad = jnp.concatenate(halves(w_up.astype(BF16)), axis=-1)
    return w_pad, tiles(conv_w.reshape(depth, 9, 2 * ffn)), tiles(conv_b.reshape(depth, 1, 2 * ffn)), wd


def _lower_bounds(lb_raw):
    p = jax.nn.softmax(lb_raw.astype(F32), axis=0)
    cs = jnp.cumsum(p, axis=0)
    return cs - cs[0:1]


def kernel(x, c, ctx, c_ctx, ada_w, ada_b, norm1_g, w_in, hg_lb_raw, hg_norm_g, w_hg_br, s5_a_re, s5_a_im, s5_log_dt, s5_b_re, s5_b_im, s5_c_re, s5_c_im, s5_d, w_s5_glu, w_s5_br, w_out, norm2_g, w_up, ffn_conv_w, ffn_conv_b, w_down, final_g):
    bsz, n_lat, d = x.shape
    n_ctx = ctx.shape[1]
    depth = ada_w.shape[0]
    dk = hg_norm_g.shape[1]
    hg_w = hg_lb_raw.shape[2]
    ngrp, nst, cg = s5_b_re.shape[1:]
    assert n_lat % TOK_TILE == 0 and n_ctx % TOK_TILE == 0 and n_lat % GLA_C == 0 and n_ctx % GLA_C == 0
    assert 5 * hg_w + ngrp * cg + 2 * d == w_in.shape[2] and S5_T * cg == 256 and 128 % cg == 0
    assert ngrp % (128 // cg) == 0

    rows = ((bsz + 1 + 7) // 8) * 8
    cond = jnp.zeros((rows, d), F32).at[:bsz].set(c).at[bsz].set(c_ctx)
    mods = _ada_mod(cond, ada_w, ada_b).reshape(depth, rows, 6, d)
    lat_row = lambda b: b
    ctx_row = lambda b: bsz
    lbs = _lower_bounds(hg_lb_raw)
    lat_blk = n_lat // TOK_TILE

    w_in_b, whg_b, wglu_b, wbr_b, wout_b = (w.astype(BF16) for w in (w_in, w_hg_br, w_s5_glu, w_s5_br, w_out))
    ffn_w = _ffn_weights(w_up, ffn_conv_w, ffn_conv_b, w_down)
    s5_tabs = jax.vmap(_s5_tables)(s5_a_re, s5_a_im, s5_log_dt, s5_b_re, s5_b_im, s5_c_re, s5_c_im)
    skip = jnp.broadcast_to(s5_d.astype(F32).reshape(depth, ngrp, 1, cg), (depth, ngrp, S5_T, cg))
    skip = skip.reshape(depth, 1, ngrp * S5_T * cg)

    h_lat, h_ctx = x, ctx
    for l in range(depth):
        last = l == depth - 1
        p_hg, z, p_gate = _proj_in(h_lat, h_ctx, mods[l], norm1_g[l], w_in_b, l, 5 * hg_w, ngrp, cg)

        o_fwd = _gla(p_hg, lbs[l, 0], hg_norm_g[l], None, n_lat, n_ctx, hg_w, dk, rev=False, ctx_out=not last)
        yh = _gla(p_hg, lbs[l, 1], hg_norm_g[l], o_fwd, n_lat, n_ctx, hg_w, dk, rev=True, ctx_out=not last)
        yz = _s5(z, s5_tabs, l, skip[l], n_lat // S5_T, n_ctx // S5_T)

        wts = (whg_b, wglu_b, wbr_b, wout_b, l, ngrp, cg)
        h_lat = _merge(h_lat, yh, yz, p_gate, mods[l], lat_row, *wts, 0)
        h_lat = _ffn(h_lat, mods[l], lat_row, norm2_g[l], ffn_w, l, final_g, GRID_W, True, last)
        if not last:
            h_ctx = _merge(h_ctx, yh, yz, p_gate, mods[l], ctx_row, *wts, lat_blk)
            stack = math.gcd(bsz, max(1, n_lat // n_ctx))
            h_ctx = _ffn(h_ctx.reshape(bsz // stack, stack * n_ctx, d), mods[l], ctx_row, norm2_g[l], ffn_w, l,
                         final_g, n_ctx, False, False).reshape(bsz, n_ctx, d)
    return h_lat
```

```python
import functools
import math

import numpy as np
import jax
import jax.numpy as jnp
from jax import lax
from jax.experimental import pallas as pl
from jax.experimental.pallas import tpu as pltpu

GRID_W = 64
RMS_EPS = 1e-6
S5_MAX_RE = -1e-4
LOG2_E = 1.4426950408889634
S5_T = 16
GLA_C = 128
HEADS_PER_ITER = 8
TOK_TILE = 256
FFN_TILE = 256
FFN_STREAMS = 8
VMEM_LIMIT = 56 * 1024 * 1024

F32 = jnp.float32
BF16 = jnp.bfloat16
_HI = lax.Precision.HIGHEST


def _nt_dot(a, b):
    return lax.dot_general(a, b, (((1,), (1,)), ((), ())), preferred_element_type=F32)


def _tn_dot(a, b):
    return lax.dot_general(a, b, (((0,), (0,)), ((), ())), preferred_element_type=F32)


def _dot(a, b):
    return jnp.dot(a, b, preferred_element_type=F32)


def _sigmoid(x):
    return 1.0 / (1.0 + jnp.exp(-x))


def _const_spec(shape):
    nd = len(shape)
    return pl.BlockSpec(shape, lambda *_: (0,) * nd, pipeline_mode=pl.Buffered(1))


def _layer_spec(stacked, layer):
    rest = stacked.shape[1:]
    return pl.BlockSpec((None,) + rest, lambda *_: (layer,) + (0,) * len(rest), pipeline_mode=pl.Buffered(1))


def _ada_kernel(cond_ref, w_ref, b_ref, o_ref):
    cnd = cond_ref[...]
    s = cnd * _sigmoid(cnd)
    o_ref[0] = jnp.dot(s, w_ref[0], preferred_element_type=F32, precision=_HI) + b_ref[0]


def _ada_mod(cond, ada_w, ada_b):
    depth, d, n6 = ada_w.shape
    rows = cond.shape[0]
    tn = 512
    return pl.pallas_call(
        _ada_kernel,
        grid=(depth, n6 // tn),
        in_specs=[
            pl.BlockSpec((rows, d), lambda l, j: (0, 0)),
            pl.BlockSpec((1, d, tn), lambda l, j: (l, 0, j)),
            pl.BlockSpec((1, 1, tn), lambda l, j: (l, 0, j)),
        ],
        out_specs=pl.BlockSpec((1, rows, tn), lambda l, j: (l, 0, j)),
        out_shape=jax.ShapeDtypeStruct((depth, rows, n6), F32),
        name="ada_mod",
    )(cond, ada_w, ada_b.reshape(depth, 1, n6))


def _modulated_norm(x, g, shift, scale):
    ms = jnp.mean(x * x, axis=-1, keepdims=True)
    y = x * lax.rsqrt(ms + RMS_EPS) * g
    return y * (1.0 + scale) + shift


def _strip_masks(n_rows, cg):
    lane = lax.broadcasted_iota(jnp.int32, (n_rows, 128), 1)
    return [(lane >= j * cg) & (lane < (j + 1) * cg) for j in range(128 // cg)]


def _chunk_flatten(u_scr, z_ref, n_chunks, ngrp, cg):
    gpt = 128 // cg
    masks = _strip_masks(n_chunks, cg)
    for g in range(ngrp):
        k, i = divmod(g, gpt)
        for hf in range(S5_T // gpt):
            acc = None
            for j in range(gpt):
                src = u_scr[k, pl.ds(hf * gpt + j, n_chunks, stride=S5_T), :]
                shift = ((j - i) * cg) % 128
                if shift:
                    src = pltpu.roll(src, shift, axis=1)
                acc = src if acc is None else jnp.where(masks[j], src, acc)
            tile = g * (S5_T // gpt) + hf
            z_ref[0, :, tile * 128:(tile + 1) * 128] = acc.astype(z_ref.dtype)


def _chunk_unflatten(yz_ref, scr, n_chunks, ngrp, cg):
    gpt = 128 // cg
    masks = _strip_masks(n_chunks, cg)
    for k in range(ngrp // gpt):
        for hf in range(S5_T // gpt):
            srcs = []
            for i in range(gpt):
                tile = (k * gpt + i) * (S5_T // gpt) + hf
                srcs.append(yz_ref[0, :, tile * 128:(tile + 1) * 128].astype(F32))
            for j in range(gpt):
                acc = None
                for i in range(gpt):
                    shift = ((i - j) * cg) % 128
                    src = pltpu.roll(srcs[i], shift, axis=1) if shift else srcs[i]
                    acc = src if acc is None else jnp.where(masks[i], src, acc)
                t = hf * gpt + j
                scr[k, t * n_chunks:(t + 1) * n_chunks, :] = acc
    return jnp.concatenate(
        [jnp.concatenate([scr[k, pl.ds(ch, S5_T, stride=n_chunks), :] for k in range(scr.shape[0])], axis=1)
         for ch in range(n_chunks)], axis=0)


def _proj_in_kernel(hl_ref, hc_ref, mod_ref, g_ref, w_ref, hg_ref, z_ref, gate_ref, xn_scr, u_scr,
                    *, nt_lat, ngrp, cg):
    def norm(h_ref):
        xn_scr[...] = _modulated_norm(h_ref[0], g_ref[...], mod_ref[0, 0:1, :], mod_ref[0, 1:2, :]).astype(BF16)

    pl.when(pl.program_id(1) < nt_lat)(lambda: norm(hl_ref))
    pl.when(pl.program_id(1) >= nt_lat)(lambda: norm(hc_ref))
    xn = xn_scr[...]
    hg_w, s5_w = hg_ref.shape[2], ngrp * cg

    def project(o_ref, col0):
        width = o_ref.shape[2]
        step = math.gcd(512, width)
        for c in range(0, width, step):
            o_ref[0, :, c:c + step] = _dot(xn, w_ref[:, col0 + c:col0 + c + step]).astype(o_ref.dtype)

    u = _dot(xn, w_ref[:, hg_w:hg_w + s5_w])
    for k in range(s5_w // 128):
        u_scr[k] = u[:, k * 128:(k + 1) * 128]
    _chunk_flatten(u_scr, z_ref, xn_scr.shape[0] // S5_T, ngrp, cg)
    project(hg_ref, 0)
    project(gate_ref, hg_w + s5_w)


def _proj_in(h_lat, h_ctx, mods, norm_g, w_stack, layer, hg_cols, ngrp, cg):
    bsz, n_lat, d = h_lat.shape
    n_ctx = h_ctx.shape[1]
    total = n_lat + n_ctx
    nt_lat, nt_ctx = n_lat // TOK_TILE, n_ctx // TOK_TILE
    s5_w = ngrp * cg
    gate_cols = w_stack.shape[2] - hg_cols - s5_w
    ct = TOK_TILE // S5_T
    return pl.pallas_call(
        functools.partial(_proj_in_kernel, nt_lat=nt_lat, ngrp=ngrp, cg=cg),
        grid=(bsz, nt_lat + nt_ctx),
        in_specs=[
            pl.BlockSpec((1, TOK_TILE, d), lambda b, t: (b, jnp.minimum(t, nt_lat - 1), 0)),
            pl.BlockSpec((1, TOK_TILE, d), lambda b, t: (b, jnp.maximum(t - nt_lat, 0), 0)),
            pl.BlockSpec((1, 6, d), lambda b, t: (jnp.where(t < nt_lat, b, bsz), 0, 0)),
            pl.BlockSpec((1, d), lambda b, t: (0, 0)),
            _layer_spec(w_stack, layer),
        ],
        out_specs=[pl.BlockSpec((1, TOK_TILE, hg_cols), lambda b, t: (b, t, 0)),
                   pl.BlockSpec((1, ct, s5_w * S5_T), lambda b, t: (b, t, 0)),
                   pl.BlockSpec((1, TOK_TILE, gate_cols), lambda b, t: (b, t, 0))],
        out_shape=[jax.ShapeDtypeStruct((bsz, total, hg_cols), BF16),
                   jax.ShapeDtypeStruct((bsz, total // S5_T, s5_w * S5_T), BF16),
                   jax.ShapeDtypeStruct((bsz, total, gate_cols), BF16)],
        scratch_shapes=[pltpu.VMEM((TOK_TILE, d), BF16), pltpu.VMEM((s5_w // 128, TOK_TILE, 128), F32)],
        compiler_params=pltpu.CompilerParams(
            dimension_semantics=("parallel", "arbitrary"), vmem_limit_bytes=VMEM_LIMIT),
        name="proj_in",
    )(h_lat, h_ctx, mods, norm_g.reshape(1, d), w_stack)


def _gla_tables(c, rev):
    t = np.arange(c)[:, None]
    s = np.arange(c)[None, :]
    x = t ^ s
    lev = np.full((c, c), -2, np.int32)
    causal = (t < s) if rev else (t > s)
    with np.errstate(divide="ignore"):
        hb = np.floor(np.log2(np.maximum(x, 1))).astype(np.int32)
    lev = np.where(causal, hb, lev)
    lev = np.where(t == s, -1, lev).astype(np.int32)
    tri = ((t <= s) if rev else (t >= s)).astype(np.float32)
    return jnp.asarray(lev), jnp.asarray(tri, dtype=BF16)


def _gla_kernel(*refs, rev, readout, c, nheads, dk, state_only_steps):
    if readout:
        (q_ref, fz_ref, i_ref, lb_ref, lev_ref, tri_ref, g_ref, of_ref, ng_ref,
         out_ref, st_ref, b_scr_all) = refs
    else:
        q_ref, fz_ref, i_ref, lb_ref, lev_ref, tri_ref, out_ref, st_ref, b_scr_all = refs
    nlev = int(math.log2(c))

    @pl.when(pl.program_id(1) == 0)
    def _():
        st_ref[...] = jnp.zeros_like(st_ref)

    lev = lev_ref[...]
    tri = tri_ref[...]
    row = lax.broadcasted_iota(jnp.int32, (c, 1), 0)

    def head(hd, b_scr, full):
        sl = pl.ds(pl.multiple_of(hd * dk, dk), dk)
        z = fz_ref[0, :, sl].astype(F32)
        v = i_ref[0, :, sl]
        lb = lb_ref[0:1, sl]

        e = jnp.exp(-jnp.abs(z))
        den = 1.0 + e
        inv = 1.0 / den
        pos = z >= 0
        sig_pos = jnp.where(pos, 1.0, e) * inv
        sig_neg = jnp.where(pos, e, 1.0) * inv
        f = lb + (1.0 - lb) * sig_pos
        a = jnp.where(lb > 0, jnp.log(f), jnp.minimum(z, 0.0) - jnp.log(den)) * LOG2_E
        k = (1.0 - lb) * sig_neg

        a_hi = a.astype(BF16)
        a_lo = (a - a_hi.astype(F32)).astype(BF16)
        bb = _dot(tri, jnp.concatenate([a_hi, a_lo], axis=1))
        b = bb[:, :dk] + bb[:, dk:]
        b_scr[...] = b

        st = st_ref[hd]
        b_end = b_scr[pl.ds(0 if rev else c - 1, 1), :]
        k_out = (k * jnp.exp2(b_end - b)).astype(BF16)
        st_ref[hd] = jnp.exp2(b_end) * st + _tn_dot(v, k_out)
        if not full:
            out_ref[0, :, sl] = jnp.zeros((c, dk), out_ref.dtype)
            return
        qz = q_ref[0, :, sl].astype(F32)
        q = qz * _sigmoid(qz)

        odd = (row & 1) == 1
        f_prev = pltpu.roll(f, 1, axis=0)
        f_next = pltpu.roll(f, c - 1, axis=0)
        qf = q * f
        scores = jnp.zeros((c, c), F32)
        for level in range(nlev):
            half = 1 << level
            if level == 0:
                xq, xk = qf, k
            elif level == 1:
                if rev:
                    xq, xk = qf * jnp.where(odd, 1.0, f_next), k * jnp.where(odd, f_prev, 1.0)
                else:
                    xq, xk = qf * jnp.where(odd, f_prev, 1.0), k * jnp.where(odd, 1.0, f_next)
            else:
                blk = 2 * half
                pieces = []
                for j in range(c // blk):
                    brow = b_scr[pl.ds(j * blk + (half if rev else half - 1), 1), :]
                    pieces.append(jnp.broadcast_to(brow, (blk, dk)))
                ref_b = jnp.concatenate(pieces, axis=0) if len(pieces) > 1 else pieces[0]
                decay = jnp.exp2(-jnp.abs(b - ref_b))
                xq, xk = q * decay, k * decay
            scores = jnp.where(lev == level, _nt_dot(xq.astype(BF16), xk.astype(BF16)), scores)
        scores = jnp.where(lev == -1, jnp.sum(q * k, axis=-1, keepdims=True), scores)
        o = _dot(scores.astype(BF16), v) + _nt_dot((q * jnp.exp2(b)).astype(BF16), st.astype(BF16))

        if readout:
            o = o + of_ref[0, :, sl].astype(F32)
            o = o * lax.rsqrt(jnp.mean(o * o, axis=-1, keepdims=True) + RMS_EPS) * ng_ref[...]
            gz = g_ref[0, :, sl].astype(F32)
            o = o * (gz * _sigmoid(gz))
        out_ref[0, :, sl] = o.astype(out_ref.dtype)

    per_iter = b_scr_all.shape[0]

    def all_heads(full):
        def head_group(i, carry):
            for slot in range(per_iter):
                head(i * per_iter + slot, b_scr_all.at[slot], full)
            return carry

        lax.fori_loop(0, nheads // per_iter, head_group, 0)

    if state_only_steps:
        pl.when(pl.program_id(1) < state_only_steps)(lambda: all_heads(False))
        pl.when(pl.program_id(1) >= state_only_steps)(lambda: all_heads(True))
    else:
        all_heads(True)


def _gla(p_hg, lb_row, norm_g, o_fwd, n_lat, n_ctx, width, dk, rev, ctx_out):
    bsz, total, _ = p_hg.shape
    c = GLA_C
    nl, nc = n_lat // c, n_ctx // c
    nheads = width // dk
    readout = rev

    if rev:
        def chunk(i):
            return jnp.where(i < nc, nl + nc - 1 - i, nl + nc - 1 - i)
    else:
        def chunk(i):
            return jnp.where(i < nc, nl + i, i - nc)

    def col_spec(colblk):
        return pl.BlockSpec((1, c, width), lambda b, i: (b, chunk(i), colblk))

    lev, tri = _gla_tables(c, rev)
    in_specs = [col_spec(0), col_spec(2 if rev else 1), col_spec(3),
                pl.BlockSpec((1, width), lambda b, i: (0, 0)),
                pl.BlockSpec((c, c), lambda b, i: (0, 0)),
                pl.BlockSpec((c, c), lambda b, i: (0, 0))]
    args = [p_hg, p_hg, p_hg, lb_row.reshape(1, width), lev, tri]
    if readout:
        in_specs += [col_spec(4),
                     pl.BlockSpec((1, c, width), lambda b, i: (b, chunk(i), 0)),
                     pl.BlockSpec((1, dk), lambda b, i: (0, 0))]
        args += [p_hg, o_fwd, norm_g.reshape(1, dk)]
    return pl.pallas_call(
        functools.partial(_gla_kernel, rev=rev, readout=readout, c=c, nheads=nheads, dk=dk,
                          state_only_steps=0 if ctx_out else nc),
        grid=(bsz, nl + nc),
        in_specs=in_specs,
        out_specs=pl.BlockSpec((1, c, width), lambda b, i: (b, chunk(i), 0)),
        out_shape=jax.ShapeDtypeStruct((bsz, total, width), BF16),
        scratch_shapes=[pltpu.VMEM((nheads, dk, dk), F32), pltpu.VMEM((math.gcd(HEADS_PER_ITER, nheads), c, dk), F32)],
        compiler_params=pltpu.CompilerParams(
            dimension_semantics=("parallel", "arbitrary"), vmem_limit_bytes=VMEM_LIMIT),
        name="gla_bwd" if rev else "gla_fwd",
    )(*args)


def _s5_tables(a_re, a_im, log_dt, b_re, b_im, c_re, c_im):
    t = S5_T
    ngrp, nst, cg = b_re.shape
    a_re = jnp.minimum(a_re.astype(F32), S5_MAX_RE)
    a_im = a_im.astype(F32)
    dt = jnp.exp(log_dt.astype(F32))[..., None]
    mag = jnp.exp(dt * a_re)
    abr, abi = mag * jnp.cos(dt * a_im), mag * jnp.sin(dt * a_im)
    den = a_re * a_re + a_im * a_im
    nr, ni = abr - 1.0, abi
    f_re = ((nr * a_re + ni * a_im) / den)[..., None]
    f_im = ((ni * a_re - nr * a_im) / den)[..., None]
    b_re, b_im = b_re.astype(F32)[None], b_im.astype(F32)[None]
    bb_re = f_re * b_re - f_im * b_im
    bb_im = f_re * b_im + f_im * b_re
    pr, pi = [jnp.ones_like(abr)], [jnp.zeros_like(abr)]
    for _ in range(t):
        pr.append(pr[-1] * abr - pi[-1] * abi)
        pi.append(pr[-2] * abi + pi[-1] * abr)
    pr, pi = jnp.stack(pr), jnp.stack(pi)
    c_re, c_im = c_re.astype(F32), c_im.astype(F32)
    l_re = c_re[None, None] * pr[:, :, :, None, :] - c_im[None, None] * pi[:, :, :, None, :]
    l_im = c_re[None, None] * pi[:, :, :, None, :] + c_im[None, None] * pr[:, :, :, None, :]
    bt_re, bt_im = bb_re.transpose(0, 1, 3, 2)[None, :, :, None], bb_im.transpose(0, 1, 3, 2)[None, :, :, None]
    kern = jnp.sum(l_re[:t, :, :, :, None, :] * bt_re - l_im[:t, :, :, :, None, :] * bt_im, axis=-1)
    lag = np.arange(t)[:, None] - np.arange(t)[None, :]
    sel = np.stack([lag[:, :, None] == np.arange(t), -lag[:, :, None] == np.arange(t)])
    mt = jnp.einsum("rtsd,drgck->gsktc", jnp.asarray(sel, BF16), kern.astype(BF16),
                    preferred_element_type=F32).astype(BF16).reshape(ngrp, t * cg, t * cg)
    pf_r, pf_i = pr[t - 1 - np.arange(t), 0], pi[t - 1 - np.arange(t), 0]
    pb_r, pb_i = pr[np.arange(t), 1], pi[np.arange(t), 1]

    def drive(p_r, p_i, r):
        g_r = p_r[..., None] * bb_re[r][None] - p_i[..., None] * bb_im[r][None]
        g_i = p_r[..., None] * bb_im[r][None] + p_i[..., None] * bb_re[r][None]
        to = lambda x: x.transpose(1, 0, 3, 2).reshape(ngrp, t * cg, nst)
        return to(g_r), to(g_i)

    gf_r, gf_i = drive(pf_r, pf_i, 0)
    gb_r, gb_i = drive(pb_r, pb_i, 1)
    def read(idx, r):
        rr = l_re[idx, r].transpose(1, 3, 0, 2).reshape(ngrp, nst, t * cg)
        ri = -l_im[idx, r].transpose(1, 3, 0, 2).reshape(ngrp, nst, t * cg)
        return rr, ri

    of_r, of_i = read(np.arange(t) + 1, 0)
    ob_r, ob_i = read(t - np.arange(t), 1)

    eye = jnp.eye(2, dtype=BF16)
    g4 = jnp.stack([gf_r, gf_i, gb_r, gb_i], axis=2).astype(BF16).reshape(ngrp // 2, 2, t * cg, 4, 1, nst)
    gin = (g4 * eye[None, :, None, None, :, None]).reshape(ngrp // 2, 2 * t * cg, 8 * nst)
    o4 = jnp.stack([of_r, of_i, ob_r, ob_i], axis=1).astype(BF16).reshape(ngrp // 2, 2, 4, nst, 1, t * cg)
    o4 = o4.transpose(0, 2, 1, 3, 4, 5)
    gout = (o4 * eye[None, None, :, None, :, None]).reshape(ngrp // 2, 8 * nst, 2 * t * cg)
    a_chunk = jnp.stack([pr[t, 0], pi[t, 0], pr[t, 1], pi[t, 1]]).reshape(4, ngrp * nst)
    return mt, gin, gout, a_chunk


def _s5_kernel(z_ref, mt_ref, gin_ref, gout_ref, ac_ref, skip_ref, y_ref, d_scr, x_scr,
               *, n_lat, n_ctx, npairs, pw):
    nrows = n_lat + n_ctx
    gw = pw // 2
    tw = pw // 4
    for p in range(npairs):
        d = _dot(z_ref[0, :, p * pw:(p + 1) * pw], gin_ref[p])
        for j in range(4):
            d_scr[j, :, p * tw:(p + 1) * tw] = d[:, j * tw:(j + 1) * tw]

    def scan(plane, forward):
        ar = ac_ref[plane:plane + 1, :]
        ai = ac_ref[plane + 1:plane + 2, :]

        def body(i, carry):
            xr, xi = carry
            if forward:
                n = jnp.where(i < n_ctx, n_lat + i, i - n_ctx)
            else:
                n = nrows - 1 - i
            x_scr[plane, pl.ds(n, 1), :] = xr
            x_scr[plane + 1, pl.ds(n, 1), :] = xi
            dr = d_scr[plane, pl.ds(n, 1), :]
            di = d_scr[plane + 1, pl.ds(n, 1), :]
            return ar * xr - ai * xi + dr, ar * xi + ai * xr + di

        zero = jnp.zeros((1, d_scr.shape[2]), F32)
        lax.fori_loop(0, nrows, body, (zero, zero))

    scan(0, True)
    scan(2, False)

    for p in range(npairs):
        xp = jnp.concatenate([x_scr[j, :, p * tw:(p + 1) * tw] for j in range(4)], axis=1).astype(BF16)
        y = _dot(xp, gout_ref[p])
        for g in range(2):
            lo = p * pw + g * gw
            zg = z_ref[0, :, lo:lo + gw]
            yg = y[:, g * gw:(g + 1) * gw] + _dot(zg, mt_ref[2 * p + g]) + skip_ref[:, lo:lo + gw] * zg.astype(F32)
            y_ref[0, :, lo:lo + gw] = yg.astype(y_ref.dtype)


def _s5(zf, tables, layer, skip_flat, n_lat_chunks, n_ctx_chunks):
    bsz, nrows, wide = zf.shape
    mt, gin, gout, a_chunk = tables
    npairs, pw = gin.shape[1:3]
    planes = a_chunk.shape[2]
    return pl.pallas_call(
        functools.partial(_s5_kernel, n_lat=n_lat_chunks, n_ctx=n_ctx_chunks, npairs=npairs, pw=pw),
        grid=(bsz,),
        in_specs=[pl.BlockSpec((1, nrows, wide), lambda b: (b, 0, 0)),
                  _layer_spec(mt, layer), _layer_spec(gin, layer), _layer_spec(gout, layer),
                  _layer_spec(a_chunk, layer), _const_spec(skip_flat.shape)],
        out_specs=pl.BlockSpec((1, nrows, wide), lambda b: (b, 0, 0)),
        out_shape=jax.ShapeDtypeStruct((bsz, nrows, wide), BF16),
        scratch_shapes=[pltpu.VMEM((4, nrows, planes), F32), pltpu.VMEM((4, nrows, planes), F32)],
        compiler_params=pltpu.CompilerParams(
            dimension_semantics=("parallel",), vmem_limit_bytes=VMEM_LIMIT),
        name="s5",
    )(zf, mt, gin, gout, a_chunk, skip_flat)


def _gelu_tanh(x):
    return 0.5 * x * (1.0 + jnp.tanh(math.sqrt(2.0 / math.pi) * (x + 0.044715 * (x * x * x))))


def _merge_kernel(h_ref, yh_ref, yz_ref, gate_ref, mod_ref, whg_ref, wglu_ref, wbr_ref, wout_ref, o_ref, y_scr,
                  *, ngrp, cg):
    d = h_ref.shape[2]
    y_hg = _dot(yh_ref[0], whg_ref[...])
    ys = _gelu_tanh(_chunk_unflatten(yz_ref, y_scr, yz_ref.shape[1], ngrp, cg))
    glu = _sigmoid(_dot(ys.astype(BF16), wglu_ref[...]))
    y_s5 = _dot((ys * glu).astype(BF16), wbr_ref[...])
    gate = gate_ref[0].astype(F32)
    merged = _sigmoid(gate[:, :d]) * y_hg + _sigmoid(gate[:, d:]) * y_s5
    o_ref[0] = h_ref[0] + mod_ref[0, 2:3, :] * _dot(merged.astype(BF16), wout_ref[...])


def _merge(h, yh, yz, p_gate, mods, mod_row_fn, whg, wglu, wbr, wout, layer, ngrp, cg, tok_off_blocks):
    bsz, length, d = h.shape
    nt = length // TOK_TILE
    ct = TOK_TILE // S5_T
    seq = lambda rows, w: pl.BlockSpec((1, rows, w), lambda b, t: (b, t + tok_off_blocks, 0))
    return pl.pallas_call(
        functools.partial(_merge_kernel, ngrp=ngrp, cg=cg),
        grid=(bsz, nt),
        in_specs=[pl.BlockSpec((1, TOK_TILE, d), lambda b, t: (b, t, 0)),
                  seq(TOK_TILE, yh.shape[2]), seq(ct, yz.shape[2]), seq(TOK_TILE, p_gate.shape[2]),
                  pl.BlockSpec((1, 6, d), lambda b, t: (mod_row_fn(b), 0, 0)),
                  _layer_spec(whg, layer), _layer_spec(wglu, layer), _layer_spec(wbr, layer),
                  _layer_spec(wout, layer)],
        out_specs=pl.BlockSpec((1, TOK_TILE, d), lambda b, t: (b, t, 0)),
        out_shape=jax.ShapeDtypeStruct(h.shape, F32),
        scratch_shapes=[pltpu.VMEM((ngrp * cg // 128, TOK_TILE, 128), F32)],
        input_output_aliases={0: 0} if layer > 0 else {},
        compiler_params=pltpu.CompilerParams(
            dimension_semantics=("parallel", "parallel"), vmem_limit_bytes=VMEM_LIMIT),
        name="merge",
    )(h, yh, yz, p_gate, mods, whg, wglu, wbr, wout)


def _ffn_kernel(h_ref, mod_ref, g_ref, wa_ref, wb_ref, cw_ref, cb_ref, wd_ref, fg_ref,
                o_ref, xn_scr, up_scr, act_scr, *, n_tok, width, vertical, cw, mrows, final):
    j = pl.program_id(1)
    nj = pl.num_programs(1)
    nblk = n_tok // mrows
    rpb = mrows // width
    pad = width

    @pl.when(j == 0)
    def _():
        def norm_body(i, carry):
            r0 = pl.multiple_of(i * mrows, mrows)
            xn_scr[pl.ds(r0, mrows), :] = _modulated_norm(
                h_ref[0, pl.ds(r0, mrows), :], g_ref[...], mod_ref[0, 3:4, :], mod_ref[0, 4:5, :]).astype(BF16)
            o_ref[0, pl.ds(r0, mrows), :] = jnp.zeros((mrows, o_ref.shape[2]), F32)
            return carry

        lax.fori_loop(0, nblk, norm_body, 0)
        xn_scr[n_tok:n_tok + width, :] = jnp.zeros((width, xn_scr.shape[1]), BF16)
        for copy in range(3):
            up_scr[copy, 0:pad, :] = jnp.zeros((pad, 2 * cw), F32)
            up_scr[copy, pad + n_tok:pad + n_tok + width, :] = jnp.zeros((width, 2 * cw), F32)

    def up_rows(row0, m):
        xn = xn_scr[pl.ds(row0, m), :]
        up = jnp.concatenate([_dot(xn, wa_ref[...]), _dot(xn, wb_ref[...])], axis=1)
        col = lax.broadcasted_iota(jnp.int32, (m, 1), 0) % width
        up_scr[0, pl.ds(pad + row0, m), :] = jnp.where(col == 0, 0.0, pltpu.roll(up, 1, axis=0))
        up_scr[1, pl.ds(pad + row0, m), :] = up
        up_scr[2, pl.ds(pad + row0, m), :] = jnp.where(col == width - 1, 0.0, pltpu.roll(up, m - 1, axis=0))

    def conv_row(row0, lane0):
        acc = cb_ref[:, lane0:lane0 + 128]
        for dr in ((-1, 0, 1) if vertical else (0,)):
            for dc in (-1, 0, 1):
                tap = (dr + 1) * 3 + (dc + 1)
                src = up_scr[dc + 1, pl.ds(pl.multiple_of(pad + row0 + dr * width, 8), width), lane0:lane0 + 128]
                acc = acc + cw_ref[tap:tap + 1, lane0:lane0 + 128] * src
        return acc

    def conv_block(blk):
        r0 = blk * mrows
        for lt in range(cw // 128):
            for r in range(rpb):
                row0 = r0 + r * width
                ca = conv_row(row0, lt * 128)
                cb = conv_row(row0, cw + lt * 128)
                act_scr[pl.ds(pl.multiple_of(row0, 16), width), lt * 128:(lt + 1) * 128] = (
                    ca * _sigmoid(ca) * cb).astype(BF16)

    def down_rows(row0):
        o_ref[0, pl.ds(row0, mrows), :] += _dot(act_scr[pl.ds(row0, mrows), :], wd_ref[...])

    if nblk == 1:
        up_rows(0, mrows)
        conv_block(0)
        down_rows(0)
    else:
        streams = math.gcd(FFN_STREAMS, nblk)
        per = nblk // streams
        ahead = width if vertical else 0
        if vertical:
            up_rows(0, width)
            for s in range(1, streams):
                up_rows(s * per * mrows - width, 2 * width)

        def body(k, carry):
            for s in range(streams):
                r0 = pl.multiple_of((k + s * per) * mrows, mrows)
                up_rows(r0 + ahead, mrows)
            for s in range(streams):
                conv_block(k + s * per)
            for s in range(streams):
                down_rows(pl.multiple_of((k + s * per) * mrows, mrows))
            return carry

        lax.fori_loop(0, per, body, 0)

    @pl.when(j == nj - 1)
    def _():
        def fin_body(i, carry):
            r0 = pl.multiple_of(i * mrows, mrows)
            out = h_ref[0, pl.ds(r0, mrows), :] + mod_ref[0, 5:6, :] * o_ref[0, pl.ds(r0, mrows), :]
            if final:
                out = out * lax.rsqrt(jnp.mean(out * out, axis=-1, keepdims=True) + RMS_EPS) * fg_ref[...]
            o_ref[0, pl.ds(r0, mrows), :] = out
            return carry

        lax.fori_loop(0, nblk, fin_body, 0)


def _ffn(h, mods, mod_row_fn, norm_g, ffn_w, layer, final_g, width, vertical, final):
    bsz, n_tok, d = h.shape
    w_pad, cwf, cbf, wd = ffn_w
    cw = FFN_TILE
    nj = wd.shape[1] // cw
    return pl.pallas_call(
        functools.partial(_ffn_kernel, n_tok=n_tok, width=width, vertical=vertical, cw=cw, mrows=TOK_TILE,
                          final=final),
        grid=(bsz, nj),
        in_specs=[pl.BlockSpec((1, n_tok, d), lambda b, j: (b, 0, 0), pipeline_mode=pl.Buffered(1)),
                  pl.BlockSpec((1, 6, d), lambda b, j: (mod_row_fn(b), 0, 0)),
                  pl.BlockSpec((1, d), lambda b, j: (0, 0)),
                  pl.BlockSpec((None, d, cw), lambda b, j: (layer, 0, j)),
                  pl.BlockSpec((None, d, cw), lambda b, j: (layer, 0, nj + j)),
                  pl.BlockSpec((None, 9, 2 * cw), lambda b, j: (layer, 0, j)),
                  pl.BlockSpec((None, 1, 2 * cw), lambda b, j: (layer, 0, j)),
                  pl.BlockSpec((None, cw, d), lambda b, j: (layer, j, 0)),
                  pl.BlockSpec((1, d), lambda b, j: (0, 0))],
        out_specs=pl.BlockSpec((1, n_tok, d), lambda b, j: (b, 0, 0)),
        out_shape=jax.ShapeDtypeStruct(h.shape, F32),
        scratch_shapes=[pltpu.VMEM((n_tok + width, d), BF16),
                        pltpu.VMEM((3, n_tok + 2 * width, 2 * cw), F32),
                        pltpu.VMEM((n_tok, cw), BF16)],
        input_output_aliases={0: 0},
        compiler_params=pltpu.CompilerParams(
            dimension_semantics=("parallel", "arbitrary"), vmem_limit_bytes=VMEM_LIMIT),
        name="ffn",
    )(h, mods, norm_g.reshape(1, d), w_pad, w_pad, cwf, cbf, wd, final_g.reshape(1, d))


def _ffn_weights(w_up, conv_w, conv_b, w_down):
    depth, ffn, _ = w_down.shape
    cw = FFN_TILE
    fp = ((ffn + cw - 1) // cw) * cw
    nj = fp // cw

    def halves(x):
        padw = [(0, 0)] * (x.ndim - 1) + [(0, fp - ffn)]
        return jnp.pad(x[..., :ffn], padw), jnp.pad(x[..., ffn:], padw)

    def tiles(x):
        lead = x.shape[:-1]
        parts = [part.reshape(lead + (nj, cw)) for part in halves(x)]
        return jnp.concatenate(parts, axis=-1).reshape(lead + (nj * 2 * cw,))

    wd = jnp.pad(w_down.astype(BF16), [(0, 0), (0, fp - ffn), (0, 0)])
    w_pad = jnp.concatenate(halves(w_up.astype(BF16)), axis=-1)
    return w_pad, tiles(conv_w.reshape(depth, 9, 2 * ffn)), tiles(conv_b.reshape(depth, 1, 2 * ffn)), wd


def _lower_bounds(lb_raw):
    p = jax.nn.softmax(lb_raw.astype(F32), axis=0)
    cs = jnp.cumsum(p, axis=0)
    return cs - cs[0:1]


def kernel(x, c, ctx, c_ctx, ada_w, ada_b, norm1_g, w_in, hg_lb_raw, hg_norm_g, w_hg_br, s5_a_re, s5_a_im, s5_log_dt, s5_b_re, s5_b_im, s5_c_re, s5_c_im, s5_d, w_s5_glu, w_s5_br, w_out, norm2_g, w_up, ffn_conv_w, ffn_conv_b, w_down, final_g):
    bsz, n_lat, d = x.shape
    n_ctx = ctx.shape[1]
    depth = ada_w.shape[0]
    dk = hg_norm_g.shape[1]
    hg_w = hg_lb_raw.shape[2]
    ngrp, nst, cg = s5_b_re.shape[1:]
    assert n_lat % TOK_TILE == 0 and n_ctx % TOK_TILE == 0 and n_lat % GLA_C == 0 and n_ctx % GLA_C == 0
    assert 5 * hg_w + ngrp * cg + 2 * d == w_in.shape[2] and S5_T * cg == 256 and 128 % cg == 0
    assert ngrp % (128 // cg) == 0

    rows = ((bsz + 1 + 7) // 8) * 8
    cond = jnp.zeros((rows, d), F32).at[:bsz].set(c).at[bsz].set(c_ctx)
    mods = _ada_mod(cond, ada_w, ada_b).reshape(depth, rows, 6, d)
    lat_row = lambda b: b
    ctx_row = lambda b: bsz
    lbs = _lower_bounds(hg_lb_raw)
    lat_blk = n_lat // TOK_TILE

    w_in_b, whg_b, wglu_b, wbr_b, wout_b = (w.astype(BF16) for w in (w_in, w_hg_br, w_s5_glu, w_s5_br, w_out))
    ffn_w = _ffn_weights(w_up, ffn_conv_w, ffn_conv_b, w_down)
    s5_tabs = jax.vmap(_s5_tables)(s5_a_re, s5_a_im, s5_log_dt, s5_b_re, s5_b_im, s5_c_re, s5_c_im)
    skip = jnp.broadcast_to(s5_d.astype(F32).reshape(depth, ngrp, 1, cg), (depth, ngrp, S5_T, cg))
    skip = skip.reshape(depth, 1, ngrp * S5_T * cg)

    h_lat, h_ctx = x, ctx
    for l in range(depth):
        last = l == depth - 1
        p_hg, z, p_gate = _proj_in(h_lat, h_ctx, mods[l], norm1_g[l], w_in_b, l, 5 * hg_w, ngrp, cg)

        o_fwd = _gla(p_hg, lbs[l, 0], hg_norm_g[l], None, n_lat, n_ctx, hg_w, dk, rev=False, ctx_out=not last)
        yh = _gla(p_hg, lbs[l, 1], hg_norm_g[l], o_fwd, n_lat, n_ctx, hg_w, dk, rev=True, ctx_out=not last)
        yz = _s5(z, s5_tabs, l, skip[l], n_lat // S5_T, n_ctx // S5_T)

        wts = (whg_b, wglu_b, wbr_b, wout_b, l, ngrp, cg)
        h_lat = _merge(h_lat, yh, yz, p_gate, mods[l], lat_row, *wts, 0)
        h_lat = _ffn(h_lat, mods[l], lat_row, norm2_g[l], ffn_w, l, final_g, GRID_W, True, last)
        if not last:
            h_ctx = _merge(h_ctx, yh, yz, p_gate, mods[l], ctx_row, *wts, lat_blk)
            stack = math.gcd(bsz, max(1, n_lat // n_ctx))
            h_ctx = _ffn(h_ctx.reshape(bsz // stack, stack * n_ctx, d), mods[l], ctx_row, norm2_g[l], ffn_w, l,
                         final_g, n_ctx, False, False).reshape(bsz, n_ctx, d)
    return h_lat
```
